```python
import math
import jax, jax.numpy as jnp
from jax import lax
import numpy as np

D_MODEL = 1024
BATCH = 8
SEQ = 2048
DEPTH = 2
DEC_BATCH = 128
DEC_SEQ = 1
PAST_LEN = 16384
PAGE_SIZE = 128

N_RET_LAYERS = (DEPTH + 1) // 2
N_HG_LAYERS = DEPTH // 2
RET_HEADS = 4
RET_DK = D_MODEL // RET_HEADS
RET_DV = 2 * D_MODEL // RET_HEADS
RET_QK = RET_HEADS * RET_DK
RET_V = RET_HEADS * RET_DV
HG_EXPAND = 128
HG_HEADS = D_MODEL // HG_EXPAND
HG_DK = HG_EXPAND
HG_DV = D_MODEL // HG_HEADS
HG_W = HG_HEADS * HG_DK
D_FF = 2816
N_EXPERTS = 8
TOP_K = 2
D_FF_EXPERT = 3584
CHUNK = 64
ROPE_THETA = 10000.0
NORM_EPS = 1e-6
GN_EPS = 1e-5

kernel_name = "retnet_hgrn2_hybrid_moe_step"

F32 = jnp.float32


def rmsnorm(x, g, eps=NORM_EPS):
    xf = x.astype(F32)
    y = xf * lax.rsqrt(jnp.mean(xf * xf, axis=-1, keepdims=True) + eps)
    return (y * g.astype(F32)).astype(x.dtype)


def rotary(x, pos):
    half = x.shape[-1] // 2
    inv = jnp.power(ROPE_THETA, -jnp.arange(half, dtype=F32) / half)
    ang = pos[:, None] * inv[None, :]
    cos = jnp.cos(ang)[None, :, None, :]
    sin = jnp.sin(ang)[None, :, None, :]
    x1, x2 = x[..., :half], x[..., half:]
    return jnp.concatenate([x1 * cos - x2 * sin, x2 * cos + x1 * sin], axis=-1)


def to_chunks(a, c):
    b, l = a.shape[0], a.shape[1]
    return jnp.moveaxis(a.reshape(b, l // c, c, *a.shape[2:]), 1, 0)


def from_chunks(a):
    a = jnp.moveaxis(a, 0, 1)
    return a.reshape(a.shape[0], a.shape[1] * a.shape[2], *a.shape[3:])


def retention_scan(q, k, v, s0):
    l, h = q.shape[1], q.shape[2]
    c = math.gcd(l, CHUNK)
    log_gamma = jnp.log1p(-jnp.exp2(-5.0 - jnp.arange(h, dtype=F32)))
    t = jnp.arange(c, dtype=F32)
    rel = t[:, None] - t[None, :]
    causal = rel >= 0
    decay_mask = jnp.where(causal[None], jnp.exp(jnp.where(causal, rel, 0.0)[None] * log_gamma[:, None, None]), 0.0)
    decay_in = jnp.exp((t + 1.0)[:, None] * log_gamma[None, :])
    decay_out = jnp.exp((c - 1.0 - t)[:, None] * log_gamma[None, :])
    decay_chunk = jnp.exp(c * log_gamma)

    def step(s, xs):
        qc, kc, vc = xs
        scores = jnp.einsum('bthd,bshd->bhts', qc, kc) * decay_mask[None]
        o = jnp.einsum('bhts,bshe->bthe', scores, vc) + jnp.einsum('bthd,bhde->bthe', qc * decay_in[None, :, :, None], s)
        s_new = s * decay_chunk[None, :, None, None] + jnp.einsum('bshd,bshe->bhde', kc * decay_out[None, :, :, None], vc)
        return s_new, o

    s_fin, o = lax.scan(step, s0, (to_chunks(q, c), to_chunks(k, c), to_chunks(v, c)))
    return from_chunks(o), s_fin


def hgrn_scan(q, k, logf, i, s0):
    l = q.shape[1]
    c = math.gcd(l, CHUNK)
    tri = jnp.tril(jnp.ones((c, c), dtype=bool))

    def step(s, xs):
        qc, kc, gc, ic = xs
        b = jnp.cumsum(gc, axis=1)
        diff = b[:, :, None] - b[:, None, :]
        w = jnp.exp(jnp.where(tri[None, :, :, None, None], diff, -jnp.inf))
        a = jnp.einsum('bthd,btshd->bhts', qc, w * kc[:, None])
        o = jnp.einsum('bhts,bshe->bthe', a, ic) + jnp.einsum('bthd,bhde->bthe', qc * jnp.exp(b), s)
        b_last = b[:, -1]
        s_new = s * jnp.exp(b_last)[..., None] + jnp.einsum('bshd,bshe->bhde', kc * jnp.exp(b_last[:, None] - b), ic)
        return s_new, o

    xs = (to_chunks(q, c), to_chunks(k, c), to_chunks(logf, c), to_chunks(i, c))
    s_fin, o = lax.scan(step, s0, xs)
    return from_chunks(o), s_fin


def retention_mixer(h, w_in, gn_g, w_out, s0, pos):
    bsz, l, _ = h.shape
    proj = h @ w_in
    q, k, v, g = jnp.split(proj, [RET_QK, 2 * RET_QK, 2 * RET_QK + RET_V], axis=-1)
    q = rotary(q.reshape(bsz, l, RET_HEADS, RET_DK).astype(F32), pos)
    k = rotary(k.reshape(bsz, l, RET_HEADS, RET_DK).astype(F32), pos) * (RET_DK ** -0.5)
    v = v.reshape(bsz, l, RET_HEADS, RET_DV).astype(F32)
    o, s = retention_scan(q, k, v, s0.astype(F32))
    mu = jnp.mean(o, axis=-1, keepdims=True)
    var = jnp.mean(jnp.square(o - mu), axis=-1, keepdims=True)
    on = (o - mu) * lax.rsqrt(var + GN_EPS) * gn_g.astype(F32).reshape(RET_HEADS, RET_DV)
    o = (jax.nn.silu(g.astype(F32)) * on.reshape(bsz, l, RET_V)).astype(h.dtype)
    return o @ w_out, s


def hgrn_mixer(h, w_in, lb, norm_g, w_out, s0):
    bsz, l, _ = h.shape
    proj = h @ w_in
    q, f, i, g = jnp.split(proj, 4, axis=-1)
    q = jax.nn.silu(q.astype(F32)).reshape(bsz, l, HG_HEADS, HG_DK)
    forget = lb[None, None, :] + (1.0 - lb[None, None, :]) * jax.nn.sigmoid(f.astype(F32))
    k = (1.0 - forget).reshape(bsz, l, HG_HEADS, HG_DK)
    logf = jnp.log(forget).reshape(bsz, l, HG_HEADS, HG_DK)
    i = i.astype(F32).reshape(bsz, l, HG_HEADS, HG_DV)
    o, s = hgrn_scan(q, k, logf, i, s0.astype(F32))
    on = o * lax.rsqrt(jnp.mean(o * o, axis=-1, keepdims=True) + NORM_EPS) * norm_g.astype(F32).reshape(HG_HEADS, HG_DV)
    o = (jax.nn.silu(g.astype(F32)) * on.reshape(bsz, l, HG_HEADS * HG_DV)).astype(h.dtype)
    return o @ w_out, s


def swiglu(h, wg, wu, wd):
    return (jax.nn.silu(h @ wg) * (h @ wu)) @ wd


def moe_swiglu(h, router, wg, wu, wd):
    probs = jax.nn.softmax((h @ router).astype(F32), axis=-1)
    vals, idx = lax.top_k(probs, TOP_K)
    vals = vals / jnp.sum(vals, axis=-1, keepdims=True)
    combine = jnp.sum(jax.nn.one_hot(idx, N_EXPERTS, dtype=F32) * vals[..., None], axis=-2)
    out = jnp.zeros(h.shape, F32)
    for e in range(N_EXPERTS):
        out = out + combine[..., e:e + 1] * swiglu(h, wg[e], wu[e], wd[e]).astype(F32)
    return out.astype(h.dtype)


def trunk(x, ret_states, hg_states, pos, state_dtype,
          norm_mix_g, norm_ffn_g, final_norm_g, ret_w_in, ret_gn_g, ret_w_out,
          hg_w_in, hg_lb_param, hg_norm_g, hg_w_out, ffn_w_gate, ffn_w_up, ffn_w_down,
          moe_router, moe_w_gate, moe_w_up, moe_w_down):
    p = jax.nn.softmax(hg_lb_param.astype(F32), axis=0)
    lbs = jnp.cumsum(p, axis=0) - p[0:1]
    new_ret, new_hg = [], []
    for layer in range(DEPTH):
        j = layer // 2
        h = rmsnorm(x, norm_mix_g[layer])
        if layer % 2 == 0:
            out, s = retention_mixer(h, ret_w_in[j], ret_gn_g[j], ret_w_out[j], ret_states[j], pos)
            new_ret.append(s.astype(state_dtype))
        else:
            out, s = hgrn_mixer(h, hg_w_in[j], lbs[layer], hg_norm_g[j], hg_w_out[j], hg_states[j])
            new_hg.append(s.astype(state_dtype))
        x = x + out
        h = rmsnorm(x, norm_ffn_g[layer])
        if layer % 2 == 0:
            x = x + swiglu(h, ffn_w_gate[j], ffn_w_up[j], ffn_w_down[j])
        else:
            x = x + moe_swiglu(h, moe_router[j], moe_w_gate[j], moe_w_up[j], moe_w_down[j])
    return rmsnorm(x, final_norm_g), jnp.stack(new_ret), jnp.stack(new_hg)


def setup_inputs(seed: int = 0) -> dict:
    key = jax.random.key(seed)
    ks = jax.random.split(key, 24)
    nrm = lambda k, shape, s: jax.random.normal(k, shape, F32) * s
    d = D_MODEL
    return {
        "x_prompt": nrm(ks[0], (BATCH, SEQ, d), 1.0),
        "x_sample": nrm(ks[1], (DEC_BATCH, DEC_SEQ, d), 1.0),
        "state_retention": nrm(ks[2], (N_RET_LAYERS, DEC_BATCH, RET_HEADS, RET_DK, RET_DV), 0.1),
        "state_hgrn": nrm(ks[3], (N_HG_LAYERS, DEC_BATCH, HG_HEADS, HG_DK, HG_DV), 0.5),
        "norm_mix_g": 1.0 + nrm(ks[4], (DEPTH, d), 0.02),
        "norm_ffn_g": 1.0 + nrm(ks[5], (DEPTH, d), 0.02),
        "final_norm_g": 1.0 + nrm(ks[6], (d,), 0.02),
        "ret_w_in": nrm(ks[7], (N_RET_LAYERS, d, 2 * RET_QK + 2 * RET_V), d ** -0.5),
        "ret_gn_g": 1.0 + nrm(ks[8], (N_RET_LAYERS, RET_V), 0.02),
        "ret_w_out": nrm(ks[9], (N_RET_LAYERS, RET_V, d), RET_V ** -0.5),
        "hg_w_in": nrm(ks[10], (N_HG_LAYERS, d, 4 * HG_W), d ** -0.5),
        "hg_lb_param": 1.0 + nrm(ks[11], (DEPTH, HG_W), 0.1),
        "hg_norm_g": 1.0 + nrm(ks[12], (N_HG_LAYERS, HG_HEADS * HG_DV), 0.02),
        "hg_w_out": nrm(ks[13], (N_HG_LAYERS, HG_HEADS * HG_DV, d), (HG_HEADS * HG_DV) ** -0.5),
        "ffn_w_gate": nrm(ks[14], (N_RET_LAYERS, d, D_FF), d ** -0.5),
        "ffn_w_up": nrm(ks[15], (N_RET_LAYERS, d, D_FF), d ** -0.5),
        "ffn_w_down": nrm(ks[16], (N_RET_LAYERS, D_FF, d), D_FF ** -0.5),
        "moe_router": nrm(ks[17], (N_HG_LAYERS, d, N_EXPERTS), d ** -0.5),
        "moe_w_gate": nrm(ks[18], (N_HG_LAYERS, N_EXPERTS, d, D_FF_EXPERT), d ** -0.5),
        "moe_w_up": nrm(ks[19], (N_HG_LAYERS, N_EXPERTS, d, D_FF_EXPERT), d ** -0.5),
        "moe_w_down": nrm(ks[20], (N_HG_LAYERS, N_EXPERTS, D_FF_EXPERT, d), D_FF_EXPERT ** -0.5),
    }


def reference(x_prompt, x_sample, state_retention, state_hgrn, norm_mix_g, norm_ffn_g, final_norm_g,
              ret_w_in, ret_gn_g, ret_w_out, hg_w_in, hg_lb_param, hg_norm_g, hg_w_out,
              ffn_w_gate, ffn_w_up, ffn_w_down, moe_router, moe_w_gate, moe_w_up, moe_w_down):
    weights = (norm_mix_g, norm_ffn_g, final_norm_g, ret_w_in, ret_gn_g, ret_w_out,
               hg_w_in, hg_lb_param, hg_norm_g, hg_w_out, ffn_w_gate, ffn_w_up, ffn_w_down,
               moe_router, moe_w_gate, moe_w_up, moe_w_down)
    bp, lp = x_prompt.shape[0], x_prompt.shape[1]
    ls = x_sample.shape[1]
    ret0 = jnp.zeros((N_RET_LAYERS, bp, RET_HEADS, RET_DK, RET_DV), F32)
    hg0 = jnp.zeros((N_HG_LAYERS, bp, HG_HEADS, HG_DK, HG_DV), F32)
    pos_p = jnp.arange(lp, dtype=F32)
    y_prompt, ret_p, hg_p = trunk(x_prompt, ret0, hg0, pos_p, x_prompt.dtype, *weights)
    pos_s = PAST_LEN + jnp.arange(ls, dtype=F32)
    y_sample, ret_s, hg_s = trunk(x_sample, state_retention, state_hgrn, pos_s, state_retention.dtype, *weights)
    return (y_prompt, y_sample, ret_p, ret_s, hg_p, hg_s)
```

```python
import functools

import jax
import jax.numpy as jnp
from jax import lax
from jax.experimental import pallas as pl
from jax.experimental.pallas import tpu as pltpu

F32 = jnp.float32
BF16 = jnp.bfloat16
I32 = jnp.int32

NORM_EPS = 1e-6
GN_EPS = 1e-5
ROPE_THETA = 10000.0
PAST_LEN = 16384
TOP_K = 2

LANES = 128
VMEM_LIMIT = 56 * 1024 * 1024

RET_CHUNK = 256
HG_CHUNK = 128
HG_SUB = 16
MOE_ROW_TILE = 1024
MOE_SUB = 256
MOE_F_TILE = 512

NT_DIMS = (((1,), (1,)), ((), ()))
TN_DIMS = (((0,), (0,)), ((), ()))


def _cparams(*sem):
    return pltpu.CompilerParams(dimension_semantics=sem, vmem_limit_bytes=VMEM_LIMIT)


def _pick_tile(n, target, mult):
    best = None
    for t in range(mult, min(n, target) + 1, mult):
        if n % t == 0:
            best = t
    assert best is not None, (n, target, mult)
    return best


def _sigmoid(x):
    return 1.0 / (1.0 + jnp.exp(-x))


def _silu(x):
    return x * _sigmoid(x)


def _rms(x, g):
    return x * lax.rsqrt(jnp.mean(x * x, axis=-1, keepdims=True) + NORM_EPS) * g


def _row_group(b):
    return pl.multiple_of(lax.shift_left(lax.shift_right_logical(b, 3), 3), 8), b & 7


def _load_row(ref, b, cols):
    base, r = _row_group(b)
    blk = ref[pl.ds(base, 8), cols]
    rows = lax.broadcasted_iota(I32, blk.shape, 0)
    return jnp.sum(jnp.where(rows == r, blk, 0.0), axis=0, keepdims=True)


def _store_row(ref, b, cols, row):
    base, r = _row_group(b)
    blk = ref[pl.ds(base, 8), cols]
    rows = lax.broadcasted_iota(I32, blk.shape, 0)
    ref[pl.ds(base, 8), cols] = jnp.where(rows == r, row, blk)


def _norm_matmul_kernel(x_ref, g_ref, w_ref, o_ref, h_ref):
    @pl.when(pl.program_id(1) == 0)
    def _():
        h_ref[...] = _rms(x_ref[...], g_ref[...]).astype(BF16)

    o_ref[...] = jnp.dot(h_ref[...], w_ref[...], preferred_element_type=F32)


def norm_matmul(x, g, w, tm, tn):
    t, d = x.shape
    n = w.shape[1]
    return pl.pallas_call(
        _norm_matmul_kernel,
        grid=(t // tm, n // tn),
        in_specs=[pl.BlockSpec((tm, d), lambda i, j: (i, 0)),
                  pl.BlockSpec((1, d), lambda i, j: (0, 0)),
                  pl.BlockSpec((d, tn), lambda i, j: (0, j))],
        out_specs=pl.BlockSpec((tm, tn), lambda i, j: (i, j)),
        out_shape=jax.ShapeDtypeStruct((t, n), F32),
        scratch_shapes=[pltpu.VMEM((tm, d), BF16)],
        compiler_params=_cparams("parallel", "arbitrary"),
        name="norm_matmul",
    )(x, g.reshape(1, d), w)


def _matmul_res_kernel(a_ref, w_ref, x_ref, o_ref):
    o_ref[...] = x_ref[...] + jnp.dot(a_ref[...], w_ref[...], preferred_element_type=F32)


def matmul_residual(a, w, x, tm):
    t, k = a.shape
    n = w.shape[1]
    return pl.pallas_call(
        _matmul_res_kernel,
        grid=(t // tm,),
        in_specs=[pl.BlockSpec((tm, k), lambda i: (i, 0)),
                  pl.BlockSpec((k, n), lambda i: (0, 0)),
                  pl.BlockSpec((tm, n), lambda i: (i, 0))],
        out_specs=pl.BlockSpec((tm, n), lambda i: (i, 0)),
        out_shape=jax.ShapeDtypeStruct((t, n), F32),
        compiler_params=_cparams("parallel"),
        name="matmul_residual",
    )(a, w, x)


def _ffn_kernel(x_ref, g_ref, wg_ref, wu_ref, wd_ref, o_ref, h_ref):
    f = pl.program_id(1)

    @pl.when(f == 0)
    def _():
        x = x_ref[...]
        h_ref[...] = _rms(x, g_ref[...]).astype(BF16)
        o_ref[...] = x

    h = h_ref[...]
    a = jnp.dot(h, wg_ref[...], preferred_element_type=F32)
    u = jnp.dot(h, wu_ref[...], preferred_element_type=F32)
    act = (_silu(a) * u).astype(BF16)
    o_ref[...] += jnp.dot(act, wd_ref[...], preferred_element_type=F32)


def ffn_residual(x, g, wg, wu, wd, tm, tf):
    t, d = x.shape
    f = wg.shape[1]
    return pl.pallas_call(
        _ffn_kernel,
        grid=(t // tm, f // tf),
        in_specs=[pl.BlockSpec((tm, d), lambda i, j: (i, 0)),
                  pl.BlockSpec((1, d), lambda i, j: (0, 0)),
                  pl.BlockSpec((d, tf), lambda i, j: (0, j)),
                  pl.BlockSpec((d, tf), lambda i, j: (0, j)),
                  pl.BlockSpec((tf, d), lambda i, j: (j, 0))],
        out_specs=pl.BlockSpec((tm, d), lambda i, j: (i, 0)),
        out_shape=jax.ShapeDtypeStruct((t, d), F32),
        scratch_shapes=[pltpu.VMEM((tm, d), BF16)],
        compiler_params=_cparams("parallel", "arbitrary"),
        name="ffn_residual",
    )(x, g.reshape(1, d), wg, wu, wd)


def _rotary(x, cos, sin, half):
    x1 = x[:, :half]
    x2 = x[:, half:]
    return jnp.concatenate([x1 * cos - x2 * sin, x2 * cos + x1 * sin], axis=-1)


def _group_norm_gate(o, gate, gain):
    mu = jnp.mean(o, axis=-1, keepdims=True)
    d = o - mu
    var = jnp.mean(d * d, axis=-1, keepdims=True)
    return _silu(gate) * (d * lax.rsqrt(var + GN_EPS) * gain)


def _ret_scan_kernel(lg_ref, q_ref, k_ref, v_ref, g_ref, cos_ref, sin_ref, gn_ref,
                     o_ref, sfin_ref, s_ref, *, chunk, dk, dv):
    c = pl.program_id(2)

    @pl.when(c == 0)
    def _():
        s_ref[...] = jnp.zeros_like(s_ref)

    lg = lg_ref[pl.program_id(1)]
    half = dk // 2
    cos = cos_ref[...]
    sin = sin_ref[...]
    q = _rotary(q_ref[...], cos, sin, half)
    k = _rotary(k_ref[...], cos, sin, half) * (dk ** -0.5)
    v = v_ref[...].astype(BF16)

    t = lax.broadcasted_iota(I32, (chunk, 1), 0).astype(F32)
    decay_in = jnp.exp((t + 1.0) * lg)
    decay_out = jnp.exp((chunk - 1.0 - t) * lg)
    ti = lax.broadcasted_iota(I32, (chunk, chunk), 0)
    si = lax.broadcasted_iota(I32, (chunk, chunk), 1)
    causal = ti >= si
    rel = (ti - si).astype(F32)
    mask = jnp.where(causal, jnp.exp(jnp.where(causal, rel, 0.0) * lg), 0.0)
    decay_chunk = jnp.exp(jnp.zeros((1, dv), F32) + chunk * lg)

    s = s_ref[...]
    scores = lax.dot_general(q.astype(BF16), k.astype(BF16), NT_DIMS, preferred_element_type=F32) * mask
    o = (jnp.dot(scores.astype(BF16), v, preferred_element_type=F32)
         + jnp.dot((q * decay_in).astype(BF16), s.astype(BF16), preferred_element_type=F32))
    kd_t = (k * decay_out).T.astype(BF16)
    s_new = s * decay_chunk + jnp.dot(kd_t, v, preferred_element_type=F32)
    s_ref[...] = s_new

    o_ref[...] = _group_norm_gate(o, g_ref[...], gn_ref[...]).astype(BF16)

    @pl.when(c == pl.num_programs(2) - 1)
    def _():
        sfin_ref[0, 0] = s_new


def retention_scan(proj, log_gamma, cos, sin, gn_g, *, batch, seq, heads, dk, dv, rows_total):
    chunk = min(RET_CHUNK, seq)
    nc = seq // chunk
    kern = functools.partial(_ret_scan_kernel, chunk=chunk, dk=dk, dv=dv)
    qk_blocks = heads * dk // dv
    return pl.pallas_call(
        kern,
        grid=(batch, heads, nc),
        in_specs=[pl.BlockSpec(memory_space=pltpu.SMEM),
                  pl.BlockSpec((chunk, dk), lambda b, h, c: (b * nc + c, h)),
                  pl.BlockSpec((chunk, dk), lambda b, h, c: (b * nc + c, heads + h)),
                  pl.BlockSpec((chunk, dv), lambda b, h, c: (b * nc + c, 2 * qk_blocks + h)),
                  pl.BlockSpec((chunk, dv), lambda b, h, c: (b * nc + c, 2 * qk_blocks + heads + h)),
                  pl.BlockSpec((chunk, dk // 2), lambda b, h, c: (c, 0)),
                  pl.BlockSpec((chunk, dk // 2), lambda b, h, c: (c, 0)),
                  pl.BlockSpec((1, dv), lambda b, h, c: (0, h))],
        out_specs=[pl.BlockSpec((chunk, dv), lambda b, h, c: (b * nc + c, h)),
                   pl.BlockSpec((1, 1, dk, dv), lambda b, h, c: (b, h, 0, 0))],
        out_shape=[jax.ShapeDtypeStruct((rows_total, heads * dv), BF16),
                   jax.ShapeDtypeStruct((batch, heads, dk, dv), F32)],
        scratch_shapes=[pltpu.VMEM((dk, dv), F32)],
        compiler_params=_cparams("parallel", "parallel", "arbitrary"),
        name="retention_scan",
    )(log_gamma, proj, proj, proj, proj, cos, sin, gn_g.reshape(1, heads * dv))


def _ret_step_kernel(lg_ref, q_ref, k_ref, v_ref, g_ref, cos_ref, sin_ref, gn_ref, s_ref, prev_ref,
                     o_ref, snew_ref, qt_ref, kt_ref, orow_ref, *, heads, dk, dv, nb):
    del prev_ref
    b = pl.program_id(0)
    half = dk // 2

    @pl.when(b == 0)
    def _():
        orow_ref[...] = jnp.zeros_like(orow_ref)
        cos = cos_ref[...]
        sin = sin_ref[...]
        for h in range(heads):
            qh = _rotary(q_ref[:, h * dk:(h + 1) * dk], cos, sin, half)
            kh = _rotary(k_ref[:, h * dk:(h + 1) * dk], cos, sin, half) * (dk ** -0.5)
            qt_ref[h] = qh.T
            kt_ref[h] = kh.T

    sel = lax.broadcasted_iota(I32, (dk, nb), 1) == b
    for h in range(heads):
        cols = slice(h * dv, (h + 1) * dv)
        kcol = jnp.sum(jnp.where(sel, kt_ref[h], 0.0), axis=1, keepdims=True)
        qcol = jnp.sum(jnp.where(sel, qt_ref[h], 0.0), axis=1, keepdims=True)
        vrow = _load_row(v_ref, b, cols)
        gamma = jnp.exp(jnp.zeros((1, dv), F32) + lg_ref[h])
        s_new = s_ref[0, h] * gamma + kcol * vrow
        snew_ref[0, h] = s_new
        o = jnp.sum(s_new * qcol, axis=0, keepdims=True)
        _store_row(orow_ref, b, cols, _group_norm_gate(o, _load_row(g_ref, b, cols), gn_ref[:, cols]))

    @pl.when(b == pl.num_programs(0) - 1)
    def _():
        o_ref[...] = orow_ref[...].astype(BF16)


def retention_step(proj, gated, state, log_gamma, cos, sin, gn_g, *, row0, nb, heads, dk, dv):
    kern = functools.partial(_ret_step_kernel, heads=heads, dk=dk, dv=dv, nb=nb)
    rb = row0 // nb
    wq = heads * dk
    wv = heads * dv
    return pl.pallas_call(
        kern,
        grid=(nb,),
        in_specs=[pl.BlockSpec(memory_space=pltpu.SMEM),
                  pl.BlockSpec((nb, wq), lambda b: (rb, 0)),
                  pl.BlockSpec((nb, wq), lambda b: (rb, 1)),
                  pl.BlockSpec((nb, wv), lambda b: (rb, 2 * wq // wv)),
                  pl.BlockSpec((nb, wv), lambda b: (rb, 2 * wq // wv + 1)),
                  pl.BlockSpec((1, dk // 2), lambda b: (0, 0)),
                  pl.BlockSpec((1, dk // 2), lambda b: (0, 0)),
                  pl.BlockSpec((1, wv), lambda b: (0, 0)),
                  pl.BlockSpec((1, heads, dk, dv), lambda b: (b, 0, 0, 0)),
                  pl.BlockSpec(memory_space=pl.ANY)],
        out_specs=[pl.BlockSpec((nb, wv), lambda b: (rb, 0)),
                   pl.BlockSpec((1, heads, dk, dv), lambda b: (b, 0, 0, 0))],
        out_shape=[jax.ShapeDtypeStruct(gated.shape, gated.dtype),
                   jax.ShapeDtypeStruct(state.shape, F32)],
        scratch_shapes=[pltpu.VMEM((heads, dk, nb), F32),
                        pltpu.VMEM((heads, dk, nb), F32),
                        pltpu.VMEM((nb, wv), F32)],
        input_output_aliases={9: 0},
        compiler_params=_cparams("arbitrary"),
        name="retention_step",
    )(log_gamma, proj, proj, proj, proj, cos, sin, gn_g.reshape(1, wv), state, gated)


def _lower_bound(lbp, layer):
    m = jnp.max(lbp, axis=0, keepdims=True)
    e = jnp.exp(lbp - m)
    p = e / jnp.sum(e, axis=0, keepdims=True)
    return jnp.sum(p[:layer + 1], axis=0, keepdims=True) - p[0:1]


def _split_dot(mat_bf16, x):
    hi = x.astype(BF16)
    r1 = x - hi.astype(F32)
    mid = r1.astype(BF16)
    lo = (r1 - mid.astype(F32)).astype(BF16)
    return (jnp.dot(mat_bf16, hi, preferred_element_type=F32)
            + jnp.dot(mat_bf16, mid, preferred_element_type=F32)
            + jnp.dot(mat_bf16, lo, preferred_element_type=F32))


def _rms_gate(o, gate, gain):
    return _silu(gate) * (o * lax.rsqrt(jnp.mean(o * o, axis=-1, keepdims=True) + NORM_EPS) * gain)


def _hgrn_scan_kernel(lbp_ref, q_ref, f_ref, i_ref, g_ref, ng_ref, o_ref, sfin_ref, st_ref,
                      *, chunk, heads, dk, sub, layer):
    c = pl.program_id(1)

    @pl.when(c == 0)
    def _():
        st_ref[...] = jnp.zeros_like(st_ref)

    lb = _lower_bound(lbp_ref[...], layer)
    qs = _silu(q_ref[...])
    forget = lb + (1.0 - lb) * _sigmoid(f_ref[...])
    kk = 1.0 - forget
    logf = jnp.log(forget)

    ri = lax.broadcasted_iota(I32, (chunk, chunk), 0)
    ci = lax.broadcasted_iota(I32, (chunk, chunk), 1)
    tril = jnp.where(ri >= ci, 1.0, 0.0).astype(BF16)
    bcum = _split_dot(tril, logf)
    blast = bcum[chunk - 1:chunk, :]
    qe = qs * jnp.exp(bcum)
    kdec = kk * jnp.exp(blast - bcum)
    iv = i_ref[...]
    gate = g_ref[...]
    gain = ng_ref[...]

    ones = jnp.ones((dk, LANES), BF16)
    lane_s = lax.broadcasted_iota(I32, (sub, LANES), 1)
    row_s = lax.broadcasted_iota(I32, (sub, LANES), 0)
    col_c = lax.broadcasted_iota(I32, (sub, chunk), 1)
    nsub = chunk // sub

    for h in range(heads):
        cols = slice(h * dk, (h + 1) * dk)
        qs_h = qs[:, cols]
        kk_h = kk[:, cols]
        b_h = bcum[:, cols]
        i_h = iv[:, cols].astype(BF16)
        strips = []
        for ib in range(nsub):
            r0 = ib * sub
            qb = qs_h[r0:r0 + sub]
            kb = kk_h[r0:r0 + sub]
            bb = b_h[r0:r0 + sub]
            ws = [qb * kb[s:s + 1] * jnp.exp(jnp.minimum(bb - bb[s:s + 1], 0.0)) for s in range(sub)]
            wstack = jnp.concatenate(ws, axis=0).astype(BF16)
            rsum = jnp.dot(wstack, ones, preferred_element_type=F32)
            adiag = jnp.zeros((sub, LANES), F32)
            for s in range(sub):
                adiag = adiag + jnp.where(lane_s == r0 + s, rsum[s * sub:(s + 1) * sub], 0.0)
            strip = jnp.where(lane_s - r0 <= row_s, adiag, 0.0)[:, :chunk]
            if ib > 0:
                bs = b_h[r0 - 1:r0]
                qt = (qb * jnp.exp(bb - bs)).astype(BF16)
                kt = (kk_h * jnp.exp(jnp.minimum(bs - b_h, 0.0))).astype(BF16)
                aoff = lax.dot_general(qt, kt, NT_DIMS, preferred_element_type=F32)
                strip = strip + jnp.where(col_c < r0, aoff, 0.0)
            strips.append(strip)
        a = jnp.concatenate(strips, axis=0).astype(BF16)
        st = st_ref[h]
        o_h = (jnp.dot(a, i_h, preferred_element_type=F32)
               + lax.dot_general(qe[:, cols].astype(BF16), st.astype(BF16), NT_DIMS,
                                 preferred_element_type=F32))
        st_new = (st * jnp.exp(blast[:, cols])
                  + jnp.dot(iv[:, cols].T.astype(BF16), kdec[:, cols].astype(BF16),
                            preferred_element_type=F32))
        st_ref[h] = st_new
        o_ref[:, cols] = _rms_gate(o_h, gate[:, cols], gain[:, cols]).astype(BF16)

        @pl.when(c == pl.num_programs(1) - 1)
        def _():
            sfin_ref[0, h] = st_new.T


def hgrn_scan(proj, lb_param, norm_g, *, batch, seq, heads, dk, rows_total, layer):
    chunk = min(HG_CHUNK, seq)
    assert chunk == LANES and chunk % HG_SUB == 0
    nc = seq // chunk
    w = heads * dk
    kern = functools.partial(_hgrn_scan_kernel, chunk=chunk, heads=heads, dk=dk, sub=HG_SUB, layer=layer)
    return pl.pallas_call(
        kern,
        grid=(batch, nc),
        in_specs=[pl.BlockSpec(lb_param.shape, lambda b, c: (0, 0)),
                  pl.BlockSpec((chunk, w), lambda b, c: (b * nc + c, 0)),
                  pl.BlockSpec((chunk, w), lambda b, c: (b * nc + c, 1)),
                  pl.BlockSpec((chunk, w), lambda b, c: (b * nc + c, 2)),
                  pl.BlockSpec((chunk, w), lambda b, c: (b * nc + c, 3)),
                  pl.BlockSpec((1, w), lambda b, c: (0, 0))],
        out_specs=[pl.BlockSpec((chunk, w), lambda b, c: (b * nc + c, 0)),
                   pl.BlockSpec((1, heads, dk, dk), lambda b, c: (b, 0, 0, 0))],
        out_shape=[jax.ShapeDtypeStruct((rows_total, w), BF16),
                   jax.ShapeDtypeStruct((batch, heads, dk, dk), F32)],
        scratch_shapes=[pltpu.VMEM((heads, dk, dk), F32)],
        compiler_params=_cparams("parallel", "arbitrary"),
        name="hgrn_scan",
    )(lb_param, proj, proj, proj, proj, norm_g.reshape(1, w))


def _hgrn_step_kernel(lbp_ref, q_ref, f_ref, i_ref, g_ref, ng_ref, s_ref, prev_ref,
                      o_ref, snew_ref, qt_ref, ft_ref, kt_ref, orow_ref, *, heads, dk, nb, layer):
    del prev_ref
    b = pl.program_id(0)

    @pl.when(b == 0)
    def _():
        orow_ref[...] = jnp.zeros_like(orow_ref)
        lb = _lower_bound(lbp_ref[...], layer)
        qs = _silu(q_ref[...])
        forget = lb + (1.0 - lb) * _sigmoid(f_ref[...])
        kk = 1.0 - forget
        for h in range(heads):
            cols = slice(h * dk, (h + 1) * dk)
            qt_ref[h] = qs[:, cols].T
            ft_ref[h] = forget[:, cols].T
            kt_ref[h] = kk[:, cols].T

    sel = lax.broadcasted_iota(I32, (dk, nb), 1) == b
    for h in range(heads):
        cols = slice(h * dk, (h + 1) * dk)
        fcol = jnp.sum(jnp.where(sel, ft_ref[h], 0.0), axis=1, keepdims=True)
        kcol = jnp.sum(jnp.where(sel, kt_ref[h], 0.0), axis=1, keepdims=True)
        qcol = jnp.sum(jnp.where(sel, qt_ref[h], 0.0), axis=1, keepdims=True)
        irow = _load_row(i_ref, b, cols)
        s_new = s_ref[0, h] * fcol + kcol * irow
        snew_ref[0, h] = s_new
        o = jnp.sum(s_new * qcol, axis=0, keepdims=True)
        _store_row(orow_ref, b, cols, _rms_gate(o, _load_row(g_ref, b, cols), ng_ref[:, cols]))

    @pl.when(b == pl.num_programs(0) - 1)
    def _():
        o_ref[...] = orow_ref[...].astype(BF16)


def hgrn_step(proj, gated, state, lb_param, norm_g, *, row0, nb, heads, dk, layer):
    kern = functools.partial(_hgrn_step_kernel, heads=heads, dk=dk, nb=nb, layer=layer)
    rb = row0 // nb
    w = heads * dk
    return pl.pallas_call(
        kern,
        grid=(nb,),
        in_specs=[pl.BlockSpec(lb_param.shape, lambda b: (0, 0)),
                  pl.BlockSpec((nb, w), lambda b: (rb, 0)),
                  pl.BlockSpec((nb, w), lambda b: (rb, 1)),
                  pl.BlockSpec((nb, w), lambda b: (rb, 2)),
                  pl.BlockSpec((nb, w), lambda b: (rb, 3)),
                  pl.BlockSpec((1, w), lambda b: (0, 0)),
                  pl.BlockSpec((1, heads, dk, dk), lambda b: (b, 0, 0, 0)),
                  pl.BlockSpec(memory_space=pl.ANY)],
        out_specs=[pl.BlockSpec((nb, w), lambda b: (rb, 0)),
                   pl.BlockSpec((1, heads, dk, dk), lambda b: (b, 0, 0, 0))],
        out_shape=[jax.ShapeDtypeStruct(gated.shape, gated.dtype),
                   jax.ShapeDtypeStruct(state.shape, F32)],
        scratch_shapes=[pltpu.VMEM((heads, dk, nb), F32),
                        pltpu.VMEM((heads, dk, nb), F32),
                        pltpu.VMEM((heads, dk, nb), F32),
                        pltpu.VMEM((nb, w), F32)],
        input_output_aliases={7: 0},
        compiler_params=_cparams("arbitrary"),
        name="hgrn_step",
    )(lb_param, proj, proj, proj, proj, norm_g.reshape(1, w), state, gated)


def _router_kernel(x_ref, g_ref, r_ref, h_ref, route_ref, cnt_ref, carry_ref, *, tm, experts):
    i = pl.program_id(0)

    @pl.when(i == 0)
    def _():
        carry_ref[...] = jnp.zeros_like(carry_ref)

    h = _rms(x_ref[...], g_ref[...])
    h_ref[...] = h
    logits = jnp.dot(h, r_ref[...], preferred_element_type=F32, precision=lax.Precision.HIGHEST)
    lane = lax.broadcasted_iota(I32, (tm, LANES), 1)
    valid = lane < experts
    z = jnp.where(valid, logits, -jnp.inf)
    ez = jnp.exp(z - jnp.max(z, axis=-1, keepdims=True))
    p = ez / jnp.sum(ez, axis=-1, keepdims=True)
    p = jnp.where(valid, p, -1.0)
    lane_f = lane.astype(F32)
    v1 = jnp.max(p, axis=-1, keepdims=True)
    i1 = jnp.min(jnp.where(p == v1, lane_f, float(LANES)), axis=-1, keepdims=True)
    p2 = jnp.where(lane_f == i1, -1.0, p)
    v2 = jnp.max(p2, axis=-1, keepdims=True)
    i2 = jnp.min(jnp.where(p2 == v2, lane_f, float(LANES)), axis=-1, keepdims=True)
    den = v1 + v2
    m0 = jnp.where(lane_f == i1, 1.0, 0.0)
    m1 = jnp.where(lane_f == i2, 1.0, 0.0)
    msum = m0 + m1
    ri = lax.broadcasted_iota(I32, (tm, tm), 0)
    ci = lax.broadcasted_iota(I32, (tm, tm), 1)
    strict = jnp.where(ri > ci, 1.0, 0.0).astype(BF16)
    before = jnp.dot(strict, msum.astype(BF16), preferred_element_type=F32) + carry_ref[...]
    rank0 = jnp.sum(m0 * before, axis=-1, keepdims=True)
    rank1 = jnp.sum(m1 * before, axis=-1, keepdims=True)
    total = carry_ref[...] + jnp.sum(msum, axis=0, keepdims=True)
    carry_ref[...] = total
    cnt_ref[...] = jnp.broadcast_to(total, cnt_ref.shape)
    route = jnp.where(lane == 0, i1,
            jnp.where(lane == 1, i2,
            jnp.where(lane == 2, v1 / den,
            jnp.where(lane == 3, v2 / den,
            jnp.where(lane == 4, rank0,
            jnp.where(lane == 5, rank1, 0.0))))))
    route_ref[...] = route


def router(x, g, router_w, tm):
    t, d = x.shape
    experts = router_w.shape[1]
    rpad = jnp.zeros((d, LANES), F32).at[:, :experts].set(router_w)
    kern = functools.partial(_router_kernel, tm=tm, experts=experts)
    return pl.pallas_call(
        kern,
        grid=(t // tm,),
        in_specs=[pl.BlockSpec((tm, d), lambda i: (i, 0)),
                  pl.BlockSpec((1, d), lambda i: (0, 0)),
                  pl.BlockSpec((d, LANES), lambda i: (0, 0))],
        out_specs=[pl.BlockSpec((tm, d), lambda i: (i, 0)),
                   pl.BlockSpec((tm, LANES), lambda i: (i, 0)),
                   pl.BlockSpec((8, LANES), lambda i: (0, 0))],
        out_shape=[jax.ShapeDtypeStruct((t, d), F32),
                   jax.ShapeDtypeStruct((t, LANES), F32),
                   jax.ShapeDtypeStruct((8, LANES), F32)],
        scratch_shapes=[pltpu.VMEM((1, LANES), F32)],
        compiler_params=_cparams("arbitrary"),
        name="router",
    )(x, g.reshape(1, d), rpad)


def _plan_kernel(cnt_ref, off_ref, te_ref, tn_ref, *, experts, ntiles, row_tile, sub):
    def clear(j, carry):
        te_ref[j] = 0
        tn_ref[j] = 0
        return carry

    lax.fori_loop(0, ntiles, clear, 0)
    tile0 = jnp.int32(0)
    last = jnp.int32(0)
    for e in range(experts):
        n = cnt_ref[e]
        off_ref[e] = tile0 * row_tile
        nt = (n + (row_tile - 1)) // row_tile

        def fill(j, carry, e=e, n=n, tile0=tile0):
            te_ref[tile0 + j] = e
            rows = jnp.minimum(n - j * row_tile, row_tile)
            tn_ref[tile0 + j] = (rows + (sub - 1)) // sub
            return carry

        lax.fori_loop(0, nt, fill, 0)
        last = jnp.where(nt > 0, e, last)
        tile0 = tile0 + nt

    def tail(j, carry):
        te_ref[j] = last
        return carry

    lax.fori_loop(tile0, ntiles, tail, 0)


def plan(counts, ntiles):
    experts = counts.shape[0]
    kern = functools.partial(_plan_kernel, experts=experts, ntiles=ntiles, row_tile=MOE_ROW_TILE, sub=MOE_SUB)
    smem = pl.BlockSpec(memory_space=pltpu.SMEM)
    return pl.pallas_call(
        kern,
        in_specs=[smem],
        out_specs=[smem, smem, smem],
        out_shape=[jax.ShapeDtypeStruct((experts,), I32),
                   jax.ShapeDtypeStruct((ntiles,), I32),
                   jax.ShapeDtypeStruct((ntiles,), I32)],
        name="moe_plan",
    )(counts)


def _dispatch_kernel(off_ref, e_ref, r_ref, h_hbm, init_hbm, out_hbm, sem, *, tm):
    del init_hbm
    i = pl.program_id(0)

    def row_copy(src, dst):
        return pltpu.make_async_copy(h_hbm.at[pl.ds(src, 1)], out_hbm.at[pl.ds(dst, 1)], sem)

    def issue(j, carry):
        dst = off_ref[e_ref[0, 0, j]] + r_ref[0, 0, j]
        row_copy(i * tm + lax.shift_right_logical(j, 1), dst).start()
        return carry

    lax.fori_loop(0, TOP_K * tm, issue, 0)

    def drain(j, carry):
        row_copy(0, 0).wait()
        return carry

    lax.fori_loop(0, TOP_K * tm, drain, 0)


def dispatch(h, off, eidx, rank, rows, tm):
    t, d = h.shape
    nt = t // tm
    kern = functools.partial(_dispatch_kernel, tm=tm)
    smem_blk = pl.BlockSpec((1, 1, TOP_K * tm), lambda i, off: (i, 0, 0), memory_space=pltpu.SMEM)
    return pl.pallas_call(
        kern,
        grid_spec=pltpu.PrefetchScalarGridSpec(
            num_scalar_prefetch=1,
            grid=(nt,),
            in_specs=[smem_blk, smem_blk,
                      pl.BlockSpec(memory_space=pl.ANY),
                      pl.BlockSpec(memory_space=pl.ANY)],
            out_specs=pl.BlockSpec(memory_space=pl.ANY),
            scratch_shapes=[pltpu.SemaphoreType.DMA(())]),
        out_shape=jax.ShapeDtypeStruct((rows, d), h.dtype),
        input_output_aliases={4: 0},
        compiler_params=_cparams("arbitrary"),
        name="moe_dispatch",
    )(off, eidx.reshape(nt, 1, TOP_K * tm), rank.reshape(nt, 1, TOP_K * tm), h,
      jnp.zeros((rows, d), h.dtype))


def _expert_kernel(te_ref, tn_ref, x_ref, wg_ref, wu_ref, wd_ref, o_ref, h_ref, wgb_ref, wub_ref, wdb_ref,
                   *, sub, nsub_max):
    del te_ref
    i = pl.program_id(0)
    f = pl.program_id(1)
    nsub = tn_ref[i]

    @pl.when((nsub == 0) & (f == 0))
    def _():
        o_ref[...] = jnp.zeros_like(o_ref)

    @pl.when(nsub > 0)
    def _():
        @pl.when(f == 0)
        def _():
            h_ref[...] = x_ref[...].astype(BF16)
            o_ref[...] = jnp.zeros_like(o_ref)

        wgb_ref[...] = wg_ref[0].astype(BF16)
        wub_ref[...] = wu_ref[0].astype(BF16)
        wdb_ref[...] = wd_ref[0].astype(BF16)
        for sb in range(nsub_max):
            @pl.when(sb < nsub)
            def _():
                rows = slice(sb * sub, (sb + 1) * sub)
                h = h_ref[rows]
                a = jnp.dot(h, wgb_ref[...], preferred_element_type=F32)
                u = jnp.dot(h, wub_ref[...], preferred_element_type=F32)
                act = (_silu(a) * u).astype(BF16)
                o_ref[rows] += jnp.dot(act, wdb_ref[...], preferred_element_type=F32)


def expert_ffn(xs, te, tn, wg, wu, wd):
    rows, d = xs.shape
    experts, _, fdim = wg.shape
    ntiles = rows // MOE_ROW_TILE
    tf = MOE_F_TILE
    nf = fdim // tf
    kern = functools.partial(_expert_kernel, sub=MOE_SUB, nsub_max=MOE_ROW_TILE // MOE_SUB)

    def fcol(i, f, te, tn):
        return jnp.where(tn[i] > 0, f, nf - 1)

    return pl.pallas_call(
        kern,
        grid_spec=pltpu.PrefetchScalarGridSpec(
            num_scalar_prefetch=2,
            grid=(ntiles, nf),
            in_specs=[pl.BlockSpec((MOE_ROW_TILE, d), lambda i, f, te, tn: (i, 0)),
                      pl.BlockSpec((1, d, tf), lambda i, f, te, tn: (te[i], 0, fcol(i, f, te, tn))),
                      pl.BlockSpec((1, d, tf), lambda i, f, te, tn: (te[i], 0, fcol(i, f, te, tn))),
                      pl.BlockSpec((1, tf, d), lambda i, f, te, tn: (te[i], fcol(i, f, te, tn), 0))],
            out_specs=pl.BlockSpec((MOE_ROW_TILE, d), lambda i, f, te, tn: (i, 0)),
            scratch_shapes=[pltpu.VMEM((MOE_ROW_TILE, d), BF16),
                            pltpu.VMEM((d, tf), BF16),
                            pltpu.VMEM((d, tf), BF16),
                            pltpu.VMEM((tf, d), BF16)]),
        out_shape=jax.ShapeDtypeStruct((rows, d), F32),
        compiler_params=_cparams("arbitrary", "arbitrary"),
        name="expert_ffn",
    )(te, tn, xs, wg, wu, wd)


def _combine_kernel(off_ref, e_ref, r_ref, x_ref, route_ref, fg_ref, ys_hbm, o_ref, gath_ref, sem, *, tm):
    def row_copy(src, k, t):
        return pltpu.make_async_copy(ys_hbm.at[pl.ds(src, 1)], gath_ref.at[k, pl.ds(t, 1)], sem)

    def issue(j, carry):
        src = off_ref[e_ref[0, 0, j]] + r_ref[0, 0, j]
        row_copy(src, j & 1, lax.shift_right_logical(j, 1)).start()
        return carry

    lax.fori_loop(0, TOP_K * tm, issue, 0)

    def drain(j, carry):
        row_copy(0, 0, 0).wait()
        return carry

    lax.fori_loop(0, TOP_K * tm, drain, 0)
    route = route_ref[...]
    x = x_ref[...] + route[:, 2:3] * gath_ref[0] + route[:, 3:4] * gath_ref[1]
    o_ref[...] = _rms(x, fg_ref[...])


def combine(x, route, ys, off, eidx, rank, final_g, tm):
    t, d = x.shape
    nt = t // tm
    kern = functools.partial(_combine_kernel, tm=tm)
    smem_blk = pl.BlockSpec((1, 1, TOP_K * tm), lambda i, off: (i, 0, 0), memory_space=pltpu.SMEM)
    return pl.pallas_call(
        kern,
        grid_spec=pltpu.PrefetchScalarGridSpec(
            num_scalar_prefetch=1,
            grid=(nt,),
            in_specs=[smem_blk, smem_blk,
                      pl.BlockSpec((tm, d), lambda i, off: (i, 0)),
                      pl.BlockSpec((tm, LANES), lambda i, off: (i, 0)),
                      pl.BlockSpec((1, d), lambda i, off: (0, 0)),
                      pl.BlockSpec(memory_space=pl.ANY)],
            out_specs=pl.BlockSpec((tm, d), lambda i, off: (i, 0)),
            scratch_shapes=[pltpu.VMEM((TOP_K, tm, d), F32),
                            pltpu.SemaphoreType.DMA(())]),
        out_shape=jax.ShapeDtypeStruct((t, d), F32),
        compiler_params=_cparams("arbitrary"),
        name="moe_combine",
    )(off, eidx.reshape(nt, 1, TOP_K * tm), rank.reshape(nt, 1, TOP_K * tm), x, route,
      final_g.reshape(1, d), ys)


def moe_residual_final_norm(x, norm_g, router_w, wg, wu, wd, final_g, tm):
    t, d = x.shape
    experts = router_w.shape[1]
    h, route, cnt = router(x, norm_g, router_w, tm)
    counts = cnt[0, :experts].astype(I32)
    eidx = route[:, 0:TOP_K].astype(I32)
    rank = route[:, 4:4 + TOP_K].astype(I32)
    ntiles = pl.cdiv(TOP_K * t + experts * (MOE_ROW_TILE - 1), MOE_ROW_TILE)
    off, te, tn = plan(counts, ntiles)
    xs = dispatch(h, off, eidx, rank, ntiles * MOE_ROW_TILE, tm)
    ys = expert_ffn(xs, te, tn, wg, wu, wd)
    return combine(x, route, ys, off, eidx, rank, final_g, tm)


def _rope_tables(pos, half):
    inv = jnp.power(ROPE_THETA, -jnp.arange(half, dtype=F32) / half)
    ang = pos[:, None] * inv[None, :]
    return jnp.cos(ang), jnp.sin(ang)


def kernel(x_prompt, x_sample, state_retention, state_hgrn, norm_mix_g, norm_ffn_g, final_norm_g, ret_w_in, ret_gn_g, ret_w_out, hg_w_in, hg_lb_param, hg_norm_g, hg_w_out, ffn_w_gate, ffn_w_up, ffn_w_down, moe_router, moe_w_gate, moe_w_up, moe_w_down):
    bp, lp, d = x_prompt.shape
    bs, ls, _ = x_sample.shape
    assert ls == 1 and norm_mix_g.shape[0] == 2
    _, _, ret_heads, ret_dk, ret_dv = state_retention.shape
    _, _, hg_heads, hg_dk, _ = state_hgrn.shape
    tp = bp * lp
    t = tp + bs
    tm = _pick_tile(t, 768, 16)

    x = jnp.concatenate([x_prompt.reshape(tp, d), x_sample.reshape(bs, d)], axis=0)

    log_gamma = jnp.log1p(-jnp.exp2(-5.0 - jnp.arange(ret_heads, dtype=F32)))
    cos_p, sin_p = _rope_tables(jnp.arange(lp, dtype=F32), ret_dk // 2)
    cos_s, sin_s = _rope_tables(PAST_LEN + jnp.arange(ls, dtype=F32), ret_dk // 2)
    proj = norm_matmul(x, norm_mix_g[0], ret_w_in[0].astype(BF16), tm, _pick_tile(ret_w_in.shape[2], 1024, LANES))
    gated, ret_p = retention_scan(proj, log_gamma, cos_p, sin_p, ret_gn_g[0], batch=bp, seq=lp,
                                  heads=ret_heads, dk=ret_dk, dv=ret_dv, rows_total=t)
    gated, ret_s = retention_step(proj, gated, state_retention[0], log_gamma, cos_s, sin_s, ret_gn_g[0],
                                  row0=tp, nb=bs, heads=ret_heads, dk=ret_dk, dv=ret_dv)
    x = matmul_residual(gated, ret_w_out[0].astype(BF16), x, tm)
    ff = ffn_w_gate.shape[2]
    x = ffn_residual(x, norm_ffn_g[0], ffn_w_gate[0].astype(BF16), ffn_w_up[0].astype(BF16),
                     ffn_w_down[0].astype(BF16), tm, _pick_tile(ff, 1536, LANES))

    proj = norm_matmul(x, norm_mix_g[1], hg_w_in[0].astype(BF16), tm, _pick_tile(hg_w_in.shape[2], 1024, LANES))
    gated, hg_p = hgrn_scan(proj, hg_lb_param, hg_norm_g[0], batch=bp, seq=lp, heads=hg_heads, dk=hg_dk,
                            rows_total=t, layer=1)
    gated, hg_s = hgrn_step(proj, gated, state_hgrn[0], hg_lb_param, hg_norm_g[0], row0=tp, nb=bs,
                            heads=hg_heads, dk=hg_dk, layer=1)
    x = matmul_residual(gated, hg_w_out[0].astype(BF16), x, tm)
    y = moe_residual_final_norm(x, norm_ffn_g[1], moe_router[0], moe_w_gate[0], moe_w_up[0], moe_w_down[0],
                                final_norm_g, tm)

    return (y[:tp].reshape(bp, lp, d), y[tp:].reshape(bs, ls, d),
            ret_p[None], ret_s[None], hg_p[None], hg_s[None])
```

```python
import functools

import jax
import jax.numpy as jnp
from jax import lax
from jax.experimental import pallas as pl
from jax.experimental.pallas import tpu as pltpu

F32 = jnp.float32
BF16 = jnp.bfloat16
I32 = jnp.int32

NORM_EPS = 1e-6
GN_EPS = 1e-5
ROPE_THETA = 10000.0
PAST_LEN = 16384
TOP_K = 2

LANES = 128
VMEM_LIMIT = 56 * 1024 * 1024

RET_CHUNK = 256
HG_CHUNK = 128
HG_SUB = 16
MOE_ROW_TILE = 1024
MOE_SUB = 256
MOE_F_TILE = 512
DMA_UNROLL = 8

NT_DIMS = (((1,), (1,)), ((), ()))
TN_DIMS = (((0,), (0,)), ((), ()))


def _cparams(*sem):
    return pltpu.CompilerParams(dimension_semantics=sem, vmem_limit_bytes=VMEM_LIMIT)


def _pick_tile(n, target, mult):
    best = None
    for t in range(mult, min(n, target) + 1, mult):
        if n % t == 0:
            best = t
    assert best is not None, (n, target, mult)
    return best


def _sigmoid(x):
    return 1.0 / (1.0 + jnp.exp(-x))


def _silu(x):
    return x * _sigmoid(x)


def _rms(x, g):
    return x * lax.rsqrt(jnp.mean(x * x, axis=-1, keepdims=True) + NORM_EPS) * g


def _row_group(b):
    return pl.multiple_of(lax.shift_left(lax.shift_right_logical(b, 3), 3), 8), b & 7


def _load_row(ref, b, cols):
    base, r = _row_group(b)
    blk = ref[pl.ds(base, 8), cols]
    rows = lax.broadcasted_iota(I32, blk.shape, 0)
    return jnp.sum(jnp.where(rows == r, blk, 0.0), axis=0, keepdims=True)


def _store_row(ref, b, cols, row):
    base, r = _row_group(b)
    blk = ref[pl.ds(base, 8), cols]
    rows = lax.broadcasted_iota(I32, blk.shape, 0)
    ref[pl.ds(base, 8), cols] = jnp.where(rows == r, row, blk)


def _norm_matmul_kernel(x_ref, g_ref, w_ref, o_ref, h_ref):
    @pl.when(pl.program_id(1) == 0)
    def _():
        h_ref[...] = _rms(x_ref[...], g_ref[...]).astype(BF16)

    o_ref[...] = jnp.dot(h_ref[...], w_ref[...], preferred_element_type=F32)


def norm_matmul(x, g, w, tm, tn):
    t, d = x.shape
    n = w.shape[1]
    return pl.pallas_call(
        _norm_matmul_kernel,
        grid=(t // tm, n // tn),
        in_specs=[pl.BlockSpec((tm, d), lambda i, j: (i, 0)),
                  pl.BlockSpec((1, d), lambda i, j: (0, 0)),
                  pl.BlockSpec((d, tn), lambda i, j: (0, j))],
        out_specs=pl.BlockSpec((tm, tn), lambda i, j: (i, j)),
        out_shape=jax.ShapeDtypeStruct((t, n), F32),
        scratch_shapes=[pltpu.VMEM((tm, d), BF16)],
        compiler_params=_cparams("parallel", "arbitrary"),
        name="norm_matmul",
    )(x, g.reshape(1, d), w)


def _matmul_res_kernel(a_ref, w_ref, x_ref, o_ref):
    o_ref[...] = x_ref[...] + jnp.dot(a_ref[...], w_ref[...], preferred_element_type=F32)


def matmul_residual(a, w, x, tm):
    t, k = a.shape
    n = w.shape[1]
    return pl.pallas_call(
        _matmul_res_kernel,
        grid=(t // tm,),
        in_specs=[pl.BlockSpec((tm, k), lambda i: (i, 0)),
                  pl.BlockSpec((k, n), lambda i: (0, 0)),
                  pl.BlockSpec((tm, n), lambda i: (i, 0))],
        out_specs=pl.BlockSpec((tm, n), lambda i: (i, 0)),
        out_shape=jax.ShapeDtypeStruct((t, n), F32),
        compiler_params=_cparams("parallel"),
        name="matmul_residual",
    )(a, w, x)


def _ffn_kernel(x_ref, g_ref, wg_ref, wu_ref, wd_ref, o_ref, h_ref):
    f = pl.program_id(1)

    @pl.when(f == 0)
    def _():
        x = x_ref[...]
        h_ref[...] = _rms(x, g_ref[...]).astype(BF16)
        o_ref[...] = x

    h = h_ref[...]
    a = jnp.dot(h, wg_ref[...], preferred_element_type=F32)
    u = jnp.dot(h, wu_ref[...], preferred_element_type=F32)
    act = (_silu(a) * u).astype(BF16)
    o_ref[...] += jnp.dot(act, wd_ref[...], preferred_element_type=F32)


def ffn_residual(x, g, wg, wu, wd, tm, tf):
    t, d = x.shape
    f = wg.shape[1]
    return pl.pallas_call(
        _ffn_kernel,
        grid=(t // tm, f // tf),
        in_specs=[pl.BlockSpec((tm, d), lambda i, j: (i, 0)),
                  pl.BlockSpec((1, d), lambda i, j: (0, 0)),
                  pl.BlockSpec((d, tf), lambda i, j: (0, j)),
                  pl.BlockSpec((d, tf), lambda i, j: (0, j)),
                  pl.BlockSpec((tf, d), lambda i, j: (j, 0))],
        out_specs=pl.BlockSpec((tm, d), lambda i, j: (i, 0)),
        out_shape=jax.ShapeDtypeStruct((t, d), F32),
        scratch_shapes=[pltpu.VMEM((tm, d), BF16)],
        compiler_params=_cparams("parallel", "arbitrary"),
        name="ffn_residual",
    )(x, g.reshape(1, d), wg, wu, wd)


def _rotary(x, cos, sin, half):
    x1 = x[:, :half]
    x2 = x[:, half:]
    return jnp.concatenate([x1 * cos - x2 * sin, x2 * cos + x1 * sin], axis=-1)


def _group_norm_gate(o, gate, gain):
    mu = jnp.mean(o, axis=-1, keepdims=True)
    d = o - mu
    var = jnp.mean(d * d, axis=-1, keepdims=True)
    return _silu(gate) * (d * lax.rsqrt(var + GN_EPS) * gain)


def _ret_scan_kernel(lg_ref, q_ref, k_ref, v_ref, g_ref, cos_ref, sin_ref, gn_ref,
                     o_ref, sfin_ref, s_ref, *, chunk, dk, dv):
    c = pl.program_id(2)

    @pl.when(c == 0)
    def _():
        s_ref[...] = jnp.zeros_like(s_ref)

    lg = lg_ref[pl.program_id(1)]
    half = dk // 2
    cos = cos_ref[...]
    sin = sin_ref[...]
    q = _rotary(q_ref[...], cos, sin, half)
    k = _rotary(k_ref[...], cos, sin, half) * (dk ** -0.5)
    v = v_ref[...].astype(BF16)

    t = lax.broadcasted_iota(I32, (chunk, 1), 0).astype(F32)
    decay_in = jnp.exp((t + 1.0) * lg)
    decay_out = jnp.exp((chunk - 1.0 - t) * lg)
    ti = lax.broadcasted_iota(I32, (chunk, chunk), 0)
    si = lax.broadcasted_iota(I32, (chunk, chunk), 1)
    causal = ti >= si
    rel = (ti - si).astype(F32)
    mask = jnp.where(causal, jnp.exp(jnp.where(causal, rel, 0.0) * lg), 0.0)
    decay_chunk = jnp.exp(jnp.zeros((1, dv), F32) + chunk * lg)

    s = s_ref[...]
    scores = lax.dot_general(q.astype(BF16), k.astype(BF16), NT_DIMS, preferred_element_type=F32) * mask
    o = (jnp.dot(scores.astype(BF16), v, preferred_element_type=F32)
         + jnp.dot((q * decay_in).astype(BF16), s.astype(BF16), preferred_element_type=F32))
    kd_t = (k * decay_out).T.astype(BF16)
    s_new = s * decay_chunk + jnp.dot(kd_t, v, preferred_element_type=F32)
    s_ref[...] = s_new

    o_ref[...] = _group_norm_gate(o, g_ref[...], gn_ref[...]).astype(BF16)

    @pl.when(c == pl.num_programs(2) - 1)
    def _():
        sfin_ref[0, 0] = s_new


def retention_scan(proj, log_gamma, cos, sin, gn_g, *, batch, seq, heads, dk, dv, rows_total):
    chunk = min(RET_CHUNK, seq)
    nc = seq // chunk
    kern = functools.partial(_ret_scan_kernel, chunk=chunk, dk=dk, dv=dv)
    qk_blocks = heads * dk // dv
    return pl.pallas_call(
        kern,
        grid=(batch, heads, nc),
        in_specs=[pl.BlockSpec(memory_space=pltpu.SMEM),
                  pl.BlockSpec((chunk, dk), lambda b, h, c: (b * nc + c, h)),
                  pl.BlockSpec((chunk, dk), lambda b, h, c: (b * nc + c, heads + h)),
                  pl.BlockSpec((chunk, dv), lambda b, h, c: (b * nc + c, 2 * qk_blocks + h)),
                  pl.BlockSpec((chunk, dv), lambda b, h, c: (b * nc + c, 2 * qk_blocks + heads + h)),
                  pl.BlockSpec((chunk, dk // 2), lambda b, h, c: (c, 0)),
                  pl.BlockSpec((chunk, dk // 2), lambda b, h, c: (c, 0)),
                  pl.BlockSpec((1, dv), lambda b, h, c: (0, h))],
        out_specs=[pl.BlockSpec((chunk, dv), lambda b, h, c: (b * nc + c, h)),
                   pl.BlockSpec((1, 1, dk, dv), lambda b, h, c: (b, h, 0, 0))],
        out_shape=[jax.ShapeDtypeStruct((rows_total, heads * dv), BF16),
                   jax.ShapeDtypeStruct((batch, heads, dk, dv), F32)],
        scratch_shapes=[pltpu.VMEM((dk, dv), F32)],
        compiler_params=_cparams("parallel", "parallel", "arbitrary"),
        name="retention_scan",
    )(log_gamma, proj, proj, proj, proj, cos, sin, gn_g.reshape(1, heads * dv))


def _ret_step_kernel(lg_ref, q_ref, k_ref, v_ref, g_ref, cos_ref, sin_ref, gn_ref, s_ref, prev_ref,
                     o_ref, snew_ref, qt_ref, kt_ref, orow_ref, *, heads, dk, dv, nb):
    del prev_ref
    b = pl.program_id(0)
    half = dk // 2

    @pl.when(b == 0)
    def _():
        orow_ref[...] = jnp.zeros_like(orow_ref)
        cos = cos_ref[...]
        sin = sin_ref[...]
        for h in range(heads):
            qh = _rotary(q_ref[:, h * dk:(h + 1) * dk], cos, sin, half)
            kh = _rotary(k_ref[:, h * dk:(h + 1) * dk], cos, sin, half) * (dk ** -0.5)
            qt_ref[h] = qh.T
            kt_ref[h] = kh.T

    sel = lax.broadcasted_iota(I32, (dk, nb), 1) == b
    for h in range(heads):
        cols = slice(h * dv, (h + 1) * dv)
        kcol = jnp.sum(jnp.where(sel, kt_ref[h], 0.0), axis=1, keepdims=True)
        qcol = jnp.sum(jnp.where(sel, qt_ref[h], 0.0), axis=1, keepdims=True)
        vrow = _load_row(v_ref, b, cols)
        gamma = jnp.exp(jnp.zeros((1, dv), F32) + lg_ref[h])
        s_new = s_ref[0, h] * gamma + kcol * vrow
        snew_ref[0, h] = s_new
        o = jnp.sum(s_new * qcol, axis=0, keepdims=True)
        _store_row(orow_ref, b, cols, _group_norm_gate(o, _load_row(g_ref, b, cols), gn_ref[:, cols]))

    @pl.when(b == pl.num_programs(0) - 1)
    def _():
        o_ref[...] = orow_ref[...].astype(BF16)


def retention_step(proj, gated, state, log_gamma, cos, sin, gn_g, *, row0, nb, heads, dk, dv):
    kern = functools.partial(_ret_step_kernel, heads=heads, dk=dk, dv=dv, nb=nb)
    rb = row0 // nb
    wq = heads * dk
    wv = heads * dv
    return pl.pallas_call(
        kern,
        grid=(nb,),
        in_specs=[pl.BlockSpec(memory_space=pltpu.SMEM),
                  pl.BlockSpec((nb, wq), lambda b: (rb, 0)),
                  pl.BlockSpec((nb, wq), lambda b: (rb, 1)),
                  pl.BlockSpec((nb, wv), lambda b: (rb, 2 * wq // wv)),
                  pl.BlockSpec((nb, wv), lambda b: (rb, 2 * wq // wv + 1)),
                  pl.BlockSpec((1, dk // 2), lambda b: (0, 0)),
                  pl.BlockSpec((1, dk // 2), lambda b: (0, 0)),
                  pl.BlockSpec((1, wv), lambda b: (0, 0)),
                  pl.BlockSpec((1, heads, dk, dv), lambda b: (b, 0, 0, 0)),
                  pl.BlockSpec(memory_space=pl.ANY)],
        out_specs=[pl.BlockSpec((nb, wv), lambda b: (rb, 0)),
                   pl.BlockSpec((1, heads, dk, dv), lambda b: (b, 0, 0, 0))],
        out_shape=[jax.ShapeDtypeStruct(gated.shape, gated.dtype),
                   jax.ShapeDtypeStruct(state.shape, F32)],
        scratch_shapes=[pltpu.VMEM((heads, dk, nb), F32),
                        pltpu.VMEM((heads, dk, nb), F32),
                        pltpu.VMEM((nb, wv), F32)],
        input_output_aliases={9: 0},
        compiler_params=_cparams("arbitrary"),
        name="retention_step",
    )(log_gamma, proj, proj, proj, proj, cos, sin, gn_g.reshape(1, wv), state, gated)


def _lower_bound(lbp, layer):
    m = jnp.max(lbp, axis=0, keepdims=True)
    e = jnp.exp(lbp - m)
    p = e / jnp.sum(e, axis=0, keepdims=True)
    return jnp.sum(p[:layer + 1], axis=0, keepdims=True) - p[0:1]


def _split_dot(mat_bf16, x):
    hi = x.astype(BF16)
    r1 = x - hi.astype(F32)
    mid = r1.astype(BF16)
    lo = (r1 - mid.astype(F32)).astype(BF16)
    return (jnp.dot(mat_bf16, hi, preferred_element_type=F32)
            + jnp.dot(mat_bf16, mid, preferred_element_type=F32)
            + jnp.dot(mat_bf16, lo, preferred_element_type=F32))


def _rms_gate(o, gate, gain):
    return _silu(gate) * (o * lax.rsqrt(jnp.mean(o * o, axis=-1, keepdims=True) + NORM_EPS) * gain)


def _hgrn_scan_kernel(lbp_ref, q_ref, f_ref, i_ref, g_ref, ng_ref, o_ref, sfin_ref, st_ref,
                      *, chunk, heads, dk, sub, layer):
    c = pl.program_id(1)

    @pl.when(c == 0)
    def _():
        st_ref[...] = jnp.zeros_like(st_ref)

    lb = _lower_bound(lbp_ref[...], layer)
    qs = _silu(q_ref[...])
    forget = lb + (1.0 - lb) * _sigmoid(f_ref[...])
    kk = 1.0 - forget
    logf = jnp.log(forget)

    ri = lax.broadcasted_iota(I32, (chunk, chunk), 0)
    ci = lax.broadcasted_iota(I32, (chunk, chunk), 1)
    tril = jnp.where(ri >= ci, 1.0, 0.0).astype(BF16)
    bcum = _split_dot(tril, logf)
    blast = bcum[chunk - 1:chunk, :]
    qe = qs * jnp.exp(bcum)
    kdec = kk * jnp.exp(blast - bcum)
    iv = i_ref[...]
    gate = g_ref[...]
    gain = ng_ref[...]

    ones = jnp.ones((dk, LANES), BF16)
    lane_s = lax.broadcasted_iota(I32, (sub, LANES), 1)
    row_s = lax.broadcasted_iota(I32, (sub, LANES), 0)
    col_c = lax.broadcasted_iota(I32, (sub, chunk), 1)
    nsub = chunk // sub

    for h in range(heads):
        cols = slice(h * dk, (h + 1) * dk)
        qs_h = qs[:, cols]
        kk_h = kk[:, cols]
        b_h = bcum[:, cols]
        i_h = iv[:, cols].astype(BF16)
        strips = []
        for ib in range(nsub):
            r0 = ib * sub
            qb = qs_h[r0:r0 + sub]
            kb = kk_h[r0:r0 + sub]
            bb = b_h[r0:r0 + sub]
            ws = [qb * kb[s:s + 1] * jnp.exp(jnp.minimum(bb - bb[s:s + 1], 0.0)) for s in range(sub)]
            wstack = jnp.concatenate(ws, axis=0).astype(BF16)
            rsum = jnp.dot(wstack, ones, preferred_element_type=F32)
            adiag = jnp.zeros((sub, LANES), F32)
            for s in range(sub):
                adiag = adiag + jnp.where(lane_s == r0 + s, rsum[s * sub:(s + 1) * sub], 0.0)
            strip = jnp.where(lane_s - r0 <= row_s, adiag, 0.0)[:, :chunk]
            if ib > 0:
                bs = b_h[r0 - 1:r0]
                qt = (qb * jnp.exp(bb - bs)).astype(BF16)
                kt = (kk_h * jnp.exp(jnp.minimum(bs - b_h, 0.0))).astype(BF16)
                aoff = lax.dot_general(qt, kt, NT_DIMS, preferred_element_type=F32)
                strip = strip + jnp.where(col_c < r0, aoff, 0.0)
            strips.append(strip)
        a = jnp.concatenate(strips, axis=0).astype(BF16)
        st = st_ref[h]
        o_h = (jnp.dot(a, i_h, preferred_element_type=F32)
               + lax.dot_general(qe[:, cols].astype(BF16), st.astype(BF16), NT_DIMS,
                                 preferred_element_type=F32))
        st_new = (st * jnp.exp(blast[:, cols])
                  + jnp.dot(iv[:, cols].T.astype(BF16), kdec[:, cols].astype(BF16),
                            preferred_element_type=F32))
        st_ref[h] = st_new
        o_ref[:, cols] = _rms_gate(o_h, gate[:, cols], gain[:, cols]).astype(BF16)

        @pl.when(c == pl.num_programs(1) - 1)
        def _():
            sfin_ref[0, h] = st_new.T


def hgrn_scan(proj, lb_param, norm_g, *, batch, seq, heads, dk, rows_total, layer):
    chunk = min(HG_CHUNK, seq)
    assert chunk == LANES and chunk % HG_SUB == 0
    nc = seq // chunk
    w = heads * dk
    kern = functools.partial(_hgrn_scan_kernel, chunk=chunk, heads=heads, dk=dk, sub=HG_SUB, layer=layer)
    return pl.pallas_call(
        kern,
        grid=(batch, nc),
        in_specs=[pl.BlockSpec(lb_param.shape, lambda b, c: (0, 0)),
                  pl.BlockSpec((chunk, w), lambda b, c: (b * nc + c, 0)),
                  pl.BlockSpec((chunk, w), lambda b, c: (b * nc + c, 1)),
                  pl.BlockSpec((chunk, w), lambda b, c: (b * nc + c, 2)),
                  pl.BlockSpec((chunk, w), lambda b, c: (b * nc + c, 3)),
                  pl.BlockSpec((1, w), lambda b, c: (0, 0))],
        out_specs=[pl.BlockSpec((chunk, w), lambda b, c: (b * nc + c, 0)),
                   pl.BlockSpec((1, heads, dk, dk), lambda b, c: (b, 0, 0, 0))],
        out_shape=[jax.ShapeDtypeStruct((rows_total, w), BF16),
                   jax.ShapeDtypeStruct((batch, heads, dk, dk), F32)],
        scratch_shapes=[pltpu.VMEM((heads, dk, dk), F32)],
        compiler_params=_cparams("parallel", "arbitrary"),
        name="hgrn_scan",
    )(lb_param, proj, proj, proj, proj, norm_g.reshape(1, w))


def _hgrn_step_kernel(lbp_ref, q_ref, f_ref, i_ref, g_ref, ng_ref, s_ref, prev_ref,
                      o_ref, snew_ref, qt_ref, ft_ref, kt_ref, orow_ref, *, heads, dk, nb, layer):
    del prev_ref
    b = pl.program_id(0)

    @pl.when(b == 0)
    def _():
        orow_ref[...] = jnp.zeros_like(orow_ref)
        lb = _lower_bound(lbp_ref[...], layer)
        qs = _silu(q_ref[...])
        forget = lb + (1.0 - lb) * _sigmoid(f_ref[...])
        kk = 1.0 - forget
        for h in range(heads):
            cols = slice(h * dk, (h + 1) * dk)
            qt_ref[h] = qs[:, cols].T
            ft_ref[h] = forget[:, cols].T
            kt_ref[h] = kk[:, cols].T

    sel = lax.broadcasted_iota(I32, (dk, nb), 1) == b
    for h in range(heads):
        cols = slice(h * dk, (h + 1) * dk)
        fcol = jnp.sum(jnp.where(sel, ft_ref[h], 0.0), axis=1, keepdims=True)
        kcol = jnp.sum(jnp.where(sel, kt_ref[h], 0.0), axis=1, keepdims=True)
        qcol = jnp.sum(jnp.where(sel, qt_ref[h], 0.0), axis=1, keepdims=True)
        irow = _load_row(i_ref, b, cols)
        s_new = s_ref[0, h] * fcol + kcol * irow
        snew_ref[0, h] = s_new
        o = jnp.sum(s_new * qcol, axis=0, keepdims=True)
        _store_row(orow_ref, b, cols, _rms_gate(o, _load_row(g_ref, b, cols), ng_ref[:, cols]))

    @pl.when(b == pl.num_programs(0) - 1)
    def _():
        o_ref[...] = orow_ref[...].astype(BF16)


def hgrn_step(proj, gated, state, lb_param, norm_g, *, row0, nb, heads, dk, layer):
    kern = functools.partial(_hgrn_step_kernel, heads=heads, dk=dk, nb=nb, layer=layer)
    rb = row0 // nb
    w = heads * dk
    return pl.pallas_call(
        kern,
        grid=(nb,),
        in_specs=[pl.BlockSpec(lb_param.shape, lambda b: (0, 0)),
                  pl.BlockSpec((nb, w), lambda b: (rb, 0)),
                  pl.BlockSpec((nb, w), lambda b: (rb, 1)),
                  pl.BlockSpec((nb, w), lambda b: (rb, 2)),
                  pl.BlockSpec((nb, w), lambda b: (rb, 3)),
                  pl.BlockSpec((1, w), lambda b: (0, 0)),
                  pl.BlockSpec((1, heads, dk, dk), lambda b: (b, 0, 0, 0)),
                  pl.BlockSpec(memory_space=pl.ANY)],
        out_specs=[pl.BlockSpec((nb, w), lambda b: (rb, 0)),
                   pl.BlockSpec((1, heads, dk, dk), lambda b: (b, 0, 0, 0))],
        out_shape=[jax.ShapeDtypeStruct(gated.shape, gated.dtype),
                   jax.ShapeDtypeStruct(state.shape, F32)],
        scratch_shapes=[pltpu.VMEM((heads, dk, nb), F32),
                        pltpu.VMEM((heads, dk, nb), F32),
                        pltpu.VMEM((heads, dk, nb), F32),
                        pltpu.VMEM((nb, w), F32)],
        input_output_aliases={7: 0},
        compiler_params=_cparams("arbitrary"),
        name="hgrn_step",
    )(lb_param, proj, proj, proj, proj, norm_g.reshape(1, w), state, gated)


def _router_kernel(x_ref, g_ref, r_ref, h_ref, route_ref, cnt_ref, carry_ref, *, tm, experts):
    i = pl.program_id(0)

    @pl.when(i == 0)
    def _():
        carry_ref[...] = jnp.zeros_like(carry_ref)

    h = _rms(x_ref[...], g_ref[...])
    h_ref[...] = h
    logits = jnp.dot(h, r_ref[...], preferred_element_type=F32, precision=lax.Precision.HIGHEST)
    lane = lax.broadcasted_iota(I32, (tm, LANES), 1)
    valid = lane < experts
    z = jnp.where(valid, logits, -jnp.inf)
    ez = jnp.exp(z - jnp.max(z, axis=-1, keepdims=True))
    p = ez / jnp.sum(ez, axis=-1, keepdims=True)
    p = jnp.where(valid, p, -1.0)
    lane_f = lane.astype(F32)
    v1 = jnp.max(p, axis=-1, keepdims=True)
    i1 = jnp.min(jnp.where(p == v1, lane_f, float(LANES)), axis=-1, keepdims=True)
    p2 = jnp.where(lane_f == i1, -1.0, p)
    v2 = jnp.max(p2, axis=-1, keepdims=True)
    i2 = jnp.min(jnp.where(p2 == v2, lane_f, float(LANES)), axis=-1, keepdims=True)
    den = v1 + v2
    m0 = jnp.where(lane_f == i1, 1.0, 0.0)
    m1 = jnp.where(lane_f == i2, 1.0, 0.0)
    msum = m0 + m1
    ri = lax.broadcasted_iota(I32, (tm, tm), 0)
    ci = lax.broadcasted_iota(I32, (tm, tm), 1)
    strict = jnp.where(ri > ci, 1.0, 0.0).astype(BF16)
    before = jnp.dot(strict, msum.astype(BF16), preferred_element_type=F32) + carry_ref[...]
    rank0 = jnp.sum(m0 * before, axis=-1, keepdims=True)
    rank1 = jnp.sum(m1 * before, axis=-1, keepdims=True)
    total = carry_ref[...] + jnp.sum(msum, axis=0, keepdims=True)
    carry_ref[...] = total
    cnt_ref[...] = jnp.broadcast_to(total, cnt_ref.shape)
    route = jnp.where(lane == 0, i1,
            jnp.where(lane == 1, i2,
            jnp.where(lane == 2, v1 / den,
            jnp.where(lane == 3, v2 / den,
            jnp.where(lane == 4, rank0,
            jnp.where(lane == 5, rank1, 0.0))))))
    route_ref[...] = route


def router(x, g, router_w, tm):
    t, d = x.shape
    experts = router_w.shape[1]
    rpad = jnp.zeros((d, LANES), F32).at[:, :experts].set(router_w)
    kern = functools.partial(_router_kernel, tm=tm, experts=experts)
    return pl.pallas_call(
        kern,
        grid=(t // tm,),
        in_specs=[pl.BlockSpec((tm, d), lambda i: (i, 0)),
                  pl.BlockSpec((1, d), lambda i: (0, 0)),
                  pl.BlockSpec((d, LANES), lambda i: (0, 0))],
        out_specs=[pl.BlockSpec((tm, d), lambda i: (i, 0)),
                   pl.BlockSpec((tm, LANES), lambda i: (i, 0)),
                   pl.BlockSpec((8, LANES), lambda i: (0, 0))],
        out_shape=[jax.ShapeDtypeStruct((t, d), F32),
                   jax.ShapeDtypeStruct((t, LANES), F32),
                   jax.ShapeDtypeStruct((8, LANES), F32)],
        scratch_shapes=[pltpu.VMEM((1, LANES), F32)],
        compiler_params=_cparams("arbitrary"),
        name="router",
    )(x, g.reshape(1, d), rpad)


def _plan_kernel(cnt_ref, off_ref, te_ref, tn_ref, *, experts, ntiles, row_tile, sub):
    def clear(j, carry):
        te_ref[j] = 0
        tn_ref[j] = 0
        return carry

    lax.fori_loop(0, ntiles, clear, 0)
    tile0 = jnp.int32(0)
    last = jnp.int32(0)
    for e in range(experts):
        n = cnt_ref[e]
        off_ref[e] = tile0 * row_tile
        nt = (n + (row_tile - 1)) // row_tile

        def fill(j, carry, e=e, n=n, tile0=tile0):
            te_ref[tile0 + j] = e
            rows = jnp.minimum(n - j * row_tile, row_tile)
            tn_ref[tile0 + j] = (rows + (sub - 1)) // sub
            return carry

        lax.fori_loop(0, nt, fill, 0)
        last = jnp.where(nt > 0, e, last)
        tile0 = tile0 + nt

    def tail(j, carry):
        te_ref[j] = last
        return carry

    lax.fori_loop(tile0, ntiles, tail, 0)


def plan(counts, ntiles):
    experts = counts.shape[0]
    kern = functools.partial(_plan_kernel, experts=experts, ntiles=ntiles, row_tile=MOE_ROW_TILE, sub=MOE_SUB)
    smem = pl.BlockSpec(memory_space=pltpu.SMEM)
    return pl.pallas_call(
        kern,
        in_specs=[smem],
        out_specs=[smem, smem, smem],
        out_shape=[jax.ShapeDtypeStruct((experts,), I32),
                   jax.ShapeDtypeStruct((ntiles,), I32),
                   jax.ShapeDtypeStruct((ntiles,), I32)],
        name="moe_plan",
    )(counts)


def _dispatch_kernel(off_ref, e_ref, r_ref, h_ref, init_hbm, out_hbm, sem, *, tm):
    del init_hbm

    def row_copy(src, dst):
        return pltpu.make_async_copy(h_ref.at[pl.ds(src, 1)], out_hbm.at[pl.ds(dst, 1)], sem)

    def issue(j, carry):
        dst = off_ref[e_ref[0, 0, j]] + r_ref[0, 0, j]
        row_copy(lax.shift_right_logical(j, 1), dst).start()
        return carry

    lax.fori_loop(0, TOP_K * tm, issue, 0, unroll=DMA_UNROLL)

    def drain(j, carry):
        row_copy(0, 0).wait()
        return carry

    lax.fori_loop(0, TOP_K * tm, drain, 0, unroll=DMA_UNROLL)


def dispatch(h, off, eidx, rank, rows, tm):
    t, d = h.shape
    nt = t // tm
    kern = functools.partial(_dispatch_kernel, tm=tm)
    smem_blk = pl.BlockSpec((1, 1, TOP_K * tm), lambda i, off: (i, 0, 0), memory_space=pltpu.SMEM)
    return pl.pallas_call(
        kern,
        grid_spec=pltpu.PrefetchScalarGridSpec(
            num_scalar_prefetch=1,
            grid=(nt,),
            in_specs=[smem_blk, smem_blk,
                      pl.BlockSpec((tm, d), lambda i, off: (i, 0)),
                      pl.BlockSpec(memory_space=pl.ANY)],
            out_specs=pl.BlockSpec(memory_space=pl.ANY),
            scratch_shapes=[pltpu.SemaphoreType.DMA(())]),
        out_shape=jax.ShapeDtypeStruct((rows, d), h.dtype),
        input_output_aliases={4: 0},
        compiler_params=_cparams("arbitrary"),
        name="moe_dispatch",
    )(off, eidx.reshape(nt, 1, TOP_K * tm), rank.reshape(nt, 1, TOP_K * tm), h,
      jnp.zeros((rows, d), h.dtype))


def _expert_kernel(te_ref, tn_ref, x_ref, wg_ref, wu_ref, wd_ref, o_ref, h_ref, wgb_ref, wub_ref, wdb_ref,
                   *, sub, nsub_max):
    del te_ref
    i = pl.program_id(0)
    f = pl.program_id(1)
    nsub = tn_ref[i]

    @pl.when((nsub == 0) & (f == 0))
    def _():
        o_ref[...] = jnp.zeros_like(o_ref)

    @pl.when(nsub > 0)
    def _():
        @pl.when(f == 0)
        def _():
            h_ref[...] = x_ref[...].astype(BF16)
            o_ref[...] = jnp.zeros_like(o_ref)

        wgb_ref[...] = wg_ref[0].astype(BF16)
        wub_ref[...] = wu_ref[0].astype(BF16)
        wdb_ref[...] = wd_ref[0].astype(BF16)
        for sb in range(nsub_max):
            @pl.when(sb < nsub)
            def _():
                rows = slice(sb * sub, (sb + 1) * sub)
                h = h_ref[rows]
                a = jnp.dot(h, wgb_ref[...], preferred_element_type=F32)
                u = jnp.dot(h, wub_ref[...], preferred_element_type=F32)
                act = (_silu(a) * u).astype(BF16)
                o_ref[rows] += jnp.dot(act, wdb_ref[...], preferred_element_type=F32)


def expert_ffn(xs, te, tn, wg, wu, wd):
    rows, d = xs.shape
    experts, _, fdim = wg.shape
    ntiles = rows // MOE_ROW_TILE
    tf = MOE_F_TILE
    nf = fdim // tf
    kern = functools.partial(_expert_kernel, sub=MOE_SUB, nsub_max=MOE_ROW_TILE // MOE_SUB)

    def fcol(i, f, te, tn):
        return jnp.where(tn[i] > 0, f, nf - 1)

    return pl.pallas_call(
        kern,
        grid_spec=pltpu.PrefetchScalarGridSpec(
            num_scalar_prefetch=2,
            grid=(ntiles, nf),
            in_specs=[pl.BlockSpec((MOE_ROW_TILE, d), lambda i, f, te, tn: (i, 0)),
                      pl.BlockSpec((1, d, tf), lambda i, f, te, tn: (te[i], 0, fcol(i, f, te, tn))),
                      pl.BlockSpec((1, d, tf), lambda i, f, te, tn: (te[i], 0, fcol(i, f, te, tn))),
                      pl.BlockSpec((1, tf, d), lambda i, f, te, tn: (te[i], fcol(i, f, te, tn), 0))],
            out_specs=pl.BlockSpec((MOE_ROW_TILE, d), lambda i, f, te, tn: (i, 0)),
            scratch_shapes=[pltpu.VMEM((MOE_ROW_TILE, d), BF16),
                            pltpu.VMEM((d, tf), BF16),
                            pltpu.VMEM((d, tf), BF16),
                            pltpu.VMEM((tf, d), BF16)]),
        out_shape=jax.ShapeDtypeStruct((rows, d), F32),
        compiler_params=_cparams("arbitrary", "arbitrary"),
        name="expert_ffn",
    )(te, tn, xs, wg, wu, wd)


def _combine_kernel(off_ref, e_ref, r_ref, x_ref, route_ref, fg_ref, ys_hbm, o_ref, gath_ref, sem, *, tm):
    def row_copy(src, k, t):
        return pltpu.make_async_copy(ys_hbm.at[pl.ds(src, 1)], gath_ref.at[k, pl.ds(t, 1)], sem)

    def issue(j, carry):
        src = off_ref[e_ref[0, 0, j]] + r_ref[0, 0, j]
        row_copy(src, j & 1, lax.shift_right_logical(j, 1)).start()
        return carry

    lax.fori_loop(0, TOP_K * tm, issue, 0, unroll=DMA_UNROLL)

    def drain(j, carry):
        row_copy(0, 0, 0).wait()
        return carry

    lax.fori_loop(0, TOP_K * tm, drain, 0, unroll=DMA_UNROLL)
    route = route_ref[...]
    x = x_ref[...] + route[:, 2:3] * gath_ref[0] + route[:, 3:4] * gath_ref[1]
    o_ref[...] = _rms(x, fg_ref[...])


def combine(x, route, ys, off, eidx, rank, final_g, tm):
    t, d = x.shape
    nt = t // tm
    kern = functools.partial(_combine_kernel, tm=tm)
    smem_blk = pl.BlockSpec((1, 1, TOP_K * tm), lambda i, off: (i, 0, 0), memory_space=pltpu.SMEM)
    return pl.pallas_call(
        kern,
        grid_spec=pltpu.PrefetchScalarGridSpec(
            num_scalar_prefetch=1,
            grid=(nt,),
            in_specs=[smem_blk, smem_blk,
                      pl.BlockSpec((tm, d), lambda i, off: (i, 0)),
                      pl.BlockSpec((tm, LANES), lambda i, off: (i, 0)),
                      pl.BlockSpec((1, d), lambda i, off: (0, 0)),
                      pl.BlockSpec(memory_space=pl.ANY)],
            out_specs=pl.BlockSpec((tm, d), lambda i, off: (i, 0)),
            scratch_shapes=[pltpu.VMEM((TOP_K, tm, d), F32),
                            pltpu.SemaphoreType.DMA(())]),
        out_shape=jax.ShapeDtypeStruct((t, d), F32),
        compiler_params=_cparams("arbitrary"),
        name="moe_combine",
    )(off, eidx.reshape(nt, 1, TOP_K * tm), rank.reshape(nt, 1, TOP_K * tm), x, route,
      final_g.reshape(1, d), ys)


def moe_residual_final_norm(x, norm_g, router_w, wg, wu, wd, final_g, tm):
    t, d = x.shape
    experts = router_w.shape[1]
    h, route, cnt = router(x, norm_g, router_w, tm)
    counts = cnt[0, :experts].astype(I32)
    eidx = route[:, 0:TOP_K].astype(I32)
    rank = route[:, 4:4 + TOP_K].astype(I32)
    ntiles = pl.cdiv(TOP_K * t + experts * (MOE_ROW_TILE - 1), MOE_ROW_TILE)
    off, te, tn = plan(counts, ntiles)
    xs = dispatch(h, off, eidx, rank, ntiles * MOE_ROW_TILE, tm)
    ys = expert_ffn(xs, te, tn, wg, wu, wd)
    return combine(x, route, ys, off, eidx, rank, final_g, tm)


def _rope_tables(pos, half):
    inv = jnp.power(ROPE_THETA, -jnp.arange(half, dtype=F32) / half)
    ang = pos[:, None] * inv[None, :]
    return jnp.cos(ang), jnp.sin(ang)


def kernel(x_prompt, x_sample, state_retention, state_hgrn, norm_mix_g, norm_ffn_g, final_norm_g, ret_w_in, ret_gn_g, ret_w_out, hg_w_in, hg_lb_param, hg_norm_g, hg_w_out, ffn_w_gate, ffn_w_up, ffn_w_down, moe_router, moe_w_gate, moe_w_up, moe_w_down):
    bp, lp, d = x_prompt.shape
    bs, ls, _ = x_sample.shape
    assert ls == 1 and norm_mix_g.shape[0] == 2
    _, _, ret_heads, ret_dk, ret_dv = state_retention.shape
    _, _, hg_heads, hg_dk, _ = state_hgrn.shape
    tp = bp * lp
    t = tp + bs
    tm = _pick_tile(t, 768, 16)

    x = jnp.concatenate([x_prompt.reshape(tp, d), x_sample.reshape(bs, d)], axis=0)

    log_gamma = jnp.log1p(-jnp.exp2(-5.0 - jnp.arange(ret_heads, dtype=F32)))
    cos_p, sin_p = _rope_tables(jnp.arange(lp, dtype=F32), ret_dk // 2)
    cos_s, sin_s = _rope_tables(PAST_LEN + jnp.arange(ls, dtype=F32), ret_dk // 2)
    proj = norm_matmul(x, norm_mix_g[0], ret_w_in[0].astype(BF16), tm, _pick_tile(ret_w_in.shape[2], 1024, LANES))
    gated, ret_p = retention_scan(proj, log_gamma, cos_p, sin_p, ret_gn_g[0], batch=bp, seq=lp,
                                  heads=ret_heads, dk=ret_dk, dv=ret_dv, rows_total=t)
    gated, ret_s = retention_step(proj, gated, state_retention[0], log_gamma, cos_s, sin_s, ret_gn_g[0],
                                  row0=tp, nb=bs, heads=ret_heads, dk=ret_dk, dv=ret_dv)
    x = matmul_residual(gated, ret_w_out[0].astype(BF16), x, tm)
    ff = ffn_w_gate.shape[2]
    x = ffn_residual(x, norm_ffn_g[0], ffn_w_gate[0].astype(BF16), ffn_w_up[0].astype(BF16),
                     ffn_w_down[0].astype(BF16), tm, _pick_tile(ff, 1536, LANES))

    proj = norm_matmul(x, norm_mix_g[1], hg_w_in[0].astype(BF16), tm, _pick_tile(hg_w_in.shape[2], 1024, LANES))
    gated, hg_p = hgrn_scan(proj, hg_lb_param, hg_norm_g[0], batch=bp, seq=lp, heads=hg_heads, dk=hg_dk,
                            rows_total=t, layer=1)
    gated, hg_s = hgrn_step(proj, gated, state_hgrn[0], hg_lb_param, hg_norm_g[0], row0=tp, nb=bs,
                            heads=hg_heads, dk=hg_dk, layer=1)
    x = matmul_residual(gated, hg_w_out[0].astype(BF16), x, tm)
    y = moe_residual_final_norm(x, norm_ffn_g[1], moe_router[0], moe_w_gate[0], moe_w_up[0], moe_w_down[0],
                                final_norm_g, tm)

    return (y[:tp].reshape(bp, lp, d), y[tp:].reshape(bs, ls, d),
            ret_p[None], ret_s[None], hg_p[None], hg_s[None])
```

```python
import functools

import jax
import jax.numpy as jnp
from jax import lax
from jax.experimental import pallas as pl
from jax.experimental.pallas import tpu as pltpu

F32 = jnp.float32
BF16 = jnp.bfloat16
I32 = jnp.int32

NORM_EPS = 1e-6
GN_EPS = 1e-5
ROPE_THETA = 10000.0
PAST_LEN = 16384
TOP_K = 2

LANES = 128
VMEM_LIMIT = 56 * 1024 * 1024

RET_CHUNK = 256
HG_CHUNK = 128
HG_DIAG = 8
MOE_ROW_TILE = 1024
MOE_SUB = 256
MOE_F_TILE = 512
DMA_UNROLL = 8

NT_DIMS = (((1,), (1,)), ((), ()))
TN_DIMS = (((0,), (0,)), ((), ()))


def _cparams(*sem):
    return pltpu.CompilerParams(dimension_semantics=sem, vmem_limit_bytes=VMEM_LIMIT)


def _pick_tile(n, target, mult):
    best = None
    for t in range(mult, min(n, target) + 1, mult):
        if n % t == 0:
            best = t
    assert best is not None, (n, target, mult)
    return best


def _sigmoid(x):
    return 1.0 / (1.0 + jnp.exp(-x))


def _silu(x):
    return x * _sigmoid(x)


def _rms(x, g):
    return x * lax.rsqrt(jnp.mean(x * x, axis=-1, keepdims=True) + NORM_EPS) * g


def _row_group(b):
    return pl.multiple_of(lax.shift_left(lax.shift_right_logical(b, 3), 3), 8), b & 7


def _load_row(ref, b, cols):
    base, r = _row_group(b)
    blk = ref[pl.ds(base, 8), cols]
    rows = lax.broadcasted_iota(I32, blk.shape, 0)
    return jnp.sum(jnp.where(rows == r, blk, 0.0), axis=0, keepdims=True)


def _store_row(ref, b, cols, row):
    base, r = _row_group(b)
    blk = ref[pl.ds(base, 8), cols]
    rows = lax.broadcasted_iota(I32, blk.shape, 0)
    ref[pl.ds(base, 8), cols] = jnp.where(rows == r, row, blk)


def _norm_matmul_kernel(x_ref, g_ref, w_ref, o_ref, h_ref):
    @pl.when(pl.program_id(1) == 0)
    def _():
        h_ref[...] = _rms(x_ref[...], g_ref[...]).astype(BF16)

    o_ref[...] = jnp.dot(h_ref[...], w_ref[...], preferred_element_type=F32).astype(o_ref.dtype)


def norm_matmul(x, g, w, tm, tn):
    t, d = x.shape
    n = w.shape[1]
    return pl.pallas_call(
        _norm_matmul_kernel,
        grid=(t // tm, n // tn),
        in_specs=[pl.BlockSpec((tm, d), lambda i, j: (i, 0)),
                  pl.BlockSpec((1, d), lambda i, j: (0, 0)),
                  pl.BlockSpec((d, tn), lambda i, j: (0, j))],
        out_specs=pl.BlockSpec((tm, tn), lambda i, j: (i, j)),
        out_shape=jax.ShapeDtypeStruct((t, n), BF16),
        scratch_shapes=[pltpu.VMEM((tm, d), BF16)],
        compiler_params=_cparams("parallel", "arbitrary"),
        name="norm_matmul",
    )(x, g.reshape(1, d), w)


def _matmul_res_kernel(a_ref, w_ref, x_ref, o_ref):
    o_ref[...] = x_ref[...] + jnp.dot(a_ref[...], w_ref[...], preferred_element_type=F32)


def matmul_residual(a, w, x, tm):
    t, k = a.shape
    n = w.shape[1]
    return pl.pallas_call(
        _matmul_res_kernel,
        grid=(t // tm,),
        in_specs=[pl.BlockSpec((tm, k), lambda i: (i, 0)),
                  pl.BlockSpec((k, n), lambda i: (0, 0)),
                  pl.BlockSpec((tm, n), lambda i: (i, 0))],
        out_specs=pl.BlockSpec((tm, n), lambda i: (i, 0)),
        out_shape=jax.ShapeDtypeStruct((t, n), F32),
        compiler_params=_cparams("parallel"),
        name="matmul_residual",
    )(a, w, x)


def _ffn_kernel(x_ref, g_ref, wg_ref, wu_ref, wd_ref, o_ref, h_ref):
    f = pl.program_id(1)

    @pl.when(f == 0)
    def _():
        x = x_ref[...]
        h_ref[...] = _rms(x, g_ref[...]).astype(BF16)
        o_ref[...] = x

    h = h_ref[...]
    a = jnp.dot(h, wg_ref[...], preferred_element_type=F32)
    u = jnp.dot(h, wu_ref[...], preferred_element_type=F32)
    act = (_silu(a) * u).astype(BF16)
    o_ref[...] += jnp.dot(act, wd_ref[...], preferred_element_type=F32)


def ffn_residual(x, g, wg, wu, wd, tm, tf):
    t, d = x.shape
    f = wg.shape[1]
    return pl.pallas_call(
        _ffn_kernel,
        grid=(t // tm, f // tf),
        in_specs=[pl.BlockSpec((tm, d), lambda i, j: (i, 0)),
                  pl.BlockSpec((1, d), lambda i, j: (0, 0)),
                  pl.BlockSpec((d, tf), lambda i, j: (0, j)),
                  pl.BlockSpec((d, tf), lambda i, j: (0, j)),
                  pl.BlockSpec((tf, d), lambda i, j: (j, 0))],
        out_specs=pl.BlockSpec((tm, d), lambda i, j: (i, 0)),
        out_shape=jax.ShapeDtypeStruct((t, d), F32),
        scratch_shapes=[pltpu.VMEM((tm, d), BF16)],
        compiler_params=_cparams("parallel", "arbitrary"),
        name="ffn_residual",
    )(x, g.reshape(1, d), wg, wu, wd)


def _rotary(x, cos, sin, half):
    x1 = x[:, :half]
    x2 = x[:, half:]
    return jnp.concatenate([x1 * cos - x2 * sin, x2 * cos + x1 * sin], axis=-1)


def _group_norm_gate(o, gate, gain):
    mu = jnp.mean(o, axis=-1, keepdims=True)
    d = o - mu
    var = jnp.mean(d * d, axis=-1, keepdims=True)
    return _silu(gate) * (d * lax.rsqrt(var + GN_EPS) * gain)


def _ret_scan_kernel(lg_ref, q_ref, k_ref, v_ref, g_ref, cos_ref, sin_ref, gn_ref,
                     o_ref, sfin_ref, s_ref, *, chunk, dk, dv):
    c = pl.program_id(2)

    @pl.when(c == 0)
    def _():
        s_ref[...] = jnp.zeros_like(s_ref)

    lg = lg_ref[pl.program_id(1)]
    half = dk // 2
    cos = cos_ref[...]
    sin = sin_ref[...]
    q = _rotary(q_ref[...].astype(F32), cos, sin, half)
    k = _rotary(k_ref[...].astype(F32), cos, sin, half) * (dk ** -0.5)
    v = v_ref[...]

    t = lax.broadcasted_iota(I32, (chunk, 1), 0).astype(F32)
    decay_in = jnp.exp((t + 1.0) * lg)
    decay_out = jnp.exp((chunk - 1.0 - t) * lg)
    ti = lax.broadcasted_iota(I32, (chunk, chunk), 0)
    si = lax.broadcasted_iota(I32, (chunk, chunk), 1)
    causal = ti >= si
    rel = (ti - si).astype(F32)
    mask = jnp.where(causal, jnp.exp(jnp.where(causal, rel, 0.0) * lg), 0.0)
    decay_chunk = jnp.exp(jnp.zeros((1, dv), F32) + chunk * lg)

    s = s_ref[...]
    scores = lax.dot_general(q.astype(BF16), k.astype(BF16), NT_DIMS, preferred_element_type=F32) * mask
    o = (jnp.dot(scores.astype(BF16), v, preferred_element_type=F32)
         + jnp.dot((q * decay_in).astype(BF16), s.astype(BF16), preferred_element_type=F32))
    kd_t = (k * decay_out).T.astype(BF16)
    s_new = s * decay_chunk + jnp.dot(kd_t, v, preferred_element_type=F32)
    s_ref[...] = s_new

    o_ref[...] = _group_norm_gate(o, g_ref[...].astype(F32), gn_ref[...]).astype(BF16)

    @pl.when(c == pl.num_programs(2) - 1)
    def _():
        sfin_ref[0, 0] = s_new


def retention_scan(proj, log_gamma, cos, sin, gn_g, *, batch, seq, heads, dk, dv, rows_total):
    chunk = min(RET_CHUNK, seq)
    nc = seq // chunk
    kern = functools.partial(_ret_scan_kernel, chunk=chunk, dk=dk, dv=dv)
    qk_blocks = heads * dk // dv
    return pl.pallas_call(
        kern,
        grid=(batch, heads, nc),
        in_specs=[pl.BlockSpec(memory_space=pltpu.SMEM),
                  pl.BlockSpec((chunk, dk), lambda b, h, c: (b * nc + c, h)),
                  pl.BlockSpec((chunk, dk), lambda b, h, c: (b * nc + c, heads + h)),
                  pl.BlockSpec((chunk, dv), lambda b, h, c: (b * nc + c, 2 * qk_blocks + h)),
                  pl.BlockSpec((chunk, dv), lambda b, h, c: (b * nc + c, 2 * qk_blocks + heads + h)),
                  pl.BlockSpec((chunk, dk // 2), lambda b, h, c: (c, 0)),
                  pl.BlockSpec((chunk, dk // 2), lambda b, h, c: (c, 0)),
                  pl.BlockSpec((1, dv), lambda b, h, c: (0, h))],
        out_specs=[pl.BlockSpec((chunk, dv), lambda b, h, c: (b * nc + c, h)),
                   pl.BlockSpec((1, 1, dk, dv), lambda b, h, c: (b, h, 0, 0))],
        out_shape=[jax.ShapeDtypeStruct((rows_total, heads * dv), BF16),
                   jax.ShapeDtypeStruct((batch, heads, dk, dv), F32)],
        scratch_shapes=[pltpu.VMEM((dk, dv), F32)],
        compiler_params=_cparams("parallel", "parallel", "arbitrary"),
        name="retention_scan",
    )(log_gamma, proj, proj, proj, proj, cos, sin, gn_g.reshape(1, heads * dv))


def _one_hot_rows(nb, b):
    return jnp.where(lax.broadcasted_iota(I32, (nb, LANES), 0) == b, 1.0, 0.0).astype(BF16)


def _lane_tile(x, width):
    return jnp.concatenate([x] * (width // x.shape[1]), axis=1)


def _ret_step_kernel(lg_ref, q_ref, k_ref, v_ref, g_ref, cos_ref, sin_ref, gn_ref, s_ref, prev_ref,
                     o_ref, snew_ref, qt_ref, kt_ref, vf_ref, gf_ref, orow_ref, *, heads, dk, dv, nb):
    del prev_ref
    b = pl.program_id(0)
    half = dk // 2

    @pl.when(b == 0)
    def _():
        orow_ref[...] = jnp.zeros_like(orow_ref)
        vf_ref[...] = v_ref[...].astype(F32)
        gf_ref[...] = g_ref[...].astype(F32)
        cos = cos_ref[...]
        sin = sin_ref[...]
        for h in range(heads):
            qh = _rotary(q_ref[:, h * dk:(h + 1) * dk].astype(F32), cos, sin, half)
            kh = _rotary(k_ref[:, h * dk:(h + 1) * dk].astype(F32), cos, sin, half) * (dk ** -0.5)
            qt_ref[h] = qh.T.astype(BF16)
            kt_ref[h] = kh.T.astype(BF16)

    onehot = _one_hot_rows(nb, b)
    for h in range(heads):
        cols = slice(h * dv, (h + 1) * dv)
        kcol = _lane_tile(jnp.dot(kt_ref[h], onehot, preferred_element_type=F32), dv)
        qcol = _lane_tile(jnp.dot(qt_ref[h], onehot, preferred_element_type=F32), dv)
        vrow = _load_row(vf_ref, b, cols)
        gamma = jnp.exp(jnp.zeros((1, dv), F32) + lg_ref[h])
        s_new = s_ref[0, h] * gamma + kcol * vrow
        snew_ref[0, h] = s_new
        o = jnp.sum(s_new * qcol, axis=0, keepdims=True)
        _store_row(orow_ref, b, cols, _group_norm_gate(o, _load_row(gf_ref, b, cols), gn_ref[:, cols]))

    @pl.when(b == pl.num_programs(0) - 1)
    def _():
        o_ref[...] = orow_ref[...].astype(BF16)


def retention_step(proj, gated, state, log_gamma, cos, sin, gn_g, *, row0, nb, heads, dk, dv):
    kern = functools.partial(_ret_step_kernel, heads=heads, dk=dk, dv=dv, nb=nb)
    rb = row0 // nb
    wq = heads * dk
    wv = heads * dv
    return pl.pallas_call(
        kern,
        grid=(nb,),
        in_specs=[pl.BlockSpec(memory_space=pltpu.SMEM),
                  pl.BlockSpec((nb, wq), lambda b: (rb, 0)),
                  pl.BlockSpec((nb, wq), lambda b: (rb, 1)),
                  pl.BlockSpec((nb, wv), lambda b: (rb, 2 * wq // wv)),
                  pl.BlockSpec((nb, wv), lambda b: (rb, 2 * wq // wv + 1)),
                  pl.BlockSpec((1, dk // 2), lambda b: (0, 0)),
                  pl.BlockSpec((1, dk // 2), lambda b: (0, 0)),
                  pl.BlockSpec((1, wv), lambda b: (0, 0)),
                  pl.BlockSpec((1, heads, dk, dv), lambda b: (b, 0, 0, 0)),
                  pl.BlockSpec(memory_space=pl.ANY)],
        out_specs=[pl.BlockSpec((nb, wv), lambda b: (rb, 0)),
                   pl.BlockSpec((1, heads, dk, dv), lambda b: (b, 0, 0, 0))],
        out_shape=[jax.ShapeDtypeStruct(gated.shape, gated.dtype),
                   jax.ShapeDtypeStruct(state.shape, F32)],
        scratch_shapes=[pltpu.VMEM((heads, dk, nb), BF16),
                        pltpu.VMEM((heads, dk, nb), BF16),
                        pltpu.VMEM((nb, wv), F32),
                        pltpu.VMEM((nb, wv), F32),
                        pltpu.VMEM((nb, wv), F32)],
        input_output_aliases={9: 0},
        compiler_params=_cparams("arbitrary"),
        name="retention_step",
    )(log_gamma, proj, proj, proj, proj, cos, sin, gn_g.reshape(1, wv), state, gated)


def _lower_bound(lbp, layer):
    m = jnp.max(lbp, axis=0, keepdims=True)
    e = jnp.exp(lbp - m)
    p = e / jnp.sum(e, axis=0, keepdims=True)
    return jnp.sum(p[:layer + 1], axis=0, keepdims=True) - p[0:1]


def _split_dot(mat_bf16, x):
    hi = x.astype(BF16)
    r1 = x - hi.astype(F32)
    mid = r1.astype(BF16)
    lo = (r1 - mid.astype(F32)).astype(BF16)
    return (jnp.dot(mat_bf16, hi, preferred_element_type=F32)
            + jnp.dot(mat_bf16, mid, preferred_element_type=F32)
            + jnp.dot(mat_bf16, lo, preferred_element_type=F32))


def _rms_gate(o, gate, gain):
    return _silu(gate) * (o * lax.rsqrt(jnp.mean(o * o, axis=-1, keepdims=True) + NORM_EPS) * gain)


def _group_row(x, s, group):
    n, w = x.shape
    x3 = x.reshape(n // group, group, w)
    return jnp.broadcast_to(x3[:, s:s + 1, :], x3.shape).reshape(n, w)


def _hgrn_scan_kernel(lbp_ref, q_ref, f_ref, i_ref, g_ref, ng_ref, o_ref, sfin_ref, st_ref,
                      *, chunk, heads, dk, diag, layer):
    c = pl.program_id(1)

    @pl.when(c == 0)
    def _():
        st_ref[...] = jnp.zeros_like(st_ref)

    w = heads * dk
    lb = _lower_bound(lbp_ref[...], layer)
    qs = _silu(q_ref[...].astype(F32))
    forget = lb + (1.0 - lb) * _sigmoid(f_ref[...].astype(F32))
    kk = 1.0 - forget
    logf = jnp.log(forget)

    row = lax.broadcasted_iota(I32, (chunk, chunk), 0)
    col = lax.broadcasted_iota(I32, (chunk, chunk), 1)
    tril = jnp.where(row >= col, 1.0, 0.0).astype(BF16)
    bcum = _split_dot(tril, logf)
    blast = bcum[chunk - 1:chunk, :]
    qe = (qs * jnp.exp(bcum)).astype(BF16)
    kdec = (kk * jnp.exp(blast - bcum)).astype(BF16)
    iv = i_ref[...]
    gate = g_ref[...].astype(F32)
    gain = ng_ref[...]

    u = jnp.where(row > col, row ^ col, 0)
    rowv = lax.broadcasted_iota(I32, (chunk, 1), 0)
    levels = []
    m = chunk // 2
    while m >= diag:
        span = 2 * m
        bref = jnp.concatenate(
            [jnp.broadcast_to(bcum[g * span + m - 1:g * span + m], (span, w)) for g in range(chunk // span)],
            axis=0)
        upper = (rowv & m) != 0
        d = bcum - bref
        x = (jnp.where(upper, qs, kk) * jnp.exp(jnp.where(upper, d, -d))).astype(BF16)
        levels.append((x, lax.shift_right_logical(u, m.bit_length() - 1) == 1))
        m //= 2
    in_block = row & (diag - 1)
    block_col0 = row - in_block
    diag_masks = [(col == block_col0 + s) & (in_block >= s) for s in range(diag)]
    ones = jnp.ones((dk, LANES), BF16)

    for h in range(heads):
        cols = slice(h * dk, (h + 1) * dk)
        qs_h = qs[:, cols]
        kk_h = kk[:, cols]
        b_h = bcum[:, cols]
        a = jnp.zeros((chunk, chunk), F32)
        for x, mask in levels:
            x_h = x[:, cols]
            a = jnp.where(mask, lax.dot_general(x_h, x_h, NT_DIMS, preferred_element_type=F32), a)
        ws = [qs_h * _group_row(kk_h, s, diag) * jnp.exp(b_h - _group_row(b_h, s, diag)) for s in range(diag)]
        rsum = jnp.dot(jnp.concatenate(ws, axis=0).astype(BF16), ones, preferred_element_type=F32)
        for s in range(diag):
            a = jnp.where(diag_masks[s], rsum[s * chunk:(s + 1) * chunk], a)
        st = st_ref[h]
        i_h = iv[:, cols]
        o_h = (jnp.dot(a.astype(BF16), i_h, preferred_element_type=F32)
               + lax.dot_general(qe[:, cols], st.astype(BF16), NT_DIMS, preferred_element_type=F32))
        st_new = (st * jnp.exp(blast[:, cols])
                  + lax.dot_general(i_h, kdec[:, cols], TN_DIMS, preferred_element_type=F32))
        st_ref[h] = st_new
        o_ref[:, cols] = _rms_gate(o_h, gate[:, cols], gain[:, cols]).astype(BF16)

    @pl.when(c == pl.num_programs(1) - 1)
    def _():
        for h in range(heads):
            sfin_ref[0, h] = st_ref[h].T


def hgrn_scan(proj, lb_param, norm_g, *, batch, seq, heads, dk, rows_total, layer):
    chunk = min(HG_CHUNK, seq)
    assert chunk == LANES and dk == LANES
    nc = seq // chunk
    w = heads * dk
    kern = functools.partial(_hgrn_scan_kernel, chunk=chunk, heads=heads, dk=dk, diag=HG_DIAG, layer=layer)
    return pl.pallas_call(
        kern,
        grid=(batch, nc),
        in_specs=[pl.BlockSpec(lb_param.shape, lambda b, c: (0, 0)),
                  pl.BlockSpec((chunk, w), lambda b, c: (b * nc + c, 0)),
                  pl.BlockSpec((chunk, w), lambda b, c: (b * nc + c, 1)),
                  pl.BlockSpec((chunk, w), lambda b, c: (b * nc + c, 2)),
                  pl.BlockSpec((chunk, w), lambda b, c: (b * nc + c, 3)),
                  pl.BlockSpec((1, w), lambda b, c: (0, 0))],
        out_specs=[pl.BlockSpec((chunk, w), lambda b, c: (b * nc + c, 0)),
                   pl.BlockSpec((1, heads, dk, dk), lambda b, c: (b, 0, 0, 0))],
        out_shape=[jax.ShapeDtypeStruct((rows_total, w), BF16),
                   jax.ShapeDtypeStruct((batch, heads, dk, dk), F32)],
        scratch_shapes=[pltpu.VMEM((heads, dk, dk), F32)],
        compiler_params=_cparams("parallel", "arbitrary"),
        name="hgrn_scan",
    )(lb_param, proj, proj, proj, proj, norm_g.reshape(1, w))


def _hgrn_step_kernel(lbp_ref, q_ref, f_ref, i_ref, g_ref, ng_ref, s_ref, prev_ref,
                      o_ref, snew_ref, qt_ref, ft_ref, if_ref, gf_ref, orow_ref, *, heads, dk, nb, layer):
    del prev_ref
    b = pl.program_id(0)

    @pl.when(b == 0)
    def _():
        orow_ref[...] = jnp.zeros_like(orow_ref)
        if_ref[...] = i_ref[...].astype(F32)
        gf_ref[...] = g_ref[...].astype(F32)
        lb = _lower_bound(lbp_ref[...], layer)
        qs = _silu(q_ref[...].astype(F32))
        forget = lb + (1.0 - lb) * _sigmoid(f_ref[...].astype(F32))
        for h in range(heads):
            cols = slice(h * dk, (h + 1) * dk)
            qt_ref[h] = qs[:, cols].T.astype(BF16)
            ft = forget[:, cols].T
            hi = ft.astype(BF16)
            r1 = ft - hi.astype(F32)
            mid = r1.astype(BF16)
            ft_ref[0, h] = hi
            ft_ref[1, h] = mid
            ft_ref[2, h] = (r1 - mid.astype(F32)).astype(BF16)

    onehot = _one_hot_rows(nb, b)
    for h in range(heads):
        cols = slice(h * dk, (h + 1) * dk)
        fcol = (jnp.dot(ft_ref[0, h], onehot, preferred_element_type=F32)
                + jnp.dot(ft_ref[1, h], onehot, preferred_element_type=F32)
                + jnp.dot(ft_ref[2, h], onehot, preferred_element_type=F32))
        qcol = jnp.dot(qt_ref[h], onehot, preferred_element_type=F32)
        irow = _load_row(if_ref, b, cols)
        s_new = s_ref[0, h] * fcol + (1.0 - fcol) * irow
        snew_ref[0, h] = s_new
        o = jnp.sum(s_new * qcol, axis=0, keepdims=True)
        _store_row(orow_ref, b, cols, _rms_gate(o, _load_row(gf_ref, b, cols), ng_ref[:, cols]))

    @pl.when(b == pl.num_programs(0) - 1)
    def _():
        o_ref[...] = orow_ref[...].astype(BF16)


def hgrn_step(proj, gated, state, lb_param, norm_g, *, row0, nb, heads, dk, layer):
    kern = functools.partial(_hgrn_step_kernel, heads=heads, dk=dk, nb=nb, layer=layer)
    rb = row0 // nb
    w = heads * dk
    return pl.pallas_call(
        kern,
        grid=(nb,),
        in_specs=[pl.BlockSpec(lb_param.shape, lambda b: (0, 0)),
                  pl.BlockSpec((nb, w), lambda b: (rb, 0)),
                  pl.BlockSpec((nb, w), lambda b: (rb, 1)),
                  pl.BlockSpec((nb, w), lambda b: (rb, 2)),
                  pl.BlockSpec((nb, w), lambda b: (rb, 3)),
                  pl.BlockSpec((1, w), lambda b: (0, 0)),
                  pl.BlockSpec((1, heads, dk, dk), lambda b: (b, 0, 0, 0)),
                  pl.BlockSpec(memory_space=pl.ANY)],
        out_specs=[pl.BlockSpec((nb, w), lambda b: (rb, 0)),
                   pl.BlockSpec((1, heads, dk, dk), lambda b: (b, 0, 0, 0))],
        out_shape=[jax.ShapeDtypeStruct(gated.shape, gated.dtype),
                   jax.ShapeDtypeStruct(state.shape, F32)],
        scratch_shapes=[pltpu.VMEM((heads, dk, nb), BF16),
                        pltpu.VMEM((3, heads, dk, nb), BF16),
                        pltpu.VMEM((nb, w), F32),
                        pltpu.VMEM((nb, w), F32),
                        pltpu.VMEM((nb, w), F32)],
        input_output_aliases={7: 0},
        compiler_params=_cparams("arbitrary"),
        name="hgrn_step",
    )(lb_param, proj, proj, proj, proj, norm_g.reshape(1, w), state, gated)


def _router_kernel(x_ref, g_ref, r_ref, h_ref, route_ref, cnt_ref, carry_ref, *, tm, experts):
    i = pl.program_id(0)

    @pl.when(i == 0)
    def _():
        carry_ref[...] = jnp.zeros_like(carry_ref)

    h = _rms(x_ref[...], g_ref[...])
    h_ref[...] = h
    logits = jnp.dot(h, r_ref[...], preferred_element_type=F32, precision=lax.Precision.HIGHEST)
    lane = lax.broadcasted_iota(I32, (tm, LANES), 1)
    valid = lane < experts
    z = jnp.where(valid, logits, -jnp.inf)
    ez = jnp.exp(z - jnp.max(z, axis=-1, keepdims=True))
    p = ez / jnp.sum(ez, axis=-1, keepdims=True)
    p = jnp.where(valid, p, -1.0)
    lane_f = lane.astype(F32)
    v1 = jnp.max(p, axis=-1, keepdims=True)
    i1 = jnp.min(jnp.where(p == v1, lane_f, float(LANES)), axis=-1, keepdims=True)
    p2 = jnp.where(lane_f == i1, -1.0, p)
    v2 = jnp.max(p2, axis=-1, keepdims=True)
    i2 = jnp.min(jnp.where(p2 == v2, lane_f, float(LANES)), axis=-1, keepdims=True)
    den = v1 + v2
    m0 = jnp.where(lane_f == i1, 1.0, 0.0)
    m1 = jnp.where(lane_f == i2, 1.0, 0.0)
    msum = m0 + m1
    ri = lax.broadcasted_iota(I32, (tm, tm), 0)
    ci = lax.broadcasted_iota(I32, (tm, tm), 1)
    strict = jnp.where(ri > ci, 1.0, 0.0).astype(BF16)
    before = jnp.dot(strict, msum.astype(BF16), preferred_element_type=F32) + carry_ref[...]
    rank0 = jnp.sum(m0 * before, axis=-1, keepdims=True)
    rank1 = jnp.sum(m1 * before, axis=-1, keepdims=True)
    total = carry_ref[...] + jnp.sum(msum, axis=0, keepdims=True)
    carry_ref[...] = total
    cnt_ref[...] = jnp.broadcast_to(total, cnt_ref.shape)
    route = jnp.where(lane == 0, i1,
            jnp.where(lane == 1, i2,
            jnp.where(lane == 2, v1 / den,
            jnp.where(lane == 3, v2 / den,
            jnp.where(lane == 4, rank0,
            jnp.where(lane == 5, rank1, 0.0))))))
    route_ref[...] = route


def router(x, g, router_w, tm):
    t, d = x.shape
    experts = router_w.shape[1]
    rpad = jnp.zeros((d, LANES), F32).at[:, :experts].set(router_w)
    kern = functools.partial(_router_kernel, tm=tm, experts=experts)
    return pl.pallas_call(
        kern,
        grid=(t // tm,),
        in_specs=[pl.BlockSpec((tm, d), lambda i: (i, 0)),
                  pl.BlockSpec((1, d), lambda i: (0, 0)),
                  pl.BlockSpec((d, LANES), lambda i: (0, 0))],
        out_specs=[pl.BlockSpec((tm, d), lambda i: (i, 0)),
                   pl.BlockSpec((tm, LANES), lambda i: (i, 0)),
                   pl.BlockSpec((8, LANES), lambda i: (0, 0))],
        out_shape=[jax.ShapeDtypeStruct((t, d), F32),
                   jax.ShapeDtypeStruct((t, LANES), F32),
                   jax.ShapeDtypeStruct((8, LANES), F32)],
        scratch_shapes=[pltpu.VMEM((1, LANES), F32)],
        compiler_params=_cparams("arbitrary"),
        name="router",
    )(x, g.reshape(1, d), rpad)


def _plan_kernel(cnt_ref, off_ref, frow_ref, fexp_ref, nfull_ref, trow_ref, tsub_ref,
                 *, experts, nfull_max, dummy, row_tile, sub):
    tile0 = jnp.int32(0)
    nfull = jnp.int32(0)
    last = jnp.int32(0)
    for e in range(experts):
        n = cnt_ref[e]
        off_ref[e] = tile0 * row_tile
        full = n // row_tile
        rem = n - full * row_tile

        def fill(j, carry, e=e, tile0=tile0, nfull=nfull):
            frow_ref[nfull + j] = tile0 + j
            fexp_ref[nfull + j] = e
            return carry

        lax.fori_loop(0, full, fill, 0)
        trow_ref[e] = jnp.where(rem > 0, tile0 + full, dummy)
        tsub_ref[e] = (rem + (sub - 1)) // sub
        last = jnp.where(full > 0, e, last)
        nfull = nfull + full
        tile0 = tile0 + full + jnp.where(rem > 0, 1, 0)
    nfull_ref[0] = nfull

    def unused(j, carry):
        frow_ref[j] = dummy
        fexp_ref[j] = last
        return carry

    lax.fori_loop(nfull, nfull_max, unused, 0)


def plan(counts, ntiles, nfull_max):
    experts = counts.shape[0]
    kern = functools.partial(_plan_kernel, experts=experts, nfull_max=nfull_max, dummy=ntiles - 1,
                             row_tile=MOE_ROW_TILE, sub=MOE_SUB)
    smem = pl.BlockSpec(memory_space=pltpu.SMEM)
    return pl.pallas_call(
        kern,
        in_specs=[smem],
        out_specs=[smem] * 6,
        out_shape=[jax.ShapeDtypeStruct((experts,), I32),
                   jax.ShapeDtypeStruct((nfull_max,), I32),
                   jax.ShapeDtypeStruct((nfull_max,), I32),
                   jax.ShapeDtypeStruct((1,), I32),
                   jax.ShapeDtypeStruct((experts,), I32),
                   jax.ShapeDtypeStruct((experts,), I32)],
        name="moe_plan",
    )(counts)


def _dispatch_kernel(off_ref, e_ref, r_ref, h_ref, init_hbm, out_hbm, sem, *, tm):
    del init_hbm

    def row_copy(src, dst):
        return pltpu.make_async_copy(h_ref.at[pl.ds(src, 1)], out_hbm.at[pl.ds(dst, 1)], sem)

    def issue(j, carry):
        dst = off_ref[e_ref[0, 0, j]] + r_ref[0, 0, j]
        row_copy(lax.shift_right_logical(j, 1), dst).start()
        return carry

    lax.fori_loop(0, TOP_K * tm, issue, 0, unroll=DMA_UNROLL)

    def drain(j, carry):
        row_copy(0, 0).wait()
        return carry

    lax.fori_loop(0, TOP_K * tm, drain, 0, unroll=DMA_UNROLL)


def dispatch(h, off, eidx, rank, rows, tm):
    t, d = h.shape
    nt = t // tm
    kern = functools.partial(_dispatch_kernel, tm=tm)
    smem_blk = pl.BlockSpec((1, 1, TOP_K * tm), lambda i, off: (i, 0, 0), memory_space=pltpu.SMEM)
    return pl.pallas_call(
        kern,
        grid_spec=pltpu.PrefetchScalarGridSpec(
            num_scalar_prefetch=1,
            grid=(nt,),
            in_specs=[smem_blk, smem_blk,
                      pl.BlockSpec((tm, d), lambda i, off: (i, 0)),
                      pl.BlockSpec(memory_space=pl.ANY)],
            out_specs=pl.BlockSpec(memory_space=pl.ANY),
            scratch_shapes=[pltpu.SemaphoreType.DMA(())]),
        out_shape=jax.ShapeDtypeStruct((rows, d), h.dtype),
        input_output_aliases={4: 0},
        compiler_params=_cparams("arbitrary"),
        name="moe_dispatch",
    )(off, eidx.reshape(nt, 1, TOP_K * tm), rank.reshape(nt, 1, TOP_K * tm), h,
      jnp.zeros((rows, d), h.dtype))


def _expert_full_kernel(nfull_ref, frow_ref, fexp_ref, x_ref, wg_ref, wu_ref, wd_ref, o_ref, h_ref):
    del frow_ref, fexp_ref
    j = pl.program_id(0)
    f = pl.program_id(1)
    active = j < nfull_ref[0]

    @pl.when(f == 0)
    def _():
        h_ref[...] = x_ref[...].astype(BF16)
        o_ref[...] = jnp.zeros_like(o_ref)

    @pl.when(active)
    def _():
        h = h_ref[...]
        a = jnp.dot(h, wg_ref[0].astype(BF16), preferred_element_type=F32)
        u = jnp.dot(h, wu_ref[0].astype(BF16), preferred_element_type=F32)
        act = (_silu(a) * u).astype(BF16)
        o_ref[...] += jnp.dot(act, wd_ref[0].astype(BF16), preferred_element_type=F32)


def _expert_tail_kernel(trow_ref, tsub_ref, x_ref, wg_ref, wu_ref, wd_ref, prev_ref, o_ref,
                        h_ref, wgb_ref, wub_ref, wdb_ref, *, sub, nsub_max):
    del trow_ref, prev_ref
    f = pl.program_id(1)
    nsub = tsub_ref[pl.program_id(0)]

    @pl.when(f == 0)
    def _():
        h_ref[...] = x_ref[...].astype(BF16)
        o_ref[...] = jnp.zeros_like(o_ref)

    @pl.when(nsub > 0)
    def _():
        wgb_ref[...] = wg_ref[0].astype(BF16)
        wub_ref[...] = wu_ref[0].astype(BF16)
        wdb_ref[...] = wd_ref[0].astype(BF16)
        for sb in range(nsub_max):
            @pl.when(sb < nsub)
            def _():
                rows = slice(sb * sub, (sb + 1) * sub)
                h = h_ref[rows]
                a = jnp.dot(h, wgb_ref[...], preferred_element_type=F32)
                u = jnp.dot(h, wub_ref[...], preferred_element_type=F32)
                act = (_silu(a) * u).astype(BF16)
                o_ref[rows] += jnp.dot(act, wdb_ref[...], preferred_element_type=F32)


def expert_ffn(xs, nfull, frow, fexp, trow, tsub, wg, wu, wd):
    rows, d = xs.shape
    experts, _, fdim = wg.shape
    tf = MOE_F_TILE
    nf = fdim // tf
    nfull_max = frow.shape[0]

    def fcol_full(j, f, nfull):
        return jnp.where(j < nfull[0], f, nf - 1)

    ys = pl.pallas_call(
        _expert_full_kernel,
        grid_spec=pltpu.PrefetchScalarGridSpec(
            num_scalar_prefetch=3,
            grid=(nfull_max, nf),
            in_specs=[pl.BlockSpec((MOE_ROW_TILE, d), lambda j, f, nfull, frow, fexp: (frow[j], 0)),
                      pl.BlockSpec((1, d, tf), lambda j, f, nfull, frow, fexp: (fexp[j], 0, fcol_full(j, f, nfull))),
                      pl.BlockSpec((1, d, tf), lambda j, f, nfull, frow, fexp: (fexp[j], 0, fcol_full(j, f, nfull))),
                      pl.BlockSpec((1, tf, d), lambda j, f, nfull, frow, fexp: (fexp[j], fcol_full(j, f, nfull), 0))],
            out_specs=pl.BlockSpec((MOE_ROW_TILE, d), lambda j, f, nfull, frow, fexp: (frow[j], 0)),
            scratch_shapes=[pltpu.VMEM((MOE_ROW_TILE, d), BF16)]),
        out_shape=jax.ShapeDtypeStruct((rows, d), F32),
        compiler_params=_cparams("arbitrary", "arbitrary"),
        name="expert_ffn_full",
    )(nfull, frow, fexp, xs, wg, wu, wd)

    def fcol_tail(e, f, tsub):
        return jnp.where(tsub[e] > 0, f, nf - 1)

    kern = functools.partial(_expert_tail_kernel, sub=MOE_SUB, nsub_max=MOE_ROW_TILE // MOE_SUB)
    return pl.pallas_call(
        kern,
        grid_spec=pltpu.PrefetchScalarGridSpec(
            num_scalar_prefetch=2,
            grid=(experts, nf),
            in_specs=[pl.BlockSpec((MOE_ROW_TILE, d), lambda e, f, trow, tsub: (trow[e], 0)),
                      pl.BlockSpec((1, d, tf), lambda e, f, trow, tsub: (e, 0, fcol_tail(e, f, tsub))),
                      pl.BlockSpec((1, d, tf), lambda e, f, trow, tsub: (e, 0, fcol_tail(e, f, tsub))),
                      pl.BlockSpec((1, tf, d), lambda e, f, trow, tsub: (e, fcol_tail(e, f, tsub), 0)),
                      pl.BlockSpec(memory_space=pl.ANY)],
            out_specs=pl.BlockSpec((MOE_ROW_TILE, d), lambda e, f, trow, tsub: (trow[e], 0)),
            scratch_shapes=[pltpu.VMEM((MOE_ROW_TILE, d), BF16),
                            pltpu.VMEM((d, tf), BF16),
                            pltpu.VMEM((d, tf), BF16),
                            pltpu.VMEM((tf, d), BF16)]),
        out_shape=jax.ShapeDtypeStruct((rows, d), F32),
        input_output_aliases={6: 0},
        compiler_params=_cparams("arbitrary", "arbitrary"),
        name="expert_ffn_tail",
    )(trow, tsub, xs, wg, wu, wd, ys)


def _combine_kernel(off_ref, e_ref, r_ref, x_ref, route_ref, fg_ref, ys_hbm, o_ref, gath_ref, sem, *, tm):
    def row_copy(src, k, t):
        return pltpu.make_async_copy(ys_hbm.at[pl.ds(src, 1)], gath_ref.at[k, pl.ds(t, 1)], sem)

    def issue(j, carry):
        src = off_ref[e_ref[0, 0, j]] + r_ref[0, 0, j]
        row_copy(src, j & 1, lax.shift_right_logical(j, 1)).start()
        return carry

    lax.fori_loop(0, TOP_K * tm, issue, 0, unroll=DMA_UNROLL)

    def drain(j, carry):
        row_copy(0, 0, 0).wait()
        return carry

    lax.fori_loop(0, TOP_K * tm, drain, 0, unroll=DMA_UNROLL)
    route = route_ref[...]
    x = x_ref[...] + route[:, 2:3] * gath_ref[0] + route[:, 3:4] * gath_ref[1]
    o_ref[...] = _rms(x, fg_ref[...])


def combine(x, route, ys, off, eidx, rank, final_g, tm):
    t, d = x.shape
    nt = t // tm
    kern = functools.partial(_combine_kernel, tm=tm)
    smem_blk = pl.BlockSpec((1, 1, TOP_K * tm), lambda i, off: (i, 0, 0), memory_space=pltpu.SMEM)
    return pl.pallas_call(
        kern,
        grid_spec=pltpu.PrefetchScalarGridSpec(
            num_scalar_prefetch=1,
            grid=(nt,),
            in_specs=[smem_blk, smem_blk,
                      pl.BlockSpec((tm, d), lambda i, off: (i, 0)),
                      pl.BlockSpec((tm, LANES), lambda i, off: (i, 0)),
                      pl.BlockSpec((1, d), lambda i, off: (0, 0)),
                      pl.BlockSpec(memory_space=pl.ANY)],
            out_specs=pl.BlockSpec((tm, d), lambda i, off: (i, 0)),
            scratch_shapes=[pltpu.VMEM((TOP_K, tm, d), F32),
                            pltpu.SemaphoreType.DMA(())]),
        out_shape=jax.ShapeDtypeStruct((t, d), F32),
        compiler_params=_cparams("arbitrary"),
        name="moe_combine",
    )(off, eidx.reshape(nt, 1, TOP_K * tm), rank.reshape(nt, 1, TOP_K * tm), x, route,
      final_g.reshape(1, d), ys)


def moe_residual_final_norm(x, norm_g, router_w, wg, wu, wd, final_g, tm):
    t, d = x.shape
    experts = router_w.shape[1]
    h, route, cnt = router(x, norm_g, router_w, tm)
    counts = cnt[0, :experts].astype(I32)
    eidx = route[:, 0:TOP_K].astype(I32)
    rank = route[:, 4:4 + TOP_K].astype(I32)
    ntiles = (TOP_K * t + experts * (MOE_ROW_TILE - 1)) // MOE_ROW_TILE + 1
    nfull_max = max(TOP_K * t // MOE_ROW_TILE, 1)
    off, frow, fexp, nfull, trow, tsub = plan(counts, ntiles, nfull_max)
    xs = dispatch(h, off, eidx, rank, ntiles * MOE_ROW_TILE, tm)
    ys = expert_ffn(xs, nfull, frow, fexp, trow, tsub, wg, wu, wd)
    return combine(x, route, ys, off, eidx, rank, final_g, tm)


def _rope_tables(pos, half):
    inv = jnp.power(ROPE_THETA, -jnp.arange(half, dtype=F32) / half)
    ang = pos[:, None] * inv[None, :]
    return jnp.cos(ang), jnp.sin(ang)


def kernel(x_prompt, x_sample, state_retention, state_hgrn, norm_mix_g, norm_ffn_g, final_norm_g, ret_w_in, ret_gn_g, ret_w_out, hg_w_in, hg_lb_param, hg_norm_g, hg_w_out, ffn_w_gate, ffn_w_up, ffn_w_down, moe_router, moe_w_gate, moe_w_up, moe_w_down):
    bp, lp, d = x_prompt.shape
    bs, ls, _ = x_sample.shape
    assert ls == 1 and norm_mix_g.shape[0] == 2
    _, _, ret_heads, ret_dk, ret_dv = state_retention.shape
    _, _, hg_heads, hg_dk, _ = state_hgrn.shape
    tp = bp * lp
    t = tp + bs
    tm = _pick_tile(t, 768, 16)

    x = jnp.concatenate([x_prompt.reshape(tp, d), x_sample.reshape(bs, d)], axis=0)

    log_gamma = jnp.log1p(-jnp.exp2(-5.0 - jnp.arange(ret_heads, dtype=F32)))
    cos_p, sin_p = _rope_tables(jnp.arange(lp, dtype=F32), ret_dk // 2)
    cos_s, sin_s = _rope_tables(PAST_LEN + jnp.arange(ls, dtype=F32), ret_dk // 2)
    tm_proj = _pick_tile(t, 1536, 16)
    proj = norm_matmul(x, norm_mix_g[0], ret_w_in[0].astype(BF16), tm_proj,
                       _pick_tile(ret_w_in.shape[2], 1024, LANES))
    gated, ret_p = retention_scan(proj, log_gamma, cos_p, sin_p, ret_gn_g[0], batch=bp, seq=lp,
                                  heads=ret_heads, dk=ret_dk, dv=ret_dv, rows_total=t)
    gated, ret_s = retention_step(proj, gated, state_retention[0], log_gamma, cos_s, sin_s, ret_gn_g[0],
                                  row0=tp, nb=bs, heads=ret_heads, dk=ret_dk, dv=ret_dv)
    x = matmul_residual(gated, ret_w_out[0].astype(BF16), x, tm)
    ff = ffn_w_gate.shape[2]
    x = ffn_residual(x, norm_ffn_g[0], ffn_w_gate[0].astype(BF16), ffn_w_up[0].astype(BF16),
                     ffn_w_down[0].astype(BF16), tm, _pick_tile(ff, 1536, LANES))

    proj = norm_matmul(x, norm_mix_g[1], hg_w_in[0].astype(BF16), tm_proj,
                       _pick_tile(hg_w_in.shape[2], 1024, LANES))
    gated, hg_p = hgrn_scan(proj, hg_lb_param, hg_norm_g[0], batch=bp, seq=lp, heads=hg_heads, dk=hg_dk,
                            rows_total=t, layer=1)
    gated, hg_s = hgrn_step(proj, gated, state_hgrn[0], hg_lb_param, hg_norm_g[0], row0=tp, nb=bs,
                            heads=hg_heads, dk=hg_dk, layer=1)
    x = matmul_residual(gated, hg_w_out[0].astype(BF16), x, tm)
    y = moe_residual_final_norm(x, norm_ffn_g[1], moe_router[0], moe_w_gate[0], moe_w_up[0], moe_w_down[0],
                                final_norm_g, tm)

    return (y[:tp].reshape(bp, lp, d), y[tp:].reshape(bs, ls, d),
            ret_p[None], ret_s[None], hg_p[None], hg_s[None])
```

```python
import functools

import jax
import jax.numpy as jnp
from jax import lax
from jax.experimental import pallas as pl
from jax.experimental.pallas import tpu as pltpu

F32 = jnp.float32
BF16 = jnp.bfloat16
I32 = jnp.int32

NORM_EPS = 1e-6
GN_EPS = 1e-5
ROPE_THETA = 10000.0
PAST_LEN = 16384
TOP_K = 2

LANES = 128
VMEM_LIMIT = 56 * 1024 * 1024

RET_CHUNK = 256
HG_CHUNK = 128
HG_DIAG = 8
RET_STEP_SEQS = 2
HG_STEP_SEQS = 4
MOE_ROW_TILE = 1024
MOE_SUB = 256
MOE_F_TILE = 512
DMA_UNROLL = 8

NT_DIMS = (((1,), (1,)), ((), ()))
TN_DIMS = (((0,), (0,)), ((), ()))


def _cparams(*sem):
    return pltpu.CompilerParams(dimension_semantics=sem, vmem_limit_bytes=VMEM_LIMIT)


def _pick_tile(n, target, mult):
    best = None
    for t in range(mult, min(n, target) + 1, mult):
        if n % t == 0:
            best = t
    assert best is not None, (n, target, mult)
    return best


def _sigmoid(x):
    return 1.0 / (1.0 + jnp.exp(-x))


def _silu(x):
    return x * _sigmoid(x)


def _rms(x, g):
    return x * lax.rsqrt(jnp.mean(x * x, axis=-1, keepdims=True) + NORM_EPS) * g


def _row_group(b):
    return pl.multiple_of(lax.shift_left(lax.shift_right_logical(b, 3), 3), 8), b & 7


def _load_row(ref, b, cols):
    base, r = _row_group(b)
    blk = ref[pl.ds(base, 8), cols]
    rows = lax.broadcasted_iota(I32, blk.shape, 0)
    return jnp.sum(jnp.where(rows == r, blk, 0.0), axis=0, keepdims=True)


def _store_row(ref, b, cols, row):
    base, r = _row_group(b)
    blk = ref[pl.ds(base, 8), cols]
    rows = lax.broadcasted_iota(I32, blk.shape, 0)
    ref[pl.ds(base, 8), cols] = jnp.where(rows == r, row, blk)


def _norm_matmul_kernel(x_ref, g_ref, w_ref, o_ref, h_ref):
    @pl.when(pl.program_id(1) == 0)
    def _():
        h_ref[...] = _rms(x_ref[...], g_ref[...]).astype(BF16)

    o_ref[...] = jnp.dot(h_ref[...], w_ref[...], preferred_element_type=F32).astype(o_ref.dtype)


def norm_matmul(x, g, w, tm, tn):
    t, d = x.shape
    n = w.shape[1]
    return pl.pallas_call(
        _norm_matmul_kernel,
        grid=(t // tm, n // tn),
        in_specs=[pl.BlockSpec((tm, d), lambda i, j: (i, 0)),
                  pl.BlockSpec((1, d), lambda i, j: (0, 0)),
                  pl.BlockSpec((d, tn), lambda i, j: (0, j))],
        out_specs=pl.BlockSpec((tm, tn), lambda i, j: (i, j)),
        out_shape=jax.ShapeDtypeStruct((t, n), BF16),
        scratch_shapes=[pltpu.VMEM((tm, d), BF16)],
        compiler_params=_cparams("parallel", "arbitrary"),
        name="norm_matmul",
    )(x, g.reshape(1, d), w)


def _with_merged_rows(xp_ref, xs_ref, fn):
    i = pl.program_id(0)
    last = pl.num_programs(0) - 1
    n_prompt = xp_ref.shape[0] - xs_ref.shape[0]

    @pl.when(i < last)
    def _():
        fn(xp_ref[...])

    @pl.when(i == last)
    def _():
        fn(jnp.concatenate([xp_ref[:n_prompt], xs_ref[...]], axis=0))


def _merged_specs(tm, d, bs, nidx):
    if nidx == 1:
        return [pl.BlockSpec((tm, d), lambda i: (i, 0)), pl.BlockSpec((bs, d), lambda i: (0, 0))]
    return [pl.BlockSpec((tm, d), lambda i, j: (i, 0)), pl.BlockSpec((bs, d), lambda i, j: (0, 0))]


def _norm_matmul2_kernel(xp_ref, xs_ref, g_ref, w_ref, o_ref, h_ref):
    @pl.when(pl.program_id(1) == 0)
    def _():
        def fill(x):
            h_ref[...] = _rms(x, g_ref[...]).astype(BF16)

        _with_merged_rows(xp_ref, xs_ref, fill)

    o_ref[...] = jnp.dot(h_ref[...], w_ref[...], preferred_element_type=F32).astype(o_ref.dtype)


def norm_matmul_merged(xp, xs, g, w, tm, tn):
    tp, d = xp.shape
    bs = xs.shape[0]
    t = tp + bs
    n = w.shape[1]
    assert t % tm == 0 and bs <= tm
    return pl.pallas_call(
        _norm_matmul2_kernel,
        grid=(t // tm, n // tn),
        in_specs=_merged_specs(tm, d, bs, 2) + [pl.BlockSpec((1, d), lambda i, j: (0, 0)),
                                                 pl.BlockSpec((d, tn), lambda i, j: (0, j))],
        out_specs=pl.BlockSpec((tm, tn), lambda i, j: (i, j)),
        out_shape=jax.ShapeDtypeStruct((t, n), BF16),
        scratch_shapes=[pltpu.VMEM((tm, d), BF16)],
        compiler_params=_cparams("parallel", "arbitrary"),
        name="norm_matmul_merged",
    )(xp, xs, g.reshape(1, d), w)


def _matmul_res_kernel(a_ref, w_ref, x_ref, o_ref):
    o_ref[...] = x_ref[...] + jnp.dot(a_ref[...], w_ref[...], preferred_element_type=F32)


def matmul_residual(a, w, x, tm):
    t, k = a.shape
    n = w.shape[1]
    return pl.pallas_call(
        _matmul_res_kernel,
        grid=(t // tm,),
        in_specs=[pl.BlockSpec((tm, k), lambda i: (i, 0)),
                  pl.BlockSpec((k, n), lambda i: (0, 0)),
                  pl.BlockSpec((tm, n), lambda i: (i, 0))],
        out_specs=pl.BlockSpec((tm, n), lambda i: (i, 0)),
        out_shape=jax.ShapeDtypeStruct((t, n), F32),
        compiler_params=_cparams("parallel"),
        name="matmul_residual",
    )(a, w, x)


def _matmul_res2_kernel(a_ref, w_ref, xp_ref, xs_ref, o_ref):
    def fill(x):
        o_ref[...] = x + jnp.dot(a_ref[...], w_ref[...], preferred_element_type=F32)

    _with_merged_rows(xp_ref, xs_ref, fill)


def matmul_residual_merged(a, w, xp, xs, tm):
    t, k = a.shape
    n = w.shape[1]
    bs = xs.shape[0]
    assert t % tm == 0 and bs <= tm and xp.shape[0] + bs == t
    return pl.pallas_call(
        _matmul_res2_kernel,
        grid=(t // tm,),
        in_specs=[pl.BlockSpec((tm, k), lambda i: (i, 0)),
                  pl.BlockSpec((k, n), lambda i: (0, 0))] + _merged_specs(tm, n, bs, 1),
        out_specs=pl.BlockSpec((tm, n), lambda i: (i, 0)),
        out_shape=jax.ShapeDtypeStruct((t, n), F32),
        compiler_params=_cparams("parallel"),
        name="matmul_residual_merged",
    )(a, w, xp, xs)


def _ffn_kernel(x_ref, g_ref, wg_ref, wu_ref, wd_ref, o_ref, h_ref):
    f = pl.program_id(1)

    @pl.when(f == 0)
    def _():
        x = x_ref[...]
        h_ref[...] = _rms(x, g_ref[...]).astype(BF16)
        o_ref[...] = x

    h = h_ref[...]
    a = jnp.dot(h, wg_ref[...], preferred_element_type=F32)
    u = jnp.dot(h, wu_ref[...], preferred_element_type=F32)
    act = (_silu(a) * u).astype(BF16)
    o_ref[...] += jnp.dot(act, wd_ref[...], preferred_element_type=F32)


def ffn_residual(x, g, wg, wu, wd, tm, tf):
    t, d = x.shape
    f = wg.shape[1]
    return pl.pallas_call(
        _ffn_kernel,
        grid=(t // tm, f // tf),
        in_specs=[pl.BlockSpec((tm, d), lambda i, j: (i, 0)),
                  pl.BlockSpec((1, d), lambda i, j: (0, 0)),
                  pl.BlockSpec((d, tf), lambda i, j: (0, j)),
                  pl.BlockSpec((d, tf), lambda i, j: (0, j)),
                  pl.BlockSpec((tf, d), lambda i, j: (j, 0))],
        out_specs=pl.BlockSpec((tm, d), lambda i, j: (i, 0)),
        out_shape=jax.ShapeDtypeStruct((t, d), F32),
        scratch_shapes=[pltpu.VMEM((tm, d), BF16)],
        compiler_params=_cparams("parallel", "arbitrary"),
        name="ffn_residual",
    )(x, g.reshape(1, d), wg, wu, wd)


def _rotary(x, cos, sin, half):
    x1 = x[:, :half]
    x2 = x[:, half:]
    return jnp.concatenate([x1 * cos - x2 * sin, x2 * cos + x1 * sin], axis=-1)


def _group_norm_gate(o, gate, gain):
    mu = jnp.mean(o, axis=-1, keepdims=True)
    d = o - mu
    var = jnp.mean(d * d, axis=-1, keepdims=True)
    return _silu(gate) * (d * lax.rsqrt(var + GN_EPS) * gain)


def _ret_scan_kernel(lg_ref, q_ref, k_ref, v_ref, g_ref, cos_ref, sin_ref, gn_ref,
                     o_ref, sfin_ref, s_ref, mask_ref, din_ref, dout_ref, *, chunk, dk, dv):
    c = pl.program_id(2)
    lg = lg_ref[pl.program_id(1)]

    @pl.when(c == 0)
    def _():
        s_ref[...] = jnp.zeros_like(s_ref)
        t = lax.broadcasted_iota(I32, (chunk, LANES), 0).astype(F32)
        din_ref[...] = jnp.exp((t + 1.0) * lg)
        dout_ref[...] = jnp.exp((chunk - 1.0 - t) * lg)
        ti = lax.broadcasted_iota(I32, (chunk, chunk), 0)
        si = lax.broadcasted_iota(I32, (chunk, chunk), 1)
        causal = ti >= si
        rel = (ti - si).astype(F32)
        mask_ref[...] = jnp.where(causal, jnp.exp(jnp.where(causal, rel, 0.0) * lg), 0.0)

    half = dk // 2
    cos = cos_ref[...]
    sin = sin_ref[...]
    q = _rotary(q_ref[...].astype(F32), cos, sin, half)
    k = _rotary(k_ref[...].astype(F32), cos, sin, half) * (dk ** -0.5)
    v = v_ref[...]
    decay_in = _lane_tile(din_ref[...], dk)
    decay_out = _lane_tile(dout_ref[...], dk)
    decay_chunk = jnp.exp(jnp.zeros((1, dv), F32) + chunk * lg)

    s = s_ref[...]
    scores = (lax.dot_general(q.astype(BF16), k.astype(BF16), NT_DIMS, preferred_element_type=F32)
              * mask_ref[...])
    o = (jnp.dot(scores.astype(BF16), v, preferred_element_type=F32)
         + jnp.dot((q * decay_in).astype(BF16), s.astype(BF16), preferred_element_type=F32))
    kd_t = (k * decay_out).T.astype(BF16)
    s_new = s * decay_chunk + jnp.dot(kd_t, v, preferred_element_type=F32)
    s_ref[...] = s_new

    o_ref[...] = _group_norm_gate(o, g_ref[...].astype(F32), gn_ref[...]).astype(BF16)

    @pl.when(c == pl.num_programs(2) - 1)
    def _():
        sfin_ref[0, 0] = s_new


def retention_scan(proj, log_gamma, cos, sin, gn_g, *, batch, seq, heads, dk, dv, rows_total):
    chunk = min(RET_CHUNK, seq)
    nc = seq // chunk
    kern = functools.partial(_ret_scan_kernel, chunk=chunk, dk=dk, dv=dv)
    qk_blocks = heads * dk // dv
    return pl.pallas_call(
        kern,
        grid=(batch, heads, nc),
        in_specs=[pl.BlockSpec(memory_space=pltpu.SMEM),
                  pl.BlockSpec((chunk, dk), lambda b, h, c: (b * nc + c, h)),
                  pl.BlockSpec((chunk, dk), lambda b, h, c: (b * nc + c, heads + h)),
                  pl.BlockSpec((chunk, dv), lambda b, h, c: (b * nc + c, 2 * qk_blocks + h)),
                  pl.BlockSpec((chunk, dv), lambda b, h, c: (b * nc + c, 2 * qk_blocks + heads + h)),
                  pl.BlockSpec((chunk, dk // 2), lambda b, h, c: (c, 0)),
                  pl.BlockSpec((chunk, dk // 2), lambda b, h, c: (c, 0)),
                  pl.BlockSpec((1, dv), lambda b, h, c: (0, h))],
        out_specs=[pl.BlockSpec((chunk, dv), lambda b, h, c: (b * nc + c, h)),
                   pl.BlockSpec((1, 1, dk, dv), lambda b, h, c: (b, h, 0, 0))],
        out_shape=[jax.ShapeDtypeStruct((rows_total, heads * dv), BF16),
                   jax.ShapeDtypeStruct((batch, heads, dk, dv), F32)],
        scratch_shapes=[pltpu.VMEM((dk, dv), F32),
                        pltpu.VMEM((chunk, chunk), F32),
                        pltpu.VMEM((chunk, LANES), F32),
                        pltpu.VMEM((chunk, LANES), F32)],
        compiler_params=_cparams("parallel", "parallel", "arbitrary"),
        name="retention_scan",
    )(log_gamma, proj, proj, proj, proj, cos, sin, gn_g.reshape(1, heads * dv))


def _one_hot_rows(nb, b):
    return jnp.where(lax.broadcasted_iota(I32, (nb, LANES), 0) == b, 1.0, 0.0).astype(BF16)


def _lane_tile(x, width):
    return jnp.concatenate([x] * (width // x.shape[1]), axis=1)


def _ret_step_kernel(lg_ref, q_ref, k_ref, v_ref, g_ref, cos_ref, sin_ref, gn_ref, s_ref, prev_ref,
                     o_ref, snew_ref, qt_ref, kt_ref, vf_ref, gf_ref, orow_ref, *, heads, dk, dv, nb, per_step):
    del prev_ref
    step = pl.program_id(0)
    half = dk // 2

    @pl.when(step == 0)
    def _():
        orow_ref[...] = jnp.zeros_like(orow_ref)
        vf_ref[...] = v_ref[...].astype(F32)
        gf_ref[...] = g_ref[...].astype(F32)
        cos = cos_ref[...]
        sin = sin_ref[...]
        for h in range(heads):
            qh = _rotary(q_ref[:, h * dk:(h + 1) * dk].astype(F32), cos, sin, half)
            kh = _rotary(k_ref[:, h * dk:(h + 1) * dk].astype(F32), cos, sin, half) * (dk ** -0.5)
            qt_ref[h] = qh.T.astype(BF16)
            kt_ref[h] = kh.T.astype(BF16)

    for i in range(per_step):
        b = step * per_step + i
        onehot = _one_hot_rows(nb, b)
        for h in range(heads):
            cols = slice(h * dv, (h + 1) * dv)
            kcol = _lane_tile(jnp.dot(kt_ref[h], onehot, preferred_element_type=F32), dv)
            qcol = _lane_tile(jnp.dot(qt_ref[h], onehot, preferred_element_type=F32), dv)
            vrow = _load_row(vf_ref, b, cols)
            gamma = jnp.exp(jnp.zeros((1, dv), F32) + lg_ref[h])
            s_new = s_ref[i, h] * gamma + kcol * vrow
            snew_ref[i, h] = s_new
            o = jnp.sum(s_new * qcol, axis=0, keepdims=True)
            _store_row(orow_ref, b, cols, _group_norm_gate(o, _load_row(gf_ref, b, cols), gn_ref[:, cols]))

    @pl.when(step == pl.num_programs(0) - 1)
    def _():
        o_ref[...] = orow_ref[...].astype(BF16)


def retention_step(proj, gated, state, log_gamma, cos, sin, gn_g, *, row0, nb, heads, dk, dv):
    per_step = RET_STEP_SEQS
    kern = functools.partial(_ret_step_kernel, heads=heads, dk=dk, dv=dv, nb=nb, per_step=per_step)
    rb = row0 // nb
    wq = heads * dk
    wv = heads * dv
    return pl.pallas_call(
        kern,
        grid=(nb // per_step,),
        in_specs=[pl.BlockSpec(memory_space=pltpu.SMEM),
                  pl.BlockSpec((nb, wq), lambda b: (rb, 0)),
                  pl.BlockSpec((nb, wq), lambda b: (rb, 1)),
                  pl.BlockSpec((nb, wv), lambda b: (rb, 2 * wq // wv)),
                  pl.BlockSpec((nb, wv), lambda b: (rb, 2 * wq // wv + 1)),
                  pl.BlockSpec((1, dk // 2), lambda b: (0, 0)),
                  pl.BlockSpec((1, dk // 2), lambda b: (0, 0)),
                  pl.BlockSpec((1, wv), lambda b: (0, 0)),
                  pl.BlockSpec((per_step, heads, dk, dv), lambda b: (b, 0, 0, 0)),
                  pl.BlockSpec(memory_space=pl.ANY)],
        out_specs=[pl.BlockSpec((nb, wv), lambda b: (rb, 0)),
                   pl.BlockSpec((per_step, heads, dk, dv), lambda b: (b, 0, 0, 0))],
        out_shape=[jax.ShapeDtypeStruct(gated.shape, gated.dtype),
                   jax.ShapeDtypeStruct(state.shape, F32)],
        scratch_shapes=[pltpu.VMEM((heads, dk, nb), BF16),
                        pltpu.VMEM((heads, dk, nb), BF16),
                        pltpu.VMEM((nb, wv), F32),
                        pltpu.VMEM((nb, wv), F32),
                        pltpu.VMEM((nb, wv), F32)],
        input_output_aliases={9: 0},
        compiler_params=_cparams("arbitrary"),
        name="retention_step",
    )(log_gamma, proj, proj, proj, proj, cos, sin, gn_g.reshape(1, wv), state, gated)


def _lower_bound(lbp, layer):
    m = jnp.max(lbp, axis=0, keepdims=True)
    e = jnp.exp(lbp - m)
    p = e / jnp.sum(e, axis=0, keepdims=True)
    return jnp.sum(p[:layer + 1], axis=0, keepdims=True) - p[0:1]


def _split_dot(mat_bf16, x):
    hi = x.astype(BF16)
    r1 = x - hi.astype(F32)
    mid = r1.astype(BF16)
    lo = (r1 - mid.astype(F32)).astype(BF16)
    return (jnp.dot(mat_bf16, hi, preferred_element_type=F32)
            + jnp.dot(mat_bf16, mid, preferred_element_type=F32)
            + jnp.dot(mat_bf16, lo, preferred_element_type=F32))


def _rms_gate(o, gate, gain):
    return _silu(gate) * (o * lax.rsqrt(jnp.mean(o * o, axis=-1, keepdims=True) + NORM_EPS) * gain)


def _group_row(x, s, group):
    n, w = x.shape
    x3 = x.reshape(n // group, group, w)
    return jnp.broadcast_to(x3[:, s:s + 1, :], x3.shape).reshape(n, w)


def _hgrn_scan_kernel(lbp_ref, q_ref, f_ref, i_ref, g_ref, ng_ref, o_ref, sfin_ref, st_ref,
                      *, chunk, heads, dk, diag, layer):
    c = pl.program_id(1)

    @pl.when(c == 0)
    def _():
        st_ref[...] = jnp.zeros_like(st_ref)

    w = heads * dk
    lb = _lower_bound(lbp_ref[...], layer)
    qs = _silu(q_ref[...].astype(F32))
    forget = lb + (1.0 - lb) * _sigmoid(f_ref[...].astype(F32))
    kk = 1.0 - forget
    logf = jnp.log(forget)

    row = lax.broadcasted_iota(I32, (chunk, chunk), 0)
    col = lax.broadcasted_iota(I32, (chunk, chunk), 1)
    tril = jnp.where(row >= col, 1.0, 0.0).astype(BF16)
    bcum = _split_dot(tril, logf)
    blast = bcum[chunk - 1:chunk, :]
    qe = (qs * jnp.exp(bcum)).astype(BF16)
    kdec = (kk * jnp.exp(blast - bcum)).astype(BF16)
    iv = i_ref[...]
    gate = g_ref[...].astype(F32)
    gain = ng_ref[...]

    u = jnp.where(row > col, row ^ col, 0)
    rowv = lax.broadcasted_iota(I32, (chunk, 1), 0)
    levels = []
    m = chunk // 2
    while m >= diag:
        span = 2 * m
        bref = jnp.concatenate(
            [jnp.broadcast_to(bcum[g * span + m - 1:g * span + m], (span, w)) for g in range(chunk // span)],
            axis=0)
        upper = (rowv & m) != 0
        d = bcum - bref
        x = (jnp.where(upper, qs, kk) * jnp.exp(jnp.where(upper, d, -d))).astype(BF16)
        levels.append((x, lax.shift_right_logical(u, m.bit_length() - 1) == 1))
        m //= 2
    in_block = row & (diag - 1)
    block_col0 = row - in_block
    diag_masks = [(col == block_col0 + s) & (in_block >= s) for s in range(diag)]
    ones = jnp.ones((dk, LANES), BF16)

    for h in range(heads):
        cols = slice(h * dk, (h + 1) * dk)
        qs_h = qs[:, cols]
        kk_h = kk[:, cols]
        b_h = bcum[:, cols]
        a = jnp.zeros((chunk, chunk), F32)
        for x, mask in levels:
            x_h = x[:, cols]
            a = jnp.where(mask, lax.dot_general(x_h, x_h, NT_DIMS, preferred_element_type=F32), a)
        ws = [qs_h * _group_row(kk_h, s, diag) * jnp.exp(b_h - _group_row(b_h, s, diag)) for s in range(diag)]
        rsum = jnp.dot(jnp.concatenate(ws, axis=0).astype(BF16), ones, preferred_element_type=F32)
        for s in range(diag):
            a = jnp.where(diag_masks[s], rsum[s * chunk:(s + 1) * chunk], a)
        st = st_ref[h]
        i_h = iv[:, cols]
        o_h = (jnp.dot(a.astype(BF16), i_h, preferred_element_type=F32)
               + lax.dot_general(qe[:, cols], st.astype(BF16), NT_DIMS, preferred_element_type=F32))
        st_new = (st * jnp.exp(blast[:, cols])
                  + lax.dot_general(i_h, kdec[:, cols], TN_DIMS, preferred_element_type=F32))
        st_ref[h] = st_new
        o_ref[:, cols] = _rms_gate(o_h, gate[:, cols], gain[:, cols]).astype(BF16)

    @pl.when(c == pl.num_programs(1) - 1)
    def _():
        for h in range(heads):
            sfin_ref[0, h] = st_ref[h].T


def hgrn_scan(proj, lb_param, norm_g, *, batch, seq, heads, dk, rows_total, layer):
    chunk = min(HG_CHUNK, seq)
    assert chunk == LANES and dk == LANES
    nc = seq // chunk
    w = heads * dk
    kern = functools.partial(_hgrn_scan_kernel, chunk=chunk, heads=heads, dk=dk, diag=HG_DIAG, layer=layer)
    return pl.pallas_call(
        kern,
        grid=(batch, nc),
        in_specs=[pl.BlockSpec(lb_param.shape, lambda b, c: (0, 0)),
                  pl.BlockSpec((chunk, w), lambda b, c: (b * nc + c, 0)),
                  pl.BlockSpec((chunk, w), lambda b, c: (b * nc + c, 1)),
                  pl.BlockSpec((chunk, w), lambda b, c: (b * nc + c, 2)),
                  pl.BlockSpec((chunk, w), lambda b, c: (b * nc + c, 3)),
                  pl.BlockSpec((1, w), lambda b, c: (0, 0))],
        out_specs=[pl.BlockSpec((chunk, w), lambda b, c: (b * nc + c, 0)),
                   pl.BlockSpec((1, heads, dk, dk), lambda b, c: (b, 0, 0, 0))],
        out_shape=[jax.ShapeDtypeStruct((rows_total, w), BF16),
                   jax.ShapeDtypeStruct((batch, heads, dk, dk), F32)],
        scratch_shapes=[pltpu.VMEM((heads, dk, dk), F32)],
        compiler_params=_cparams("parallel", "arbitrary"),
        name="hgrn_scan",
    )(lb_param, proj, proj, proj, proj, norm_g.reshape(1, w))


def _hgrn_step_kernel(lbp_ref, q_ref, f_ref, i_ref, g_ref, ng_ref, s_ref, prev_ref,
                      o_ref, snew_ref, qt_ref, ft_ref, if_ref, gf_ref, orow_ref,
                      *, heads, dk, nb, layer, per_step):
    del prev_ref
    step = pl.program_id(0)

    @pl.when(step == 0)
    def _():
        orow_ref[...] = jnp.zeros_like(orow_ref)
        if_ref[...] = i_ref[...].astype(F32)
        gf_ref[...] = g_ref[...].astype(F32)
        lb = _lower_bound(lbp_ref[...], layer)
        qs = _silu(q_ref[...].astype(F32))
        forget = lb + (1.0 - lb) * _sigmoid(f_ref[...].astype(F32))
        for h in range(heads):
            cols = slice(h * dk, (h + 1) * dk)
            qt_ref[h] = qs[:, cols].T.astype(BF16)
            ft = forget[:, cols].T
            hi = ft.astype(BF16)
            r1 = ft - hi.astype(F32)
            mid = r1.astype(BF16)
            ft_ref[0, h] = hi
            ft_ref[1, h] = mid
            ft_ref[2, h] = (r1 - mid.astype(F32)).astype(BF16)

    for i in range(per_step):
        b = step * per_step + i
        onehot = _one_hot_rows(nb, b)
        for h in range(heads):
            cols = slice(h * dk, (h + 1) * dk)
            fcol = (jnp.dot(ft_ref[0, h], onehot, preferred_element_type=F32)
                    + jnp.dot(ft_ref[1, h], onehot, preferred_element_type=F32)
                    + jnp.dot(ft_ref[2, h], onehot, preferred_element_type=F32))
            qcol = jnp.dot(qt_ref[h], onehot, preferred_element_type=F32)
            irow = _load_row(if_ref, b, cols)
            s_new = s_ref[i, h] * fcol + (1.0 - fcol) * irow
            snew_ref[i, h] = s_new
            o = jnp.sum(s_new * qcol, axis=0, keepdims=True)
            _store_row(orow_ref, b, cols, _rms_gate(o, _load_row(gf_ref, b, cols), ng_ref[:, cols]))

    @pl.when(step == pl.num_programs(0) - 1)
    def _():
        o_ref[...] = orow_ref[...].astype(BF16)


def hgrn_step(proj, gated, state, lb_param, norm_g, *, row0, nb, heads, dk, layer):
    per_step = HG_STEP_SEQS
    kern = functools.partial(_hgrn_step_kernel, heads=heads, dk=dk, nb=nb, layer=layer, per_step=per_step)
    rb = row0 // nb
    w = heads * dk
    return pl.pallas_call(
        kern,
        grid=(nb // per_step,),
        in_specs=[pl.BlockSpec(lb_param.shape, lambda b: (0, 0)),
                  pl.BlockSpec((nb, w), lambda b: (rb, 0)),
                  pl.BlockSpec((nb, w), lambda b: (rb, 1)),
                  pl.BlockSpec((nb, w), lambda b: (rb, 2)),
                  pl.BlockSpec((nb, w), lambda b: (rb, 3)),
                  pl.BlockSpec((1, w), lambda b: (0, 0)),
                  pl.BlockSpec((per_step, heads, dk, dk), lambda b: (b, 0, 0, 0)),
                  pl.BlockSpec(memory_space=pl.ANY)],
        out_specs=[pl.BlockSpec((nb, w), lambda b: (rb, 0)),
                   pl.BlockSpec((per_step, heads, dk, dk), lambda b: (b, 0, 0, 0))],
        out_shape=[jax.ShapeDtypeStruct(gated.shape, gated.dtype),
                   jax.ShapeDtypeStruct(state.shape, F32)],
        scratch_shapes=[pltpu.VMEM((heads, dk, nb), BF16),
                        pltpu.VMEM((3, heads, dk, nb), BF16),
                        pltpu.VMEM((nb, w), F32),
                        pltpu.VMEM((nb, w), F32),
                        pltpu.VMEM((nb, w), F32)],
        input_output_aliases={7: 0},
        compiler_params=_cparams("arbitrary"),
        name="hgrn_step",
    )(lb_param, proj, proj, proj, proj, norm_g.reshape(1, w), state, gated)


def _router_kernel(x_ref, g_ref, r_ref, h_ref, route_ref, cnt_ref, carry_ref, *, tm, experts):
    i = pl.program_id(0)

    @pl.when(i == 0)
    def _():
        carry_ref[...] = jnp.zeros_like(carry_ref)

    h = _rms(x_ref[...], g_ref[...])
    h_ref[...] = h
    logits = jnp.dot(h, r_ref[...], preferred_element_type=F32, precision=lax.Precision.HIGHEST)
    lane = lax.broadcasted_iota(I32, (tm, LANES), 1)
    valid = lane < experts
    z = jnp.where(valid, logits, -jnp.inf)
    ez = jnp.exp(z - jnp.max(z, axis=-1, keepdims=True))
    p = ez / jnp.sum(ez, axis=-1, keepdims=True)
    p = jnp.where(valid, p, -1.0)
    lane_f = lane.astype(F32)
    v1 = jnp.max(p, axis=-1, keepdims=True)
    i1 = jnp.min(jnp.where(p == v1, lane_f, float(LANES)), axis=-1, keepdims=True)
    p2 = jnp.where(lane_f == i1, -1.0, p)
    v2 = jnp.max(p2, axis=-1, keepdims=True)
    i2 = jnp.min(jnp.where(p2 == v2, lane_f, float(LANES)), axis=-1, keepdims=True)
    den = v1 + v2
    m0 = jnp.where(lane_f == i1, 1.0, 0.0)
    m1 = jnp.where(lane_f == i2, 1.0, 0.0)
    msum = m0 + m1
    ri = lax.broadcasted_iota(I32, (tm, tm), 0)
    ci = lax.broadcasted_iota(I32, (tm, tm), 1)
    strict = jnp.where(ri > ci, 1.0, 0.0).astype(BF16)
    before = jnp.dot(strict, msum.astype(BF16), preferred_element_type=F32) + carry_ref[...]
    rank0 = jnp.sum(m0 * before, axis=-1, keepdims=True)
    rank1 = jnp.sum(m1 * before, axis=-1, keepdims=True)
    total = carry_ref[...] + jnp.sum(msum, axis=0, keepdims=True)
    carry_ref[...] = total
    cnt_ref[...] = jnp.broadcast_to(total, cnt_ref.shape)
    route = jnp.where(lane == 0, i1,
            jnp.where(lane == 1, i2,
            jnp.where(lane == 2, v1 / den,
            jnp.where(lane == 3, v2 / den,
            jnp.where(lane == 4, rank0,
            jnp.where(lane == 5, rank1, 0.0))))))
    route_ref[...] = route


def router(x, g, router_w, tm):
    t, d = x.shape
    experts = router_w.shape[1]
    rpad = jnp.zeros((d, LANES), F32).at[:, :experts].set(router_w)
    kern = functools.partial(_router_kernel, tm=tm, experts=experts)
    return pl.pallas_call(
        kern,
        grid=(t // tm,),
        in_specs=[pl.BlockSpec((tm, d), lambda i: (i, 0)),
                  pl.BlockSpec((1, d), lambda i: (0, 0)),
                  pl.BlockSpec((d, LANES), lambda i: (0, 0))],
        out_specs=[pl.BlockSpec((tm, d), lambda i: (i, 0)),
                   pl.BlockSpec((tm, LANES), lambda i: (i, 0)),
                   pl.BlockSpec((8, LANES), lambda i: (0, 0))],
        out_shape=[jax.ShapeDtypeStruct((t, d), F32),
                   jax.ShapeDtypeStruct((t, LANES), F32),
                   jax.ShapeDtypeStruct((8, LANES), F32)],
        scratch_shapes=[pltpu.VMEM((1, LANES), F32)],
        compiler_params=_cparams("arbitrary"),
        name="router",
    )(x, g.reshape(1, d), rpad)


def _plan_kernel(cnt_ref, route_ref, frow_ref, fexp_ref, nfull_ref, trow_ref, tsub_ref, pos_ref,
                 *, experts, nfull_max, dummy, row_tile, sub):
    tile0 = jnp.int32(0)
    nfull = jnp.int32(0)
    last = jnp.int32(0)
    route = route_ref[...]
    slot_expert = pltpu.roll(route, 4, axis=1)
    first_row = jnp.zeros_like(route)
    for e in range(experts):
        n = cnt_ref[e]
        first_row = jnp.where(slot_expert == float(e), (tile0 * row_tile).astype(F32), first_row)
        full = n // row_tile
        rem = n - full * row_tile

        def fill(j, carry, e=e, tile0=tile0, nfull=nfull):
            frow_ref[nfull + j] = tile0 + j
            fexp_ref[nfull + j] = e
            return carry

        lax.fori_loop(0, full, fill, 0)
        trow_ref[e] = jnp.where(rem > 0, tile0 + full, dummy)
        tsub_ref[e] = (rem + (sub - 1)) // sub
        last = jnp.where(full > 0, e, last)
        nfull = nfull + full
        tile0 = tile0 + full + jnp.where(rem > 0, 1, 0)
    nfull_ref[0] = nfull
    pos_ref[...] = (first_row + route).astype(I32)

    def unused(j, carry):
        frow_ref[j] = dummy
        fexp_ref[j] = last
        return carry

    lax.fori_loop(nfull, nfull_max, unused, 0)


def plan(counts, route, ntiles, nfull_max):
    experts = counts.shape[0]
    kern = functools.partial(_plan_kernel, experts=experts, nfull_max=nfull_max, dummy=ntiles - 1,
                             row_tile=MOE_ROW_TILE, sub=MOE_SUB)
    smem = pl.BlockSpec(memory_space=pltpu.SMEM)
    vmem = pl.BlockSpec(memory_space=pltpu.VMEM)
    return pl.pallas_call(
        kern,
        in_specs=[smem, vmem],
        out_specs=[smem] * 5 + [vmem],
        out_shape=[jax.ShapeDtypeStruct((nfull_max,), I32),
                   jax.ShapeDtypeStruct((nfull_max,), I32),
                   jax.ShapeDtypeStruct((1,), I32),
                   jax.ShapeDtypeStruct((experts,), I32),
                   jax.ShapeDtypeStruct((experts,), I32),
                   jax.ShapeDtypeStruct(route.shape, I32)],
        compiler_params=pltpu.CompilerParams(vmem_limit_bytes=VMEM_LIMIT),
        name="moe_plan",
    )(counts, route)


def _dispatch_kernel(p_ref, h_ref, init_hbm, out_hbm, sem, *, tm):
    del init_hbm

    def row_copy(t, dst):
        return pltpu.make_async_copy(h_ref.at[pl.ds(t, 1)], out_hbm.at[pl.ds(dst, 1)], sem)

    def issue(t, carry):
        for k in range(TOP_K):
            row_copy(t, p_ref[0, 0, TOP_K * t + k]).start()
        return carry

    lax.fori_loop(0, tm, issue, 0, unroll=DMA_UNROLL)

    def drain(t, carry):
        for k in range(TOP_K):
            row_copy(0, 0).wait()
        return carry

    lax.fori_loop(0, tm, drain, 0, unroll=DMA_UNROLL)


def dispatch(h, pos, rows, tm):
    t, d = h.shape
    nt = t // tm
    kern = functools.partial(_dispatch_kernel, tm=tm)
    return pl.pallas_call(
        kern,
        grid=(nt,),
        in_specs=[pl.BlockSpec((1, 1, TOP_K * tm), lambda i: (i, 0, 0), memory_space=pltpu.SMEM),
                  pl.BlockSpec((tm, d), lambda i: (i, 0)),
                  pl.BlockSpec(memory_space=pl.ANY)],
        out_specs=pl.BlockSpec(memory_space=pl.ANY),
        scratch_shapes=[pltpu.SemaphoreType.DMA(())],
        out_shape=jax.ShapeDtypeStruct((rows, d), h.dtype),
        input_output_aliases={2: 0},
        compiler_params=_cparams("arbitrary"),
        name="moe_dispatch",
    )(pos.reshape(nt, 1, TOP_K * tm), h, jnp.zeros((rows, d), h.dtype))


def _expert_full_kernel(nfull_ref, frow_ref, fexp_ref, x_ref, wg_ref, wu_ref, wd_ref, o_ref, h_ref):
    del frow_ref, fexp_ref
    j = pl.program_id(0)
    f = pl.program_id(1)
    active = j < nfull_ref[0]

    @pl.when(f == 0)
    def _():
        h_ref[...] = x_ref[...].astype(BF16)
        o_ref[...] = jnp.zeros_like(o_ref)

    @pl.when(active)
    def _():
        h = h_ref[...]
        a = jnp.dot(h, wg_ref[0].astype(BF16), preferred_element_type=F32)
        u = jnp.dot(h, wu_ref[0].astype(BF16), preferred_element_type=F32)
        act = (_silu(a) * u).astype(BF16)
        o_ref[...] += jnp.dot(act, wd_ref[0].astype(BF16), preferred_element_type=F32)


def _expert_tail_kernel(trow_ref, tsub_ref, x_ref, wg_ref, wu_ref, wd_ref, prev_ref, o_ref,
                        h_ref, wgb_ref, wub_ref, wdb_ref, *, sub, nsub_max):
    del trow_ref, prev_ref
    f = pl.program_id(1)
    nsub = tsub_ref[pl.program_id(0)]

    @pl.when(f == 0)
    def _():
        h_ref[...] = x_ref[...].astype(BF16)
        o_ref[...] = jnp.zeros_like(o_ref)

    @pl.when(nsub > 0)
    def _():
        wgb_ref[...] = wg_ref[0].astype(BF16)
        wub_ref[...] = wu_ref[0].astype(BF16)
        wdb_ref[...] = wd_ref[0].astype(BF16)
        for sb in range(nsub_max):
            @pl.when(sb < nsub)
            def _():
                rows = slice(sb * sub, (sb + 1) * sub)
                h = h_ref[rows]
                a = jnp.dot(h, wgb_ref[...], preferred_element_type=F32)
                u = jnp.dot(h, wub_ref[...], preferred_element_type=F32)
                act = (_silu(a) * u).astype(BF16)
                o_ref[rows] += jnp.dot(act, wdb_ref[...], preferred_element_type=F32)


def expert_ffn(xs, nfull, frow, fexp, trow, tsub, wg, wu, wd):
    rows, d = xs.shape
    experts, _, fdim = wg.shape
    tf = MOE_F_TILE
    nf = fdim // tf
    nfull_max = frow.shape[0]

    def fcol_full(j, f, nfull):
        return jnp.where(j < nfull[0], f, nf - 1)

    ys = pl.pallas_call(
        _expert_full_kernel,
        grid_spec=pltpu.PrefetchScalarGridSpec(
            num_scalar_prefetch=3,
            grid=(nfull_max, nf),
            in_specs=[pl.BlockSpec((MOE_ROW_TILE, d), lambda j, f, nfull, frow, fexp: (frow[j], 0)),
                      pl.BlockSpec((1, d, tf), lambda j, f, nfull, frow, fexp: (fexp[j], 0, fcol_full(j, f, nfull))),
                      pl.BlockSpec((1, d, tf), lambda j, f, nfull, frow, fexp: (fexp[j], 0, fcol_full(j, f, nfull))),
                      pl.BlockSpec((1, tf, d), lambda j, f, nfull, frow, fexp: (fexp[j], fcol_full(j, f, nfull), 0))],
            out_specs=pl.BlockSpec((MOE_ROW_TILE, d), lambda j, f, nfull, frow, fexp: (frow[j], 0)),
            scratch_shapes=[pltpu.VMEM((MOE_ROW_TILE, d), BF16)]),
        out_shape=jax.ShapeDtypeStruct((rows, d), F32),
        compiler_params=_cparams("arbitrary", "arbitrary"),
        name="expert_ffn_full",
    )(nfull, frow, fexp, xs, wg, wu, wd)

    def fcol_tail(e, f, tsub):
        return jnp.where(tsub[e] > 0, f, nf - 1)

    kern = functools.partial(_expert_tail_kernel, sub=MOE_SUB, nsub_max=MOE_ROW_TILE // MOE_SUB)
    return pl.pallas_call(
        kern,
        grid_spec=pltpu.PrefetchScalarGridSpec(
            num_scalar_prefetch=2,
            grid=(experts, nf),
            in_specs=[pl.BlockSpec((MOE_ROW_TILE, d), lambda e, f, trow, tsub: (trow[e], 0)),
                      pl.BlockSpec((1, d, tf), lambda e, f, trow, tsub: (e, 0, fcol_tail(e, f, tsub))),
                      pl.BlockSpec((1, d, tf), lambda e, f, trow, tsub: (e, 0, fcol_tail(e, f, tsub))),
                      pl.BlockSpec((1, tf, d), lambda e, f, trow, tsub: (e, fcol_tail(e, f, tsub), 0)),
                      pl.BlockSpec(memory_space=pl.ANY)],
            out_specs=pl.BlockSpec((MOE_ROW_TILE, d), lambda e, f, trow, tsub: (trow[e], 0)),
            scratch_shapes=[pltpu.VMEM((MOE_ROW_TILE, d), BF16),
                            pltpu.VMEM((d, tf), BF16),
                            pltpu.VMEM((d, tf), BF16),
                            pltpu.VMEM((tf, d), BF16)]),
        out_shape=jax.ShapeDtypeStruct((rows, d), F32),
        input_output_aliases={6: 0},
        compiler_params=_cparams("arbitrary", "arbitrary"),
        name="expert_ffn_tail",
    )(trow, tsub, xs, wg, wu, wd, ys)


def _combine_kernel(p_ref, x_ref, route_ref, fg_ref, ys_hbm, op_ref, os_ref, gath_ref, sem, *, tm):
    def row_copy(src, k, t):
        return pltpu.make_async_copy(ys_hbm.at[pl.ds(src, 1)], gath_ref.at[k, pl.ds(t, 1)], sem)

    def issue(t, carry):
        for k in range(TOP_K):
            row_copy(p_ref[0, 0, TOP_K * t + k], k, t).start()
        return carry

    lax.fori_loop(0, tm, issue, 0, unroll=DMA_UNROLL)

    def drain(t, carry):
        for k in range(TOP_K):
            row_copy(0, 0, 0).wait()
        return carry

    lax.fori_loop(0, tm, drain, 0, unroll=DMA_UNROLL)
    route = route_ref[...]
    x = x_ref[...] + route[:, 2:3] * gath_ref[0] + route[:, 3:4] * gath_ref[1]
    y = _rms(x, fg_ref[...])
    op_ref[...] = y

    @pl.when(pl.program_id(0) == pl.num_programs(0) - 1)
    def _():
        os_ref[...] = y[tm - os_ref.shape[0]:]


def combine(x, route, ys, pos, final_g, tm, n_sample):
    t, d = x.shape
    nt = t // tm
    assert n_sample <= tm
    kern = functools.partial(_combine_kernel, tm=tm)
    return pl.pallas_call(
        kern,
        grid=(nt,),
        in_specs=[pl.BlockSpec((1, 1, TOP_K * tm), lambda i: (i, 0, 0), memory_space=pltpu.SMEM),
                  pl.BlockSpec((tm, d), lambda i: (i, 0)),
                  pl.BlockSpec((tm, LANES), lambda i: (i, 0)),
                  pl.BlockSpec((1, d), lambda i: (0, 0)),
                  pl.BlockSpec(memory_space=pl.ANY)],
        out_specs=[pl.BlockSpec((tm, d), lambda i: (i, 0)),
                   pl.BlockSpec((n_sample, d), lambda i: (0, 0))],
        out_shape=[jax.ShapeDtypeStruct((t - n_sample, d), F32),
                   jax.ShapeDtypeStruct((n_sample, d), F32)],
        scratch_shapes=[pltpu.VMEM((TOP_K, tm, d), F32),
                        pltpu.SemaphoreType.DMA(())],
        compiler_params=_cparams("arbitrary"),
        name="moe_combine",
    )(pos.reshape(nt, 1, TOP_K * tm), x, route, final_g.reshape(1, d), ys)


def moe_residual_final_norm(x, norm_g, router_w, wg, wu, wd, final_g, tm, n_sample):
    t, d = x.shape
    experts = router_w.shape[1]
    h, route, cnt = router(x, norm_g, router_w, tm)
    counts = cnt[0, :experts].astype(I32)
    ntiles = (TOP_K * t + experts * (MOE_ROW_TILE - 1)) // MOE_ROW_TILE + 1
    nfull_max = max(TOP_K * t // MOE_ROW_TILE, 1)
    frow, fexp, nfull, trow, tsub, pos = plan(counts, route, ntiles, nfull_max)
    pos = pos[:, 4:4 + TOP_K]
    xs = dispatch(h, pos, ntiles * MOE_ROW_TILE, tm)
    ys = expert_ffn(xs, nfull, frow, fexp, trow, tsub, wg, wu, wd)
    return combine(x, route, ys, pos, final_g, tm, n_sample)


def _rope_tables(pos, half):
    inv = jnp.power(ROPE_THETA, -jnp.arange(half, dtype=F32) / half)
    ang = pos[:, None] * inv[None, :]
    return jnp.cos(ang), jnp.sin(ang)


def kernel(x_prompt, x_sample, state_retention, state_hgrn, norm_mix_g, norm_ffn_g, final_norm_g, ret_w_in, ret_gn_g, ret_w_out, hg_w_in, hg_lb_param, hg_norm_g, hg_w_out, ffn_w_gate, ffn_w_up, ffn_w_down, moe_router, moe_w_gate, moe_w_up, moe_w_down):
    bp, lp, d = x_prompt.shape
    bs, ls, _ = x_sample.shape
    assert ls == 1 and norm_mix_g.shape[0] == 2
    _, _, ret_heads, ret_dk, ret_dv = state_retention.shape
    _, _, hg_heads, hg_dk, _ = state_hgrn.shape
    tp = bp * lp
    t = tp + bs
    tm = _pick_tile(t, 768, 16)

    xp = x_prompt.reshape(tp, d)
    xs = x_sample.reshape(bs, d)

    log_gamma = jnp.log1p(-jnp.exp2(-5.0 - jnp.arange(ret_heads, dtype=F32)))
    cos_p, sin_p = _rope_tables(jnp.arange(lp, dtype=F32), ret_dk // 2)
    cos_s, sin_s = _rope_tables(PAST_LEN + jnp.arange(ls, dtype=F32), ret_dk // 2)
    tm_proj = _pick_tile(t, 1536, 16)
    proj = norm_matmul_merged(xp, xs, norm_mix_g[0], ret_w_in[0].astype(BF16), tm_proj,
                              _pick_tile(ret_w_in.shape[2], 1024, LANES))
    gated, ret_p = retention_scan(proj, log_gamma, cos_p, sin_p, ret_gn_g[0], batch=bp, seq=lp,
                                  heads=ret_heads, dk=ret_dk, dv=ret_dv, rows_total=t)
    gated, ret_s = retention_step(proj, gated, state_retention[0], log_gamma, cos_s, sin_s, ret_gn_g[0],
                                  row0=tp, nb=bs, heads=ret_heads, dk=ret_dk, dv=ret_dv)
    x = matmul_residual_merged(gated, ret_w_out[0].astype(BF16), xp, xs, tm)
    ff = ffn_w_gate.shape[2]
    x = ffn_residual(x, norm_ffn_g[0], ffn_w_gate[0].astype(BF16), ffn_w_up[0].astype(BF16),
                     ffn_w_down[0].astype(BF16), tm, _pick_tile(ff, 1536, LANES))

    proj = norm_matmul(x, norm_mix_g[1], hg_w_in[0].astype(BF16), tm_proj,
                       _pick_tile(hg_w_in.shape[2], 1024, LANES))
    gated, hg_p = hgrn_scan(proj, hg_lb_param, hg_norm_g[0], batch=bp, seq=lp, heads=hg_heads, dk=hg_dk,
                            rows_total=t, layer=1)
    gated, hg_s = hgrn_step(proj, gated, state_hgrn[0], hg_lb_param, hg_norm_g[0], row0=tp, nb=bs,
                            heads=hg_heads, dk=hg_dk, layer=1)
    x = matmul_residual(gated, hg_w_out[0].astype(BF16), x, tm)
    y_p, y_s = moe_residual_final_norm(x, norm_ffn_g[1], moe_router[0], moe_w_gate[0], moe_w_up[0],
                                       moe_w_down[0], final_norm_g, tm, bs)

    return (y_p.reshape(bp, lp, d), y_s.reshape(bs, ls, d),
            ret_p[None], ret_s[None], hg_p[None], hg_s[None])
```

```python
import functools

import jax
import jax.numpy as jnp
from jax import lax
from jax.experimental import pallas as pl
from jax.experimental.pallas import tpu as pltpu

F32 = jnp.float32
BF16 = jnp.bfloat16
I32 = jnp.int32

NORM_EPS = 1e-6
GN_EPS = 1e-5
ROPE_THETA = 10000.0
PAST_LEN = 16384
TOP_K = 2

LANES = 128
VMEM_LIMIT = 56 * 1024 * 1024

RET_CHUNK = 256
HG_CHUNK = 128
HG_DIAG = 4
RET_SCAN_HEADS = 2
RET_STEP_SEQS = 2
HG_STEP_SEQS = 4
MOE_ROW_TILE = 1024
MOE_SUB = 256
MOE_F_TILE = 512
DMA_UNROLL = 8

NT_DIMS = (((1,), (1,)), ((), ()))
TN_DIMS = (((0,), (0,)), ((), ()))


def _cparams(*sem):
    return pltpu.CompilerParams(dimension_semantics=sem, vmem_limit_bytes=VMEM_LIMIT)


def _pick_tile(n, target, mult):
    best = None
    for t in range(mult, min(n, target) + 1, mult):
        if n % t == 0:
            best = t
    assert best is not None, (n, target, mult)
    return best


def _sigmoid(x):
    return 0.5 * jnp.tanh(0.5 * x) + 0.5


def _silu(x):
    h = 0.5 * x
    return h * jnp.tanh(h) + h


def _rms(x, g):
    return x * lax.rsqrt(jnp.mean(x * x, axis=-1, keepdims=True) + NORM_EPS) * g


def _row_group(b):
    return pl.multiple_of(lax.shift_left(lax.shift_right_logical(b, 3), 3), 8), b & 7


def _load_row(ref, b, cols):
    base, r = _row_group(b)
    blk = ref[pl.ds(base, 8), cols]
    rows = lax.broadcasted_iota(I32, blk.shape, 0)
    return jnp.sum(jnp.where(rows == r, blk, 0.0), axis=0, keepdims=True)


def _store_row(ref, b, cols, row):
    base, r = _row_group(b)
    blk = ref[pl.ds(base, 8), cols]
    rows = lax.broadcasted_iota(I32, blk.shape, 0)
    ref[pl.ds(base, 8), cols] = jnp.where(rows == r, row, blk)


def _norm_matmul_kernel(x_ref, g_ref, w_ref, o_ref, h_ref):
    @pl.when(pl.program_id(1) == 0)
    def _():
        h_ref[...] = _rms(x_ref[...], g_ref[...]).astype(BF16)

    o_ref[...] = jnp.dot(h_ref[...], w_ref[...], preferred_element_type=F32).astype(o_ref.dtype)


def norm_matmul(x, g, w, tm, tn):
    t, d = x.shape
    n = w.shape[1]
    return pl.pallas_call(
        _norm_matmul_kernel,
        grid=(t // tm, n // tn),
        in_specs=[pl.BlockSpec((tm, d), lambda i, j: (i, 0)),
                  pl.BlockSpec((1, d), lambda i, j: (0, 0)),
                  pl.BlockSpec((d, tn), lambda i, j: (0, j))],
        out_specs=pl.BlockSpec((tm, tn), lambda i, j: (i, j)),
        out_shape=jax.ShapeDtypeStruct((t, n), BF16),
        scratch_shapes=[pltpu.VMEM((tm, d), BF16)],
        compiler_params=_cparams("parallel", "arbitrary"),
        name="norm_matmul",
    )(x, g.reshape(1, d), w)


def _with_merged_rows(xp_ref, xs_ref, fn):
    i = pl.program_id(0)
    last = pl.num_programs(0) - 1
    n_prompt = xp_ref.shape[0] - xs_ref.shape[0]

    @pl.when(i < last)
    def _():
        fn(xp_ref[...])

    @pl.when(i == last)
    def _():
        fn(jnp.concatenate([xp_ref[:n_prompt], xs_ref[...]], axis=0))


def _merged_specs(tm, d, bs, nidx):
    if nidx == 1:
        return [pl.BlockSpec((tm, d), lambda i: (i, 0)), pl.BlockSpec((bs, d), lambda i: (0, 0))]
    return [pl.BlockSpec((tm, d), lambda i, j: (i, 0)), pl.BlockSpec((bs, d), lambda i, j: (0, 0))]


def _norm_matmul2_kernel(xp_ref, xs_ref, g_ref, w_ref, o_ref, h_ref):
    @pl.when(pl.program_id(1) == 0)
    def _():
        def fill(x):
            h_ref[...] = _rms(x, g_ref[...]).astype(BF16)

        _with_merged_rows(xp_ref, xs_ref, fill)

    o_ref[...] = jnp.dot(h_ref[...], w_ref[...], preferred_element_type=F32).astype(o_ref.dtype)


def norm_matmul_merged(xp, xs, g, w, tm, tn):
    tp, d = xp.shape
    bs = xs.shape[0]
    t = tp + bs
    n = w.shape[1]
    assert t % tm == 0 and bs <= tm
    return pl.pallas_call(
        _norm_matmul2_kernel,
        grid=(t // tm, n // tn),
        in_specs=_merged_specs(tm, d, bs, 2) + [pl.BlockSpec((1, d), lambda i, j: (0, 0)),
                                                 pl.BlockSpec((d, tn), lambda i, j: (0, j))],
        out_specs=pl.BlockSpec((tm, tn), lambda i, j: (i, j)),
        out_shape=jax.ShapeDtypeStruct((t, n), BF16),
        scratch_shapes=[pltpu.VMEM((tm, d), BF16)],
        compiler_params=_cparams("parallel", "arbitrary"),
        name="norm_matmul_merged",
    )(xp, xs, g.reshape(1, d), w)


def _matmul_res_kernel(a_ref, w_ref, x_ref, o_ref):
    o_ref[...] = x_ref[...] + jnp.dot(a_ref[...], w_ref[...], preferred_element_type=F32)


def matmul_residual(a, w, x, tm):
    t, k = a.shape
    n = w.shape[1]
    return pl.pallas_call(
        _matmul_res_kernel,
        grid=(t // tm,),
        in_specs=[pl.BlockSpec((tm, k), lambda i: (i, 0)),
                  pl.BlockSpec((k, n), lambda i: (0, 0)),
                  pl.BlockSpec((tm, n), lambda i: (i, 0))],
        out_specs=pl.BlockSpec((tm, n), lambda i: (i, 0)),
        out_shape=jax.ShapeDtypeStruct((t, n), F32),
        compiler_params=_cparams("parallel"),
        name="matmul_residual",
    )(a, w, x)


def _matmul_res2_kernel(a_ref, w_ref, xp_ref, xs_ref, o_ref):
    def fill(x):
        o_ref[...] = x + jnp.dot(a_ref[...], w_ref[...], preferred_element_type=F32)

    _with_merged_rows(xp_ref, xs_ref, fill)


def matmul_residual_merged(a, w, xp, xs, tm):
    t, k = a.shape
    n = w.shape[1]
    bs = xs.shape[0]
    assert t % tm == 0 and bs <= tm and xp.shape[0] + bs == t
    return pl.pallas_call(
        _matmul_res2_kernel,
        grid=(t // tm,),
        in_specs=[pl.BlockSpec((tm, k), lambda i: (i, 0)),
                  pl.BlockSpec((k, n), lambda i: (0, 0))] + _merged_specs(tm, n, bs, 1),
        out_specs=pl.BlockSpec((tm, n), lambda i: (i, 0)),
        out_shape=jax.ShapeDtypeStruct((t, n), F32),
        compiler_params=_cparams("parallel"),
        name="matmul_residual_merged",
    )(a, w, xp, xs)


def _ffn_kernel(x_ref, g_ref, wg_ref, wu_ref, wd_ref, o_ref, h_ref):
    f = pl.program_id(1)

    @pl.when(f == 0)
    def _():
        x = x_ref[...]
        h_ref[...] = _rms(x, g_ref[...]).astype(BF16)
        o_ref[...] = x

    h = h_ref[...]
    a = jnp.dot(h, wg_ref[...], preferred_element_type=F32)
    u = jnp.dot(h, wu_ref[...], preferred_element_type=F32)
    act = (_silu(a) * u).astype(BF16)
    o_ref[...] += jnp.dot(act, wd_ref[...], preferred_element_type=F32)


def ffn_residual(x, g, wg, wu, wd, tm, tf):
    t, d = x.shape
    f = wg.shape[1]
    return pl.pallas_call(
        _ffn_kernel,
        grid=(t // tm, f // tf),
        in_specs=[pl.BlockSpec((tm, d), lambda i, j: (i, 0)),
                  pl.BlockSpec((1, d), lambda i, j: (0, 0)),
                  pl.BlockSpec((d, tf), lambda i, j: (0, j)),
                  pl.BlockSpec((d, tf), lambda i, j: (0, j)),
                  pl.BlockSpec((tf, d), lambda i, j: (j, 0))],
        out_specs=pl.BlockSpec((tm, d), lambda i, j: (i, 0)),
        out_shape=jax.ShapeDtypeStruct((t, d), F32),
        scratch_shapes=[pltpu.VMEM((tm, d), BF16)],
        compiler_params=_cparams("parallel", "arbitrary"),
        name="ffn_residual",
    )(x, g.reshape(1, d), wg, wu, wd)


def _rotary(x, cos, sin, half):
    x1 = x[:, :half]
    x2 = x[:, half:]
    return jnp.concatenate([x1 * cos - x2 * sin, x2 * cos + x1 * sin], axis=-1)


def _group_norm_gate(o, gate, gain):
    mu = jnp.mean(o, axis=-1, keepdims=True)
    d = o - mu
    var = jnp.mean(d * d, axis=-1, keepdims=True)
    return _silu(gate) * (d * lax.rsqrt(var + GN_EPS) * gain)


def _ret_scan_kernel(lg_ref, q_ref, k_ref, v_ref, g_ref, cos_ref, sin_ref, gn_ref,
                     o_ref, sfin_ref, s_ref, mask_ref, din_ref, dout_ref, *, chunk, dk, dv, hp):
    c = pl.program_id(2)
    head0 = pl.program_id(1) * hp

    @pl.when(c == 0)
    def _():
        s_ref[...] = jnp.zeros_like(s_ref)
        t = lax.broadcasted_iota(I32, (chunk, LANES), 0).astype(F32)
        ti = lax.broadcasted_iota(I32, (chunk, chunk), 0)
        si = lax.broadcasted_iota(I32, (chunk, chunk), 1)
        causal = ti >= si
        rel = jnp.where(causal, (ti - si).astype(F32), 0.0)
        for j in range(hp):
            lg = lg_ref[head0 + j]
            din_ref[j] = jnp.exp((t + 1.0) * lg)
            dout_ref[j] = jnp.exp((chunk - 1.0 - t) * lg)
            mask_ref[j] = jnp.where(causal, jnp.exp(rel * lg), 0.0)

    half = dk // 2
    cos = cos_ref[...]
    sin = sin_ref[...]
    for j in range(hp):
        qk_cols = slice(j * dk, (j + 1) * dk)
        v_cols = slice(j * dv, (j + 1) * dv)
        q = _rotary(q_ref[:, qk_cols].astype(F32), cos, sin, half)
        k = _rotary(k_ref[:, qk_cols].astype(F32), cos, sin, half) * (dk ** -0.5)
        v = v_ref[:, v_cols]
        decay_in = _lane_tile(din_ref[j], dk)
        decay_out = _lane_tile(dout_ref[j], dk)
        decay_chunk = jnp.exp(jnp.zeros((1, dv), F32) + chunk * lg_ref[head0 + j])

        s = s_ref[j]
        scores = (lax.dot_general(q.astype(BF16), k.astype(BF16), NT_DIMS, preferred_element_type=F32)
                  * mask_ref[j])
        o = (jnp.dot(scores.astype(BF16), v, preferred_element_type=F32)
             + jnp.dot((q * decay_in).astype(BF16), s.astype(BF16), preferred_element_type=F32))
        kd_t = (k * decay_out).T.astype(BF16)
        s_ref[j] = s * decay_chunk + jnp.dot(kd_t, v, preferred_element_type=F32)
        o_ref[:, v_cols] = _group_norm_gate(o, g_ref[:, v_cols].astype(F32), gn_ref[:, v_cols]).astype(BF16)

    @pl.when(c == pl.num_programs(2) - 1)
    def _():
        sfin_ref[0] = s_ref[...]


def retention_scan(proj, log_gamma, cos, sin, gn_g, *, batch, seq, heads, dk, dv, rows_total):
    chunk = min(RET_CHUNK, seq)
    nc = seq // chunk
    hp = RET_SCAN_HEADS
    assert heads % hp == 0
    ng = heads // hp
    kern = functools.partial(_ret_scan_kernel, chunk=chunk, dk=dk, dv=dv, hp=hp)
    v0 = 2 * heads * dk // (hp * dv)
    return pl.pallas_call(
        kern,
        grid=(batch, ng, nc),
        in_specs=[pl.BlockSpec(memory_space=pltpu.SMEM),
                  pl.BlockSpec((chunk, hp * dk), lambda b, h, c: (b * nc + c, h)),
                  pl.BlockSpec((chunk, hp * dk), lambda b, h, c: (b * nc + c, ng + h)),
                  pl.BlockSpec((chunk, hp * dv), lambda b, h, c: (b * nc + c, v0 + h)),
                  pl.BlockSpec((chunk, hp * dv), lambda b, h, c: (b * nc + c, v0 + ng + h)),
                  pl.BlockSpec((chunk, dk // 2), lambda b, h, c: (c, 0)),
                  pl.BlockSpec((chunk, dk // 2), lambda b, h, c: (c, 0)),
                  pl.BlockSpec((1, hp * dv), lambda b, h, c: (0, h))],
        out_specs=[pl.BlockSpec((chunk, hp * dv), lambda b, h, c: (b * nc + c, h)),
                   pl.BlockSpec((1, hp, dk, dv), lambda b, h, c: (b, h, 0, 0))],
        out_shape=[jax.ShapeDtypeStruct((rows_total, heads * dv), BF16),
                   jax.ShapeDtypeStruct((batch, heads, dk, dv), F32)],
        scratch_shapes=[pltpu.VMEM((hp, dk, dv), F32),
                        pltpu.VMEM((hp, chunk, chunk), F32),
                        pltpu.VMEM((hp, chunk, LANES), F32),
                        pltpu.VMEM((hp, chunk, LANES), F32)],
        compiler_params=_cparams("parallel", "parallel", "arbitrary"),
        name="retention_scan",
    )(log_gamma, proj, proj, proj, proj, cos, sin, gn_g.reshape(1, heads * dv))


def _one_hot_rows(nb, b):
    return jnp.where(lax.broadcasted_iota(I32, (nb, LANES), 0) == b, 1.0, 0.0).astype(BF16)


def _lane_tile(x, width):
    return jnp.concatenate([x] * (width // x.shape[1]), axis=1)


def _ret_step_kernel(lg_ref, q_ref, k_ref, v_ref, g_ref, cos_ref, sin_ref, gn_ref, s_ref, prev_ref,
                     o_ref, snew_ref, qt_ref, kt_ref, vf_ref, gf_ref, orow_ref, *, heads, dk, dv, nb, per_step):
    del prev_ref
    step = pl.program_id(0)
    half = dk // 2

    @pl.when(step == 0)
    def _():
        orow_ref[...] = jnp.zeros_like(orow_ref)
        vf_ref[...] = v_ref[...].astype(F32)
        gf_ref[...] = g_ref[...].astype(F32)
        cos = cos_ref[...]
        sin = sin_ref[...]
        for h in range(heads):
            qh = _rotary(q_ref[:, h * dk:(h + 1) * dk].astype(F32), cos, sin, half)
            kh = _rotary(k_ref[:, h * dk:(h + 1) * dk].astype(F32), cos, sin, half) * (dk ** -0.5)
            qt_ref[h] = qh.T.astype(BF16)
            kt_ref[h] = kh.T.astype(BF16)

    for i in range(per_step):
        b = step * per_step + i
        onehot = _one_hot_rows(nb, b)
        for h in range(heads):
            cols = slice(h * dv, (h + 1) * dv)
            kcol = _lane_tile(jnp.dot(kt_ref[h], onehot, preferred_element_type=F32), dv)
            qcol = _lane_tile(jnp.dot(qt_ref[h], onehot, preferred_element_type=F32), dv)
            vrow = _load_row(vf_ref, b, cols)
            gamma = jnp.exp(jnp.zeros((1, dv), F32) + lg_ref[h])
            s_new = s_ref[i, h] * gamma + kcol * vrow
            snew_ref[i, h] = s_new
            o = jnp.sum(s_new * qcol, axis=0, keepdims=True)
            _store_row(orow_ref, b, cols, _group_norm_gate(o, _load_row(gf_ref, b, cols), gn_ref[:, cols]))

    @pl.when(step == pl.num_programs(0) - 1)
    def _():
        o_ref[...] = orow_ref[...].astype(BF16)


def retention_step(proj, gated, state, log_gamma, cos, sin, gn_g, *, row0, nb, heads, dk, dv):
    per_step = RET_STEP_SEQS
    kern = functools.partial(_ret_step_kernel, heads=heads, dk=dk, dv=dv, nb=nb, per_step=per_step)
    rb = row0 // nb
    wq = heads * dk
    wv = heads * dv
    return pl.pallas_call(
        kern,
        grid=(nb // per_step,),
        in_specs=[pl.BlockSpec(memory_space=pltpu.SMEM),
                  pl.BlockSpec((nb, wq), lambda b: (rb, 0)),
                  pl.BlockSpec((nb, wq), lambda b: (rb, 1)),
                  pl.BlockSpec((nb, wv), lambda b: (rb, 2 * wq // wv)),
                  pl.BlockSpec((nb, wv), lambda b: (rb, 2 * wq // wv + 1)),
                  pl.BlockSpec((1, dk // 2), lambda b: (0, 0)),
                  pl.BlockSpec((1, dk // 2), lambda b: (0, 0)),
                  pl.BlockSpec((1, wv), lambda b: (0, 0)),
                  pl.BlockSpec((per_step, heads, dk, dv), lambda b: (b, 0, 0, 0)),
                  pl.BlockSpec(memory_space=pl.ANY)],
        out_specs=[pl.BlockSpec((nb, wv), lambda b: (rb, 0)),
                   pl.BlockSpec((per_step, heads, dk, dv), lambda b: (b, 0, 0, 0))],
        out_shape=[jax.ShapeDtypeStruct(gated.shape, gated.dtype),
                   jax.ShapeDtypeStruct(state.shape, F32)],
        scratch_shapes=[pltpu.VMEM((heads, dk, nb), BF16),
                        pltpu.VMEM((heads, dk, nb), BF16),
                        pltpu.VMEM((nb, wv), F32),
                        pltpu.VMEM((nb, wv), F32),
                        pltpu.VMEM((nb, wv), F32)],
        input_output_aliases={9: 0},
        compiler_params=_cparams("arbitrary"),
        name="retention_step",
    )(log_gamma, proj, proj, proj, proj, cos, sin, gn_g.reshape(1, wv), state, gated)


def _lower_bound(lbp, layer):
    m = jnp.max(lbp, axis=0, keepdims=True)
    e = jnp.exp(lbp - m)
    p = e / jnp.sum(e, axis=0, keepdims=True)
    return jnp.sum(p[:layer + 1], axis=0, keepdims=True) - p[0:1]


def _split_dot(mat_bf16, x):
    hi = x.astype(BF16)
    r1 = x - hi.astype(F32)
    mid = r1.astype(BF16)
    lo = (r1 - mid.astype(F32)).astype(BF16)
    return (jnp.dot(mat_bf16, hi, preferred_element_type=F32)
            + jnp.dot(mat_bf16, mid, preferred_element_type=F32)
            + jnp.dot(mat_bf16, lo, preferred_element_type=F32))


def _rms_gate(o, gate, gain):
    return _silu(gate) * (o * lax.rsqrt(jnp.mean(o * o, axis=-1, keepdims=True) + NORM_EPS) * gain)


def _group_row(x, s, group):
    n, w = x.shape
    x3 = x.reshape(n // group, group, w)
    return jnp.broadcast_to(x3[:, s:s + 1, :], x3.shape).reshape(n, w)


def _hgrn_scan_kernel(lbp_ref, q_ref, f_ref, i_ref, g_ref, ng_ref, o_ref, sfin_ref, st_ref,
                      *, chunk, heads, dk, diag, layer):
    c = pl.program_id(1)

    @pl.when(c == 0)
    def _():
        st_ref[...] = jnp.zeros_like(st_ref)

    w = heads * dk
    lb = _lower_bound(lbp_ref[...], layer)
    qs = _silu(q_ref[...].astype(F32))
    forget = lb + (1.0 - lb) * _sigmoid(f_ref[...].astype(F32))
    kk = 1.0 - forget
    logf = jnp.log(forget)

    row = lax.broadcasted_iota(I32, (chunk, chunk), 0)
    col = lax.broadcasted_iota(I32, (chunk, chunk), 1)
    tril = jnp.where(row >= col, 1.0, 0.0).astype(BF16)
    bcum = _split_dot(tril, logf)
    blast = bcum[chunk - 1:chunk, :]
    qe = (qs * jnp.exp(bcum)).astype(BF16)
    kdec = (kk * jnp.exp(blast - bcum)).astype(BF16)
    lk = jnp.log(kk) - bcum
    iv = i_ref[...]
    gate = g_ref[...].astype(F32)
    gain = ng_ref[...]

    u = jnp.where(row > col, row ^ col, 0)
    rowv = lax.broadcasted_iota(I32, (chunk, 1), 0)
    levels = []
    m = chunk // 2
    while m >= diag:
        span = 2 * m
        bref = jnp.concatenate(
            [jnp.broadcast_to(bcum[g * span + m - 1:g * span + m], (span, w)) for g in range(chunk // span)],
            axis=0)
        upper = (rowv & m) != 0
        d = bcum - bref
        x = (jnp.where(upper, qs, kk) * jnp.exp(jnp.where(upper, d, -d))).astype(BF16)
        levels.append((x, lax.shift_right_logical(u, m.bit_length() - 1) == 1))
        m //= 2
    in_block = row & (diag - 1)
    block_col0 = row - in_block
    diag_masks = [(col == block_col0 + s) & (in_block >= s) for s in range(diag)]
    ones = jnp.ones((dk, LANES), BF16)

    for h in range(heads):
        cols = slice(h * dk, (h + 1) * dk)
        qs_h = qs[:, cols]
        lk_h = lk[:, cols]
        b_h = bcum[:, cols]
        a = jnp.zeros((chunk, chunk), F32)
        for x, mask in levels:
            x_h = x[:, cols]
            a = jnp.where(mask, lax.dot_general(x_h, x_h, NT_DIMS, preferred_element_type=F32), a)
        ws = [qs_h * jnp.exp(b_h + _group_row(lk_h, s, diag)) for s in range(diag)]
        rsum = jnp.dot(jnp.concatenate(ws, axis=0).astype(BF16), ones, preferred_element_type=F32)
        for s in range(diag):
            a = jnp.where(diag_masks[s], rsum[s * chunk:(s + 1) * chunk], a)
        st = st_ref[h]
        i_h = iv[:, cols]
        o_h = (jnp.dot(a.astype(BF16), i_h, preferred_element_type=F32)
               + lax.dot_general(qe[:, cols], st.astype(BF16), NT_DIMS, preferred_element_type=F32))
        st_new = (st * jnp.exp(blast[:, cols])
                  + lax.dot_general(i_h, kdec[:, cols], TN_DIMS, preferred_element_type=F32))
        st_ref[h] = st_new
        o_ref[:, cols] = _rms_gate(o_h, gate[:, cols], gain[:, cols]).astype(BF16)

    @pl.when(c == pl.num_programs(1) - 1)
    def _():
        for h in range(heads):
            sfin_ref[0, h] = st_ref[h].T


def hgrn_scan(proj, lb_param, norm_g, *, batch, seq, heads, dk, rows_total, layer):
    chunk = min(HG_CHUNK, seq)
    assert chunk == LANES and dk == LANES
    nc = seq // chunk
    w = heads * dk
    kern = functools.partial(_hgrn_scan_kernel, chunk=chunk, heads=heads, dk=dk, diag=HG_DIAG, layer=layer)
    return pl.pallas_call(
        kern,
        grid=(batch, nc),
        in_specs=[pl.BlockSpec(lb_param.shape, lambda b, c: (0, 0)),
                  pl.BlockSpec((chunk, w), lambda b, c: (b * nc + c, 0)),
                  pl.BlockSpec((chunk, w), lambda b, c: (b * nc + c, 1)),
                  pl.BlockSpec((chunk, w), lambda b, c: (b * nc + c, 2)),
                  pl.BlockSpec((chunk, w), lambda b, c: (b * nc + c, 3)),
                  pl.BlockSpec((1, w), lambda b, c: (0, 0))],
        out_specs=[pl.BlockSpec((chunk, w), lambda b, c: (b * nc + c, 0)),
                   pl.BlockSpec((1, heads, dk, dk), lambda b, c: (b, 0, 0, 0))],
        out_shape=[jax.ShapeDtypeStruct((rows_total, w), BF16),
                   jax.ShapeDtypeStruct((batch, heads, dk, dk), F32)],
        scratch_shapes=[pltpu.VMEM((heads, dk, dk), F32)],
        compiler_params=_cparams("parallel", "arbitrary"),
        name="hgrn_scan",
    )(lb_param, proj, proj, proj, proj, norm_g.reshape(1, w))


def _hgrn_step_kernel(lbp_ref, q_ref, f_ref, i_ref, g_ref, ng_ref, s_ref, prev_ref,
                      o_ref, snew_ref, qt_ref, ft_ref, if_ref, gf_ref, orow_ref,
                      *, heads, dk, nb, layer, per_step):
    del prev_ref
    step = pl.program_id(0)

    @pl.when(step == 0)
    def _():
        orow_ref[...] = jnp.zeros_like(orow_ref)
        if_ref[...] = i_ref[...].astype(F32)
        gf_ref[...] = g_ref[...].astype(F32)
        lb = _lower_bound(lbp_ref[...], layer)
        qs = _silu(q_ref[...].astype(F32))
        forget = lb + (1.0 - lb) * _sigmoid(f_ref[...].astype(F32))
        for h in range(heads):
            cols = slice(h * dk, (h + 1) * dk)
            qt_ref[h] = qs[:, cols].T.astype(BF16)
            ft = forget[:, cols].T
            hi = ft.astype(BF16)
            r1 = ft - hi.astype(F32)
            mid = r1.astype(BF16)
            ft_ref[0, h] = hi
            ft_ref[1, h] = mid
            ft_ref[2, h] = (r1 - mid.astype(F32)).astype(BF16)

    for i in range(per_step):
        b = step * per_step + i
        onehot = _one_hot_rows(nb, b)
        for h in range(heads):
            cols = slice(h * dk, (h + 1) * dk)
            fcol = (jnp.dot(ft_ref[0, h], onehot, preferred_element_type=F32)
                    + jnp.dot(ft_ref[1, h], onehot, preferred_element_type=F32)
                    + jnp.dot(ft_ref[2, h], onehot, preferred_element_type=F32))
            qcol = jnp.dot(qt_ref[h], onehot, preferred_element_type=F32)
            irow = _load_row(if_ref, b, cols)
            s_new = s_ref[i, h] * fcol + (1.0 - fcol) * irow
            snew_ref[i, h] = s_new
            o = jnp.sum(s_new * qcol, axis=0, keepdims=True)
            _store_row(orow_ref, b, cols, _rms_gate(o, _load_row(gf_ref, b, cols), ng_ref[:, cols]))

    @pl.when(step == pl.num_programs(0) - 1)
    def _():
        o_ref[...] = orow_ref[...].astype(BF16)


def hgrn_step(proj, gated, state, lb_param, norm_g, *, row0, nb, heads, dk, layer):
    per_step = HG_STEP_SEQS
    kern = functools.partial(_hgrn_step_kernel, heads=heads, dk=dk, nb=nb, layer=layer, per_step=per_step)
    rb = row0 // nb
    w = heads * dk
    return pl.pallas_call(
        kern,
        grid=(nb // per_step,),
        in_specs=[pl.BlockSpec(lb_param.shape, lambda b: (0, 0)),
                  pl.BlockSpec((nb, w), lambda b: (rb, 0)),
                  pl.BlockSpec((nb, w), lambda b: (rb, 1)),
                  pl.BlockSpec((nb, w), lambda b: (rb, 2)),
                  pl.BlockSpec((nb, w), lambda b: (rb, 3)),
                  pl.BlockSpec((1, w), lambda b: (0, 0)),
                  pl.BlockSpec((per_step, heads, dk, dk), lambda b: (b, 0, 0, 0)),
                  pl.BlockSpec(memory_space=pl.ANY)],
        out_specs=[pl.BlockSpec((nb, w), lambda b: (rb, 0)),
                   pl.BlockSpec((per_step, heads, dk, dk), lambda b: (b, 0, 0, 0))],
        out_shape=[jax.ShapeDtypeStruct(gated.shape, gated.dtype),
                   jax.ShapeDtypeStruct(state.shape, F32)],
        scratch_shapes=[pltpu.VMEM((heads, dk, nb), BF16),
                        pltpu.VMEM((3, heads, dk, nb), BF16),
                        pltpu.VMEM((nb, w), F32),
                        pltpu.VMEM((nb, w), F32),
                        pltpu.VMEM((nb, w), F32)],
        input_output_aliases={7: 0},
        compiler_params=_cparams("arbitrary"),
        name="hgrn_step",
    )(lb_param, proj, proj, proj, proj, norm_g.reshape(1, w), state, gated)


def _router_kernel(x_ref, g_ref, r_ref, h_ref, route_ref, cnt_ref, carry_ref, *, tm, experts):
    i = pl.program_id(0)

    @pl.when(i == 0)
    def _():
        carry_ref[...] = jnp.zeros_like(carry_ref)

    h = _rms(x_ref[...], g_ref[...])
    h_ref[...] = h
    logits = jnp.dot(h, r_ref[...], preferred_element_type=F32, precision=lax.Precision.HIGHEST)
    lane = lax.broadcasted_iota(I32, (tm, LANES), 1)
    valid = lane < experts
    z = jnp.where(valid, logits, -jnp.inf)
    ez = jnp.exp(z - jnp.max(z, axis=-1, keepdims=True))
    p = ez / jnp.sum(ez, axis=-1, keepdims=True)
    p = jnp.where(valid, p, -1.0)
    lane_f = lane.astype(F32)
    v1 = jnp.max(p, axis=-1, keepdims=True)
    i1 = jnp.min(jnp.where(p == v1, lane_f, float(LANES)), axis=-1, keepdims=True)
    p2 = jnp.where(lane_f == i1, -1.0, p)
    v2 = jnp.max(p2, axis=-1, keepdims=True)
    i2 = jnp.min(jnp.where(p2 == v2, lane_f, float(LANES)), axis=-1, keepdims=True)
    den = v1 + v2
    m0 = jnp.where(lane_f == i1, 1.0, 0.0)
    m1 = jnp.where(lane_f == i2, 1.0, 0.0)
    msum = m0 + m1
    ri = lax.broadcasted_iota(I32, (tm, tm), 0)
    ci = lax.broadcasted_iota(I32, (tm, tm), 1)
    strict = jnp.where(ri > ci, 1.0, 0.0).astype(BF16)
    before = jnp.dot(strict, msum.astype(BF16), preferred_element_type=F32) + carry_ref[...]
    rank0 = jnp.sum(m0 * before, axis=-1, keepdims=True)
    rank1 = jnp.sum(m1 * before, axis=-1, keepdims=True)
    total = carry_ref[...] + jnp.sum(msum, axis=0, keepdims=True)
    carry_ref[...] = total
    cnt_ref[...] = jnp.broadcast_to(total, cnt_ref.shape)
    route = jnp.where(lane == 0, i1,
            jnp.where(lane == 1, i2,
            jnp.where(lane == 2, v1 / den,
            jnp.where(lane == 3, v2 / den,
            jnp.where(lane == 4, rank0,
            jnp.where(lane == 5, rank1, 0.0))))))
    route_ref[...] = route


def router(x, g, router_w, tm):
    t, d = x.shape
    experts = router_w.shape[1]
    rpad = jnp.zeros((d, LANES), F32).at[:, :experts].set(router_w)
    kern = functools.partial(_router_kernel, tm=tm, experts=experts)
    return pl.pallas_call(
        kern,
        grid=(t // tm,),
        in_specs=[pl.BlockSpec((tm, d), lambda i: (i, 0)),
                  pl.BlockSpec((1, d), lambda i: (0, 0)),
                  pl.BlockSpec((d, LANES), lambda i: (0, 0))],
        out_specs=[pl.BlockSpec((tm, d), lambda i: (i, 0)),
                   pl.BlockSpec((tm, LANES), lambda i: (i, 0)),
                   pl.BlockSpec((8, LANES), lambda i: (0, 0))],
        out_shape=[jax.ShapeDtypeStruct((t, d), F32),
                   jax.ShapeDtypeStruct((t, LANES), F32),
                   jax.ShapeDtypeStruct((8, LANES), F32)],
        scratch_shapes=[pltpu.VMEM((1, LANES), F32)],
        compiler_params=_cparams("arbitrary"),
        name="router",
    )(x, g.reshape(1, d), rpad)


def _plan_kernel(cnt_ref, route_ref, frow_ref, fexp_ref, nfull_ref, trow_ref, tsub_ref, zrow_ref, pos_ref,
                 *, experts, nfull_max, dummy, row_tile, sub):
    tile0 = jnp.int32(0)
    nfull = jnp.int32(0)
    last = jnp.int32(0)
    route = route_ref[...]
    slot_expert = pltpu.roll(route, 4, axis=1)
    first_row = jnp.zeros_like(route)
    for e in range(experts):
        n = cnt_ref[e]
        first_row = jnp.where(slot_expert == float(e), (tile0 * row_tile).astype(F32), first_row)
        full = n // row_tile
        rem = n - full * row_tile

        def fill(j, carry, e=e, tile0=tile0, nfull=nfull):
            frow_ref[nfull + j] = tile0 + j
            fexp_ref[nfull + j] = e
            return carry

        lax.fori_loop(0, full, fill, 0)
        trow_ref[e] = jnp.where(rem > 0, tile0 + full, dummy)
        tsub_ref[e] = (rem + (sub - 1)) // sub
        zrow_ref[e] = tile0 * row_tile + (n // sub) * sub
        last = jnp.where(full > 0, e, last)
        nfull = nfull + full
        tile0 = tile0 + full + jnp.where(rem > 0, 1, 0)
    nfull_ref[0] = nfull
    pos_ref[...] = (first_row + route).astype(I32)

    def unused(j, carry):
        frow_ref[j] = dummy
        fexp_ref[j] = last
        return carry

    lax.fori_loop(nfull, nfull_max, unused, 0)


def plan(counts, route, ntiles, nfull_max):
    experts = counts.shape[0]
    kern = functools.partial(_plan_kernel, experts=experts, nfull_max=nfull_max, dummy=ntiles - 1,
                             row_tile=MOE_ROW_TILE, sub=MOE_SUB)
    smem = pl.BlockSpec(memory_space=pltpu.SMEM)
    vmem = pl.BlockSpec(memory_space=pltpu.VMEM)
    return pl.pallas_call(
        kern,
        in_specs=[smem, vmem],
        out_specs=[smem] * 6 + [vmem],
        out_shape=[jax.ShapeDtypeStruct((nfull_max,), I32),
                   jax.ShapeDtypeStruct((nfull_max,), I32),
                   jax.ShapeDtypeStruct((1,), I32),
                   jax.ShapeDtypeStruct((experts,), I32),
                   jax.ShapeDtypeStruct((experts,), I32),
                   jax.ShapeDtypeStruct((experts,), I32),
                   jax.ShapeDtypeStruct(route.shape, I32)],
        compiler_params=pltpu.CompilerParams(vmem_limit_bytes=VMEM_LIMIT),
        name="moe_plan",
    )(counts, route)


def _dispatch_kernel(zrow_ref, p_ref, h_ref, out_hbm, zero_ref, sem, zsem, *, tm, experts):
    @pl.when(pl.program_id(0) == 0)
    def _():
        zero_ref[...] = jnp.zeros_like(zero_ref)

        def zero_copy(e):
            row = pl.multiple_of(zrow_ref[e], MOE_SUB)
            return pltpu.make_async_copy(zero_ref, out_hbm.at[pl.ds(row, MOE_SUB)], zsem)

        for e in range(experts):
            zero_copy(e).start()
        for e in range(experts):
            zero_copy(e).wait()

    def row_copy(t, dst):
        return pltpu.make_async_copy(h_ref.at[pl.ds(t, 1)], out_hbm.at[pl.ds(dst, 1)], sem)

    def issue(t, carry):
        for k in range(TOP_K):
            row_copy(t, p_ref[0, 0, TOP_K * t + k]).start()
        return carry

    lax.fori_loop(0, tm, issue, 0, unroll=DMA_UNROLL)

    def drain(t, carry):
        for k in range(TOP_K):
            row_copy(0, 0).wait()
        return carry

    lax.fori_loop(0, tm, drain, 0, unroll=DMA_UNROLL)


def dispatch(h, pos, zrow, rows, tm):
    t, d = h.shape
    nt = t // tm
    kern = functools.partial(_dispatch_kernel, tm=tm, experts=zrow.shape[0])
    return pl.pallas_call(
        kern,
        grid_spec=pltpu.PrefetchScalarGridSpec(
            num_scalar_prefetch=1,
            grid=(nt,),
            in_specs=[pl.BlockSpec((1, 1, TOP_K * tm), lambda i, zrow: (i, 0, 0), memory_space=pltpu.SMEM),
                      pl.BlockSpec((tm, d), lambda i, zrow: (i, 0))],
            out_specs=pl.BlockSpec(memory_space=pl.ANY),
            scratch_shapes=[pltpu.VMEM((MOE_SUB, d), h.dtype),
                            pltpu.SemaphoreType.DMA(()),
                            pltpu.SemaphoreType.DMA(())]),
        out_shape=jax.ShapeDtypeStruct((rows, d), h.dtype),
        compiler_params=_cparams("arbitrary"),
        name="moe_dispatch",
    )(zrow, pos.reshape(nt, 1, TOP_K * tm), h)


def _expert_full_kernel(nfull_ref, frow_ref, fexp_ref, x_ref, wg_ref, wu_ref, wd_ref, o_ref, h_ref):
    del frow_ref, fexp_ref
    j = pl.program_id(0)
    f = pl.program_id(1)
    active = j < nfull_ref[0]

    @pl.when(f == 0)
    def _():
        o_ref[...] = jnp.zeros_like(o_ref)

    @pl.when(active & (f == 0))
    def _():
        h_ref[...] = x_ref[...].astype(BF16)

    @pl.when(active)
    def _():
        h = h_ref[...]
        a = jnp.dot(h, wg_ref[0].astype(BF16), preferred_element_type=F32)
        u = jnp.dot(h, wu_ref[0].astype(BF16), preferred_element_type=F32)
        act = (_silu(a) * u).astype(BF16)
        o_ref[...] += jnp.dot(act, wd_ref[0].astype(BF16), preferred_element_type=F32)


def _expert_tail_kernel(trow_ref, tsub_ref, x_ref, wg_ref, wu_ref, wd_ref, prev_ref, o_ref,
                        h_ref, wgb_ref, wub_ref, wdb_ref, *, sub, nsub_max):
    del trow_ref, prev_ref
    f = pl.program_id(1)
    nsub = tsub_ref[pl.program_id(0)]

    @pl.when(f == 0)
    def _():
        o_ref[...] = jnp.zeros_like(o_ref)

    @pl.when(nsub > 0)
    def _():
        wgb_ref[...] = wg_ref[0].astype(BF16)
        wub_ref[...] = wu_ref[0].astype(BF16)
        wdb_ref[...] = wd_ref[0].astype(BF16)
        for sb in range(nsub_max):
            @pl.when(sb < nsub)
            def _():
                rows = slice(sb * sub, (sb + 1) * sub)

                @pl.when(f == 0)
                def _():
                    h_ref[rows] = x_ref[rows].astype(BF16)

                h = h_ref[rows]
                a = jnp.dot(h, wgb_ref[...], preferred_element_type=F32)
                u = jnp.dot(h, wub_ref[...], preferred_element_type=F32)
                act = (_silu(a) * u).astype(BF16)
                o_ref[rows] += jnp.dot(act, wdb_ref[...], preferred_element_type=F32)


def expert_ffn(xs, nfull, frow, fexp, trow, tsub, wg, wu, wd):
    rows, d = xs.shape
    experts, _, fdim = wg.shape
    tf = MOE_F_TILE
    nf = fdim // tf
    nfull_max = frow.shape[0]

    def fcol_full(j, f, nfull):
        return jnp.where(j < nfull[0], f, nf - 1)

    ys = pl.pallas_call(
        _expert_full_kernel,
        grid_spec=pltpu.PrefetchScalarGridSpec(
            num_scalar_prefetch=3,
            grid=(nfull_max, nf),
            in_specs=[pl.BlockSpec((MOE_ROW_TILE, d), lambda j, f, nfull, frow, fexp: (frow[j], 0)),
                      pl.BlockSpec((1, d, tf), lambda j, f, nfull, frow, fexp: (fexp[j], 0, fcol_full(j, f, nfull))),
                      pl.BlockSpec((1, d, tf), lambda j, f, nfull, frow, fexp: (fexp[j], 0, fcol_full(j, f, nfull))),
                      pl.BlockSpec((1, tf, d), lambda j, f, nfull, frow, fexp: (fexp[j], fcol_full(j, f, nfull), 0))],
            out_specs=pl.BlockSpec((MOE_ROW_TILE, d), lambda j, f, nfull, frow, fexp: (frow[j], 0)),
            scratch_shapes=[pltpu.VMEM((MOE_ROW_TILE, d), BF16)]),
        out_shape=jax.ShapeDtypeStruct((rows, d), F32),
        compiler_params=_cparams("arbitrary", "arbitrary"),
        name="expert_ffn_full",
    )(nfull, frow, fexp, xs, wg, wu, wd)

    def fcol_tail(e, f, tsub):
        return jnp.where(tsub[e] > 0, f, nf - 1)

    kern = functools.partial(_expert_tail_kernel, sub=MOE_SUB, nsub_max=MOE_ROW_TILE // MOE_SUB)
    return pl.pallas_call(
        kern,
        grid_spec=pltpu.PrefetchScalarGridSpec(
            num_scalar_prefetch=2,
            grid=(experts, nf),
            in_specs=[pl.BlockSpec((MOE_ROW_TILE, d), lambda e, f, trow, tsub: (trow[e], 0)),
                      pl.BlockSpec((1, d, tf), lambda e, f, trow, tsub: (e, 0, fcol_tail(e, f, tsub))),
                      pl.BlockSpec((1, d, tf), lambda e, f, trow, tsub: (e, 0, fcol_tail(e, f, tsub))),
                      pl.BlockSpec((1, tf, d), lambda e, f, trow, tsub: (e, fcol_tail(e, f, tsub), 0)),
                      pl.BlockSpec(memory_space=pl.ANY)],
            out_specs=pl.BlockSpec((MOE_ROW_TILE, d), lambda e, f, trow, tsub: (trow[e], 0)),
            scratch_shapes=[pltpu.VMEM((MOE_ROW_TILE, d), BF16),
                            pltpu.VMEM((d, tf), BF16),
                            pltpu.VMEM((d, tf), BF16),
                            pltpu.VMEM((tf, d), BF16)]),
        out_shape=jax.ShapeDtypeStruct((rows, d), F32),
        input_output_aliases={6: 0},
        compiler_params=_cparams("arbitrary", "arbitrary"),
        name="expert_ffn_tail",
    )(trow, tsub, xs, wg, wu, wd, ys)


def _combine_kernel(p_ref, x_ref, route_ref, fg_ref, ys_hbm, op_ref, os_ref, gath_ref, sem, *, tm):
    def row_copy(src, k, t):
        return pltpu.make_async_copy(ys_hbm.at[pl.ds(src, 1)], gath_ref.at[k, pl.ds(t, 1)], sem)

    def issue(t, carry):
        for k in range(TOP_K):
            row_copy(p_ref[0, 0, TOP_K * t + k], k, t).start()
        return carry

    lax.fori_loop(0, tm, issue, 0, unroll=DMA_UNROLL)

    def drain(t, carry):
        for k in range(TOP_K):
            row_copy(0, 0, 0).wait()
        return carry

    lax.fori_loop(0, tm, drain, 0, unroll=DMA_UNROLL)
    route = route_ref[...]
    x = x_ref[...] + route[:, 2:3] * gath_ref[0] + route[:, 3:4] * gath_ref[1]
    y = _rms(x, fg_ref[...])
    op_ref[...] = y

    @pl.when(pl.program_id(0) == pl.num_programs(0) - 1)
    def _():
        os_ref[...] = y[tm - os_ref.shape[0]:]


def combine(x, route, ys, pos, final_g, tm, n_sample):
    t, d = x.shape
    nt = t // tm
    assert n_sample <= tm
    kern = functools.partial(_combine_kernel, tm=tm)
    return pl.pallas_call(
        kern,
        grid=(nt,),
        in_specs=[pl.BlockSpec((1, 1, TOP_K * tm), lambda i: (i, 0, 0), memory_space=pltpu.SMEM),
                  pl.BlockSpec((tm, d), lambda i: (i, 0)),
                  pl.BlockSpec((tm, LANES), lambda i: (i, 0)),
                  pl.BlockSpec((1, d), lambda i: (0, 0)),
                  pl.BlockSpec(memory_space=pl.ANY)],
        out_specs=[pl.BlockSpec((tm, d), lambda i: (i, 0)),
                   pl.BlockSpec((n_sample, d), lambda i: (0, 0))],
        out_shape=[jax.ShapeDtypeStruct((t - n_sample, d), F32),
                   jax.ShapeDtypeStruct((n_sample, d), F32)],
        scratch_shapes=[pltpu.VMEM((TOP_K, tm, d), F32),
                        pltpu.SemaphoreType.DMA(())],
        compiler_params=_cparams("arbitrary"),
        name="moe_combine",
    )(pos.reshape(nt, 1, TOP_K * tm), x, route, final_g.reshape(1, d), ys)


def moe_residual_final_norm(x, norm_g, router_w, wg, wu, wd, final_g, tm, n_sample):
    t, d = x.shape
    experts = router_w.shape[1]
    h, route, cnt = router(x, norm_g, router_w, tm)
    counts = cnt[0, :experts].astype(I32)
    ntiles = (TOP_K * t + experts * (MOE_ROW_TILE - 1)) // MOE_ROW_TILE + 1
    nfull_max = max(TOP_K * t // MOE_ROW_TILE, 1)
    frow, fexp, nfull, trow, tsub, zrow, pos = plan(counts, route, ntiles, nfull_max)
    pos = pos[:, 4:4 + TOP_K]
    xs = dispatch(h, pos, zrow, ntiles * MOE_ROW_TILE, tm)
    ys = expert_ffn(xs, nfull, frow, fexp, trow, tsub, wg, wu, wd)
    return combine(x, route, ys, pos, final_g, tm, n_sample)


def _rope_tables(pos, half):
    inv = jnp.power(ROPE_THETA, -jnp.arange(half, dtype=F32) / half)
    ang = pos[:, None] * inv[None, :]
    return jnp.cos(ang), jnp.sin(ang)


def kernel(x_prompt, x_sample, state_retention, state_hgrn, norm_mix_g, norm_ffn_g, final_norm_g, ret_w_in, ret_gn_g, ret_w_out, hg_w_in, hg_lb_param, hg_norm_g, hg_w_out, ffn_w_gate, ffn_w_up, ffn_w_down, moe_router, moe_w_gate, moe_w_up, moe_w_down):
    bp, lp, d = x_prompt.shape
    bs, ls, _ = x_sample.shape
    assert ls == 1 and norm_mix_g.shape[0] == 2
    _, _, ret_heads, ret_dk, ret_dv = state_retention.shape
    _, _, hg_heads, hg_dk, _ = state_hgrn.shape
    tp = bp * lp
    t = tp + bs
    tm = _pick_tile(t, 768, 16)

    xp = x_prompt.reshape(tp, d)
    xs = x_sample.reshape(bs, d)

    log_gamma = jnp.log1p(-jnp.exp2(-5.0 - jnp.arange(ret_heads, dtype=F32)))
    cos_p, sin_p = _rope_tables(jnp.arange(lp, dtype=F32), ret_dk // 2)
    cos_s, sin_s = _rope_tables(PAST_LEN + jnp.arange(ls, dtype=F32), ret_dk // 2)
    tm_proj = _pick_tile(t, 1536, 16)
    proj = norm_matmul_merged(xp, xs, norm_mix_g[0], ret_w_in[0].astype(BF16), tm_proj,
                              _pick_tile(ret_w_in.shape[2], 1024, LANES))
    gated, ret_p = retention_scan(proj, log_gamma, cos_p, sin_p, ret_gn_g[0], batch=bp, seq=lp,
                                  heads=ret_heads, dk=ret_dk, dv=ret_dv, rows_total=t)
    gated, ret_s = retention_step(proj, gated, state_retention[0], log_gamma, cos_s, sin_s, ret_gn_g[0],
                                  row0=tp, nb=bs, heads=ret_heads, dk=ret_dk, dv=ret_dv)
    x = matmul_residual_merged(gated, ret_w_out[0].astype(BF16), xp, xs, tm)
    ff = ffn_w_gate.shape[2]
    x = ffn_residual(x, norm_ffn_g[0], ffn_w_gate[0].astype(BF16), ffn_w_up[0].astype(BF16),
                     ffn_w_down[0].astype(BF16), tm, _pick_tile(ff, 1536, LANES))

    proj = norm_matmul(x, norm_mix_g[1], hg_w_in[0].astype(BF16), tm_proj,
                       _pick_tile(hg_w_in.shape[2], 1024, LANES))
    gated, hg_p = hgrn_scan(proj, hg_lb_param, hg_norm_g[0], batch=bp, seq=lp, heads=hg_heads, dk=hg_dk,
                            rows_total=t, layer=1)
    gated, hg_s = hgrn_step(proj, gated, state_hgrn[0], hg_lb_param, hg_norm_g[0], row0=tp, nb=bs,
                            heads=hg_heads, dk=hg_dk, layer=1)
    x = matmul_residual(gated, hg_w_out[0].astype(BF16), x, tm)
    y_p, y_s = moe_residual_final_norm(x, norm_ffn_g[1], moe_router[0], moe_w_gate[0], moe_w_up[0],
                                       moe_w_down[0], final_norm_g, tm, bs)

    return (y_p.reshape(bp, lp, d), y_s.reshape(bs, ls, d),
            ret_p[None], ret_s[None], hg_p[None], hg_s[None])
```

```python
import functools

import jax
import jax.numpy as jnp
from jax import lax
from jax.experimental import pallas as pl
from jax.experimental.pallas import tpu as pltpu

F32 = jnp.float32
BF16 = jnp.bfloat16
I32 = jnp.int32

NORM_EPS = 1e-6
GN_EPS = 1e-5
ROPE_THETA = 10000.0
PAST_LEN = 16384
TOP_K = 2

LANES = 128
VMEM_LIMIT = 56 * 1024 * 1024

RET_CHUNK = 256
HG_CHUNK = 128
HG_DIAG = 4
RET_SCAN_HEADS = 2
RET_STEP_SEQS = 2
HG_STEP_SEQS = 4
MOE_ROW_TILE = 1024
MOE_SUB = 256
MOE_F_TILE = 512
ROUTER_GROUPS = 2
DMA_UNROLL = 8

NT_DIMS = (((1,), (1,)), ((), ()))
TN_DIMS = (((0,), (0,)), ((), ()))


def _cparams(*sem):
    return pltpu.CompilerParams(dimension_semantics=sem, vmem_limit_bytes=VMEM_LIMIT)


def _pick_tile(n, target, mult):
    best = None
    for t in range(mult, min(n, target) + 1, mult):
        if n % t == 0:
            best = t
    assert best is not None, (n, target, mult)
    return best


def _sigmoid(x):
    return 0.5 * jnp.tanh(0.5 * x) + 0.5


def _silu(x):
    h = 0.5 * x
    return h * jnp.tanh(h) + h


def _rms(x, g):
    return x * lax.rsqrt(jnp.mean(x * x, axis=-1, keepdims=True) + NORM_EPS) * g


def _row_group(b):
    return pl.multiple_of(lax.shift_left(lax.shift_right_logical(b, 3), 3), 8), b & 7


def _load_row(ref, b, cols):
    base, r = _row_group(b)
    blk = ref[pl.ds(base, 8), cols]
    rows = lax.broadcasted_iota(I32, blk.shape, 0)
    return jnp.sum(jnp.where(rows == r, blk, 0.0), axis=0, keepdims=True)


def _store_row(ref, b, cols, row):
    base, r = _row_group(b)
    blk = ref[pl.ds(base, 8), cols]
    rows = lax.broadcasted_iota(I32, blk.shape, 0)
    ref[pl.ds(base, 8), cols] = jnp.where(rows == r, row, blk)


def _norm_matmul_kernel(x_ref, g_ref, w_ref, o_ref, h_ref):
    @pl.when(pl.program_id(1) == 0)
    def _():
        h_ref[...] = _rms(x_ref[...], g_ref[...]).astype(BF16)

    o_ref[...] = jnp.dot(h_ref[...], w_ref[...], preferred_element_type=F32).astype(o_ref.dtype)


def norm_matmul(x, g, w, tm, tn):
    t, d = x.shape
    n = w.shape[1]
    return pl.pallas_call(
        _norm_matmul_kernel,
        grid=(t // tm, n // tn),
        in_specs=[pl.BlockSpec((tm, d), lambda i, j: (i, 0)),
                  pl.BlockSpec((1, d), lambda i, j: (0, 0)),
                  pl.BlockSpec((d, tn), lambda i, j: (0, j))],
        out_specs=pl.BlockSpec((tm, tn), lambda i, j: (i, j)),
        out_shape=jax.ShapeDtypeStruct((t, n), BF16),
        scratch_shapes=[pltpu.VMEM((tm, d), BF16)],
        compiler_params=_cparams("parallel", "arbitrary"),
        name="norm_matmul",
    )(x, g.reshape(1, d), w)


def _with_merged_rows(xp_ref, xs_ref, fn):
    i = pl.program_id(0)
    last = pl.num_programs(0) - 1
    n_prompt = xp_ref.shape[0] - xs_ref.shape[0]

    @pl.when(i < last)
    def _():
        fn(xp_ref[...])

    @pl.when(i == last)
    def _():
        fn(jnp.concatenate([xp_ref[:n_prompt], xs_ref[...]], axis=0))


def _merged_specs(tm, d, bs, nidx):
    if nidx == 1:
        return [pl.BlockSpec((tm, d), lambda i: (i, 0)), pl.BlockSpec((bs, d), lambda i: (0, 0))]
    return [pl.BlockSpec((tm, d), lambda i, j: (i, 0)), pl.BlockSpec((bs, d), lambda i, j: (0, 0))]


def _norm_matmul2_kernel(xp_ref, xs_ref, g_ref, w_ref, o_ref, h_ref):
    @pl.when(pl.program_id(1) == 0)
    def _():
        def fill(x):
            h_ref[...] = _rms(x, g_ref[...]).astype(BF16)

        _with_merged_rows(xp_ref, xs_ref, fill)

    o_ref[...] = jnp.dot(h_ref[...], w_ref[...], preferred_element_type=F32).astype(o_ref.dtype)


def norm_matmul_merged(xp, xs, g, w, tm, tn):
    tp, d = xp.shape
    bs = xs.shape[0]
    t = tp + bs
    n = w.shape[1]
    assert t % tm == 0 and bs <= tm
    return pl.pallas_call(
        _norm_matmul2_kernel,
        grid=(t // tm, n // tn),
        in_specs=_merged_specs(tm, d, bs, 2) + [pl.BlockSpec((1, d), lambda i, j: (0, 0)),
                                                 pl.BlockSpec((d, tn), lambda i, j: (0, j))],
        out_specs=pl.BlockSpec((tm, tn), lambda i, j: (i, j)),
        out_shape=jax.ShapeDtypeStruct((t, n), BF16),
        scratch_shapes=[pltpu.VMEM((tm, d), BF16)],
        compiler_params=_cparams("parallel", "arbitrary"),
        name="norm_matmul_merged",
    )(xp, xs, g.reshape(1, d), w)


def _matmul_res2_kernel(a_ref, w_ref, xp_ref, xs_ref, o_ref):
    def fill(x):
        o_ref[...] = x + jnp.dot(a_ref[...], w_ref[...], preferred_element_type=F32)

    _with_merged_rows(xp_ref, xs_ref, fill)


def matmul_residual_merged(a, w, xp, xs, tm):
    t, k = a.shape
    n = w.shape[1]
    bs = xs.shape[0]
    assert t % tm == 0 and bs <= tm and xp.shape[0] + bs == t
    return pl.pallas_call(
        _matmul_res2_kernel,
        grid=(t // tm,),
        in_specs=[pl.BlockSpec((tm, k), lambda i: (i, 0)),
                  pl.BlockSpec((k, n), lambda i: (0, 0))] + _merged_specs(tm, n, bs, 1),
        out_specs=pl.BlockSpec((tm, n), lambda i: (i, 0)),
        out_shape=jax.ShapeDtypeStruct((t, n), F32),
        compiler_params=_cparams("parallel"),
        name="matmul_residual_merged",
    )(a, w, xp, xs)


def _ffn_kernel(x_ref, g_ref, wg_ref, wu_ref, wd_ref, o_ref, h_ref):
    f = pl.program_id(1)

    @pl.when(f == 0)
    def _():
        x = x_ref[...]
        h_ref[...] = _rms(x, g_ref[...]).astype(BF16)
        o_ref[...] = x

    h = h_ref[...]
    a = jnp.dot(h, wg_ref[...], preferred_element_type=F32)
    u = jnp.dot(h, wu_ref[...], preferred_element_type=F32)
    act = (_silu(a) * u).astype(BF16)
    o_ref[...] += jnp.dot(act, wd_ref[...], preferred_element_type=F32)


def ffn_residual(x, g, wg, wu, wd, tm, tf):
    t, d = x.shape
    f = wg.shape[1]
    return pl.pallas_call(
        _ffn_kernel,
        grid=(t // tm, f // tf),
        in_specs=[pl.BlockSpec((tm, d), lambda i, j: (i, 0)),
                  pl.BlockSpec((1, d), lambda i, j: (0, 0)),
                  pl.BlockSpec((d, tf), lambda i, j: (0, j)),
                  pl.BlockSpec((d, tf), lambda i, j: (0, j)),
                  pl.BlockSpec((tf, d), lambda i, j: (j, 0))],
        out_specs=pl.BlockSpec((tm, d), lambda i, j: (i, 0)),
        out_shape=jax.ShapeDtypeStruct((t, d), F32),
        scratch_shapes=[pltpu.VMEM((tm, d), BF16)],
        compiler_params=_cparams("parallel", "arbitrary"),
        name="ffn_residual",
    )(x, g.reshape(1, d), wg, wu, wd)


def _rotary(x, cos, sin, half):
    x1 = x[:, :half]
    x2 = x[:, half:]
    return jnp.concatenate([x1 * cos - x2 * sin, x2 * cos + x1 * sin], axis=-1)


def _group_norm_gate(o, gate, gain):
    mu = jnp.mean(o, axis=-1, keepdims=True)
    d = o - mu
    var = jnp.mean(d * d, axis=-1, keepdims=True)
    return _silu(gate) * (d * lax.rsqrt(var + GN_EPS) * gain)


def _ret_scan_kernel(lg_ref, q_ref, k_ref, v_ref, g_ref, cos_ref, sin_ref, gn_ref,
                     o_ref, sfin_ref, s_ref, mask_ref, din_ref, dout_ref, *, chunk, dk, dv, hp):
    c = pl.program_id(2)
    head0 = pl.program_id(1) * hp

    @pl.when(c == 0)
    def _():
        s_ref[...] = jnp.zeros_like(s_ref)
        t = lax.broadcasted_iota(I32, (chunk, LANES), 0).astype(F32)
        ti = lax.broadcasted_iota(I32, (chunk, chunk), 0)
        si = lax.broadcasted_iota(I32, (chunk, chunk), 1)
        causal = ti >= si
        rel = jnp.where(causal, (ti - si).astype(F32), 0.0)
        for j in range(hp):
            lg = lg_ref[head0 + j]
            din_ref[j] = jnp.exp((t + 1.0) * lg)
            dout_ref[j] = jnp.exp((chunk - 1.0 - t) * lg)
            mask_ref[j] = jnp.where(causal, jnp.exp(rel * lg), 0.0)

    half = dk // 2
    cos = cos_ref[...]
    sin = sin_ref[...]
    for j in range(hp):
        qk_cols = slice(j * dk, (j + 1) * dk)
        v_cols = slice(j * dv, (j + 1) * dv)
        q = _rotary(q_ref[:, qk_cols].astype(F32), cos, sin, half)
        k = _rotary(k_ref[:, qk_cols].astype(F32), cos, sin, half) * (dk ** -0.5)
        v = v_ref[:, v_cols]
        decay_in = _lane_tile(din_ref[j], dk)
        decay_out = _lane_tile(dout_ref[j], dk)
        decay_chunk = jnp.exp(jnp.zeros((1, dv), F32) + chunk * lg_ref[head0 + j])

        s = s_ref[j]
        scores = (lax.dot_general(q.astype(BF16), k.astype(BF16), NT_DIMS, preferred_element_type=F32)
                  * mask_ref[j])
        o = (jnp.dot(scores.astype(BF16), v, preferred_element_type=F32)
             + jnp.dot((q * decay_in).astype(BF16), s.astype(BF16), preferred_element_type=F32))
        kd_t = (k * decay_out).T.astype(BF16)
        s_ref[j] = s * decay_chunk + jnp.dot(kd_t, v, preferred_element_type=F32)
        o_ref[:, v_cols] = _group_norm_gate(o, g_ref[:, v_cols].astype(F32), gn_ref[:, v_cols]).astype(BF16)

    @pl.when(c == pl.num_programs(2) - 1)
    def _():
        sfin_ref[0] = s_ref[...]


def retention_scan(proj, log_gamma, cos, sin, gn_g, *, batch, seq, heads, dk, dv, rows_total):
    chunk = min(RET_CHUNK, seq)
    nc = seq // chunk
    hp = RET_SCAN_HEADS
    assert heads % hp == 0
    ng = heads // hp
    kern = functools.partial(_ret_scan_kernel, chunk=chunk, dk=dk, dv=dv, hp=hp)
    v0 = 2 * heads * dk // (hp * dv)
    return pl.pallas_call(
        kern,
        grid=(batch, ng, nc),
        in_specs=[pl.BlockSpec(memory_space=pltpu.SMEM),
                  pl.BlockSpec((chunk, hp * dk), lambda b, h, c: (b * nc + c, h)),
                  pl.BlockSpec((chunk, hp * dk), lambda b, h, c: (b * nc + c, ng + h)),
                  pl.BlockSpec((chunk, hp * dv), lambda b, h, c: (b * nc + c, v0 + h)),
                  pl.BlockSpec((chunk, hp * dv), lambda b, h, c: (b * nc + c, v0 + ng + h)),
                  pl.BlockSpec((chunk, dk // 2), lambda b, h, c: (c, 0)),
                  pl.BlockSpec((chunk, dk // 2), lambda b, h, c: (c, 0)),
                  pl.BlockSpec((1, hp * dv), lambda b, h, c: (0, h))],
        out_specs=[pl.BlockSpec((chunk, hp * dv), lambda b, h, c: (b * nc + c, h)),
                   pl.BlockSpec((1, hp, dk, dv), lambda b, h, c: (b, h, 0, 0))],
        out_shape=[jax.ShapeDtypeStruct((rows_total, heads * dv), BF16),
                   jax.ShapeDtypeStruct((batch, heads, dk, dv), F32)],
        scratch_shapes=[pltpu.VMEM((hp, dk, dv), F32),
                        pltpu.VMEM((hp, chunk, chunk), F32),
                        pltpu.VMEM((hp, chunk, LANES), F32),
                        pltpu.VMEM((hp, chunk, LANES), F32)],
        compiler_params=_cparams("parallel", "parallel", "arbitrary"),
        name="retention_scan",
    )(log_gamma, proj, proj, proj, proj, cos, sin, gn_g.reshape(1, heads * dv))


def _one_hot_rows(nb, b):
    return jnp.where(lax.broadcasted_iota(I32, (nb, LANES), 0) == b, 1.0, 0.0).astype(BF16)


def _lane_tile(x, width):
    return jnp.concatenate([x] * (width // x.shape[1]), axis=1)


def _ret_step_kernel(lg_ref, q_ref, k_ref, v_ref, g_ref, cos_ref, sin_ref, gn_ref, s_ref, prev_ref,
                     o_ref, snew_ref, qt_ref, kt_ref, vf_ref, gf_ref, orow_ref, *, heads, dk, dv, nb, per_step):
    del prev_ref
    step = pl.program_id(0)
    half = dk // 2

    @pl.when(step == 0)
    def _():
        orow_ref[...] = jnp.zeros_like(orow_ref)
        vf_ref[...] = v_ref[...].astype(F32)
        gf_ref[...] = g_ref[...].astype(F32)
        cos = cos_ref[...]
        sin = sin_ref[...]
        for h in range(heads):
            qh = _rotary(q_ref[:, h * dk:(h + 1) * dk].astype(F32), cos, sin, half)
            kh = _rotary(k_ref[:, h * dk:(h + 1) * dk].astype(F32), cos, sin, half) * (dk ** -0.5)
            qt_ref[h] = qh.T.astype(BF16)
            kt_ref[h] = kh.T.astype(BF16)

    for i in range(per_step):
        b = step * per_step + i
        onehot = _one_hot_rows(nb, b)
        for h in range(heads):
            cols = slice(h * dv, (h + 1) * dv)
            kcol = _lane_tile(jnp.dot(kt_ref[h], onehot, preferred_element_type=F32), dv)
            qcol = _lane_tile(jnp.dot(qt_ref[h], onehot, preferred_element_type=F32), dv)
            vrow = _load_row(vf_ref, b, cols)
            gamma = jnp.exp(jnp.zeros((1, dv), F32) + lg_ref[h])
            s_new = s_ref[i, h] * gamma + kcol * vrow
            snew_ref[i, h] = s_new
            o = jnp.sum(s_new * qcol, axis=0, keepdims=True)
            _store_row(orow_ref, b, cols, _group_norm_gate(o, _load_row(gf_ref, b, cols), gn_ref[:, cols]))

    @pl.when(step == pl.num_programs(0) - 1)
    def _():
        o_ref[...] = orow_ref[...].astype(BF16)


def retention_step(proj, gated, state, log_gamma, cos, sin, gn_g, *, row0, nb, heads, dk, dv):
    per_step = RET_STEP_SEQS
    kern = functools.partial(_ret_step_kernel, heads=heads, dk=dk, dv=dv, nb=nb, per_step=per_step)
    rb = row0 // nb
    wq = heads * dk
    wv = heads * dv
    return pl.pallas_call(
        kern,
        grid=(nb // per_step,),
        in_specs=[pl.BlockSpec(memory_space=pltpu.SMEM),
                  pl.BlockSpec((nb, wq), lambda b: (rb, 0)),
                  pl.BlockSpec((nb, wq), lambda b: (rb, 1)),
                  pl.BlockSpec((nb, wv), lambda b: (rb, 2 * wq // wv)),
                  pl.BlockSpec((nb, wv), lambda b: (rb, 2 * wq // wv + 1)),
                  pl.BlockSpec((1, dk // 2), lambda b: (0, 0)),
                  pl.BlockSpec((1, dk // 2), lambda b: (0, 0)),
                  pl.BlockSpec((1, wv), lambda b: (0, 0)),
                  pl.BlockSpec((per_step, heads, dk, dv), lambda b: (b, 0, 0, 0)),
                  pl.BlockSpec(memory_space=pl.ANY)],
        out_specs=[pl.BlockSpec((nb, wv), lambda b: (rb, 0)),
                   pl.BlockSpec((per_step, heads, dk, dv), lambda b: (b, 0, 0, 0))],
        out_shape=[jax.ShapeDtypeStruct(gated.shape, gated.dtype),
                   jax.ShapeDtypeStruct(state.shape, F32)],
        scratch_shapes=[pltpu.VMEM((heads, dk, nb), BF16),
                        pltpu.VMEM((heads, dk, nb), BF16),
                        pltpu.VMEM((nb, wv), F32),
                        pltpu.VMEM((nb, wv), F32),
                        pltpu.VMEM((nb, wv), F32)],
        input_output_aliases={9: 0},
        compiler_params=_cparams("arbitrary"),
        name="retention_step",
    )(log_gamma, proj, proj, proj, proj, cos, sin, gn_g.reshape(1, wv), state, gated)


def _lower_bound(lbp, layer):
    m = jnp.max(lbp, axis=0, keepdims=True)
    e = jnp.exp(lbp - m)
    p = e / jnp.sum(e, axis=0, keepdims=True)
    return jnp.sum(p[:layer + 1], axis=0, keepdims=True) - p[0:1]


def _split_dot(mat_bf16, x):
    hi = x.astype(BF16)
    r1 = x - hi.astype(F32)
    mid = r1.astype(BF16)
    lo = (r1 - mid.astype(F32)).astype(BF16)
    return (jnp.dot(mat_bf16, hi, preferred_element_type=F32)
            + jnp.dot(mat_bf16, mid, preferred_element_type=F32)
            + jnp.dot(mat_bf16, lo, preferred_element_type=F32))


def _rms_gate(o, gate, gain):
    return _silu(gate) * (o * lax.rsqrt(jnp.mean(o * o, axis=-1, keepdims=True) + NORM_EPS) * gain)


def _group_row(x, s, group):
    n, w = x.shape
    x3 = x.reshape(n // group, group, w)
    return jnp.broadcast_to(x3[:, s:s + 1, :], x3.shape).reshape(n, w)


def _hgrn_scan_kernel(*refs, chunk, heads, dk, diag, layer, ncast):
    lbp_ref, q_ref, f_ref, i_ref, g_ref, ng_ref = refs[:6]
    cast_in = refs[6:6 + ncast]
    o_ref, sfin_ref = refs[6 + ncast:8 + ncast]
    cast_out = refs[8 + ncast:8 + 2 * ncast]
    st_ref = refs[8 + 2 * ncast]
    c = pl.program_id(1)

    @pl.when(c == 0)
    def _():
        st_ref[...] = jnp.zeros_like(st_ref)

    for src, dst in zip(cast_in, cast_out):
        dst[...] = src[...].astype(BF16)

    w = heads * dk
    lb = _lower_bound(lbp_ref[...], layer)
    qs = _silu(q_ref[...].astype(F32))
    forget = lb + (1.0 - lb) * _sigmoid(f_ref[...].astype(F32))
    kk = 1.0 - forget
    logf = jnp.log(forget)

    row = lax.broadcasted_iota(I32, (chunk, chunk), 0)
    col = lax.broadcasted_iota(I32, (chunk, chunk), 1)
    tril = jnp.where(row >= col, 1.0, 0.0).astype(BF16)
    bcum = _split_dot(tril, logf)
    blast = bcum[chunk - 1:chunk, :]
    qe = (qs * jnp.exp(bcum)).astype(BF16)
    kdec = (kk * jnp.exp(blast - bcum)).astype(BF16)
    lk = jnp.log(kk) - bcum
    iv = i_ref[...]
    gate = g_ref[...].astype(F32)
    gain = ng_ref[...]

    u = jnp.where(row > col, row ^ col, 0)
    rowv = lax.broadcasted_iota(I32, (chunk, 1), 0)
    levels = []
    m = chunk // 2
    while m >= diag:
        span = 2 * m
        bref = jnp.concatenate(
            [jnp.broadcast_to(bcum[g * span + m - 1:g * span + m], (span, w)) for g in range(chunk // span)],
            axis=0)
        upper = (rowv & m) != 0
        d = bcum - bref
        x = (jnp.where(upper, qs, kk) * jnp.exp(jnp.where(upper, d, -d))).astype(BF16)
        levels.append((x, lax.shift_right_logical(u, m.bit_length() - 1) == 1))
        m //= 2
    in_block = row & (diag - 1)
    block_col0 = row - in_block
    diag_masks = [(col == block_col0 + s) & (in_block >= s) for s in range(diag)]
    ones = jnp.ones((dk, LANES), BF16)

    for h in range(heads):
        cols = slice(h * dk, (h + 1) * dk)
        qs_h = qs[:, cols]
        lk_h = lk[:, cols]
        b_h = bcum[:, cols]
        a = jnp.zeros((chunk, chunk), F32)
        for x, mask in levels:
            x_h = x[:, cols]
            a = jnp.where(mask, lax.dot_general(x_h, x_h, NT_DIMS, preferred_element_type=F32), a)
        ws = [qs_h * jnp.exp(b_h + _group_row(lk_h, s, diag)) for s in range(diag)]
        rsum = jnp.dot(jnp.concatenate(ws, axis=0).astype(BF16), ones, preferred_element_type=F32)
        for s in range(diag):
            a = jnp.where(diag_masks[s], rsum[s * chunk:(s + 1) * chunk], a)
        st = st_ref[h]
        i_h = iv[:, cols]
        o_h = (jnp.dot(a.astype(BF16), i_h, preferred_element_type=F32)
               + lax.dot_general(qe[:, cols], st.astype(BF16), NT_DIMS, preferred_element_type=F32))
        st_new = (st * jnp.exp(blast[:, cols])
                  + lax.dot_general(i_h, kdec[:, cols], TN_DIMS, preferred_element_type=F32))
        st_ref[h] = st_new
        o_ref[:, cols] = _rms_gate(o_h, gate[:, cols], gain[:, cols]).astype(BF16)

    @pl.when(c == pl.num_programs(1) - 1)
    def _():
        for h in range(heads):
            sfin_ref[0, h] = st_ref[h].T


def hgrn_scan(proj, lb_param, norm_g, side_f32, *, batch, seq, heads, dk, rows_total, layer):
    chunk = min(HG_CHUNK, seq)
    assert chunk == LANES and dk == LANES
    nc = seq // chunk
    w = heads * dk
    steps = batch * nc
    slabs = []
    for a in side_f32:
        assert a.shape[0] % (steps * 16) == 0, (a.shape, steps)
        slabs.append(pl.BlockSpec((a.shape[0] // steps, a.shape[1]), lambda b, c: (b * nc + c, 0)))
    kern = functools.partial(_hgrn_scan_kernel, chunk=chunk, heads=heads, dk=dk, diag=HG_DIAG, layer=layer,
                             ncast=len(side_f32))
    return pl.pallas_call(
        kern,
        grid=(batch, nc),
        in_specs=[pl.BlockSpec(lb_param.shape, lambda b, c: (0, 0)),
                  pl.BlockSpec((chunk, w), lambda b, c: (b * nc + c, 0)),
                  pl.BlockSpec((chunk, w), lambda b, c: (b * nc + c, 1)),
                  pl.BlockSpec((chunk, w), lambda b, c: (b * nc + c, 2)),
                  pl.BlockSpec((chunk, w), lambda b, c: (b * nc + c, 3)),
                  pl.BlockSpec((1, w), lambda b, c: (0, 0))] + slabs,
        out_specs=[pl.BlockSpec((chunk, w), lambda b, c: (b * nc + c, 0)),
                   pl.BlockSpec((1, heads, dk, dk), lambda b, c: (b, 0, 0, 0))] + slabs,
        out_shape=[jax.ShapeDtypeStruct((rows_total, w), BF16),
                   jax.ShapeDtypeStruct((batch, heads, dk, dk), F32)]
                  + [jax.ShapeDtypeStruct(a.shape, BF16) for a in side_f32],
        scratch_shapes=[pltpu.VMEM((heads, dk, dk), F32)],
        compiler_params=_cparams("parallel", "arbitrary"),
        name="hgrn_scan",
    )(lb_param, proj, proj, proj, proj, norm_g.reshape(1, w), *side_f32)


def _hgrn_step_kernel(lbp_ref, q_ref, f_ref, i_ref, g_ref, ng_ref, s_ref, prev_ref,
                      o_ref, snew_ref, qt_ref, ft_ref, if_ref, gf_ref, orow_ref,
                      *, heads, dk, nb, layer, per_step):
    del prev_ref
    step = pl.program_id(0)

    @pl.when(step == 0)
    def _():
        orow_ref[...] = jnp.zeros_like(orow_ref)
        if_ref[...] = i_ref[...].astype(F32)
        gf_ref[...] = g_ref[...].astype(F32)
        lb = _lower_bound(lbp_ref[...], layer)
        qs = _silu(q_ref[...].astype(F32))
        forget = lb + (1.0 - lb) * _sigmoid(f_ref[...].astype(F32))
        for h in range(heads):
            cols = slice(h * dk, (h + 1) * dk)
            qt_ref[h] = qs[:, cols].T.astype(BF16)
            ft = forget[:, cols].T
            hi = ft.astype(BF16)
            r1 = ft - hi.astype(F32)
            mid = r1.astype(BF16)
            ft_ref[0, h] = hi
            ft_ref[1, h] = mid
            ft_ref[2, h] = (r1 - mid.astype(F32)).astype(BF16)

    for i in range(per_step):
        b = step * per_step + i
        onehot = _one_hot_rows(nb, b)
        for h in range(heads):
            cols = slice(h * dk, (h + 1) * dk)
            fcol = (jnp.dot(ft_ref[0, h], onehot, preferred_element_type=F32)
                    + jnp.dot(ft_ref[1, h], onehot, preferred_element_type=F32)
                    + jnp.dot(ft_ref[2, h], onehot, preferred_element_type=F32))
            qcol = jnp.dot(qt_ref[h], onehot, preferred_element_type=F32)
            irow = _load_row(if_ref, b, cols)
            s_new = s_ref[i, h] * fcol + (1.0 - fcol) * irow
            snew_ref[i, h] = s_new
            o = jnp.sum(s_new * qcol, axis=0, keepdims=True)
            _store_row(orow_ref, b, cols, _rms_gate(o, _load_row(gf_ref, b, cols), ng_ref[:, cols]))

    @pl.when(step == pl.num_programs(0) - 1)
    def _():
        o_ref[...] = orow_ref[...].astype(BF16)


def hgrn_step(proj, gated, state, lb_param, norm_g, *, row0, nb, heads, dk, layer):
    per_step = HG_STEP_SEQS
    kern = functools.partial(_hgrn_step_kernel, heads=heads, dk=dk, nb=nb, layer=layer, per_step=per_step)
    rb = row0 // nb
    w = heads * dk
    return pl.pallas_call(
        kern,
        grid=(nb // per_step,),
        in_specs=[pl.BlockSpec(lb_param.shape, lambda b: (0, 0)),
                  pl.BlockSpec((nb, w), lambda b: (rb, 0)),
                  pl.BlockSpec((nb, w), lambda b: (rb, 1)),
                  pl.BlockSpec((nb, w), lambda b: (rb, 2)),
                  pl.BlockSpec((nb, w), lambda b: (rb, 3)),
                  pl.BlockSpec((1, w), lambda b: (0, 0)),
                  pl.BlockSpec((per_step, heads, dk, dk), lambda b: (b, 0, 0, 0)),
                  pl.BlockSpec(memory_space=pl.ANY)],
        out_specs=[pl.BlockSpec((nb, w), lambda b: (rb, 0)),
                   pl.BlockSpec((per_step, heads, dk, dk), lambda b: (b, 0, 0, 0))],
        out_shape=[jax.ShapeDtypeStruct(gated.shape, gated.dtype),
                   jax.ShapeDtypeStruct(state.shape, F32)],
        scratch_shapes=[pltpu.VMEM((heads, dk, nb), BF16),
                        pltpu.VMEM((3, heads, dk, nb), BF16),
                        pltpu.VMEM((nb, w), F32),
                        pltpu.VMEM((nb, w), F32),
                        pltpu.VMEM((nb, w), F32)],
        input_output_aliases={7: 0},
        compiler_params=_cparams("arbitrary"),
        name="hgrn_step",
    )(lb_param, proj, proj, proj, proj, norm_g.reshape(1, w), state, gated)


def _router_kernel(a_ref, w_ref, x_ref, g_ref, r_ref, xo_ref, h_ref, route_ref, cnt_ref,
                   carry_ref, rhi_ref, rlo_ref, *, tm, experts):
    i = pl.program_id(0)

    @pl.when(i == 0)
    def _():
        carry_ref[...] = jnp.zeros_like(carry_ref)
        r = r_ref[...]
        rhi = r.astype(BF16)
        rhi_ref[...] = rhi
        rlo_ref[...] = (r - rhi.astype(F32)).astype(BF16)

    step = (tm // ROUTER_GROUPS + 15) // 16 * 16
    groups = tuple((r, min(r + step, tm)) for r in range(0, tm, step))
    all_logits = []
    for r0, r1 in groups:
        rows = slice(r0, r1)
        x = x_ref[rows] + jnp.dot(a_ref[rows], w_ref[...], preferred_element_type=F32)
        xo_ref[rows] = x
        h = _rms(x, g_ref[...])
        h_ref[rows] = h
        hhi = h.astype(BF16)
        hlo = (h - hhi.astype(F32)).astype(BF16)
        all_logits.append(jnp.dot(hhi, rhi_ref[...], preferred_element_type=F32)
                          + jnp.dot(hlo, rhi_ref[...], preferred_element_type=F32)
                          + jnp.dot(hhi, rlo_ref[...], preferred_element_type=F32))
    total = carry_ref[...]
    for (r0, r1), logits in zip(groups, all_logits):
        n = r1 - r0
        rows = slice(r0, r1)
        lane = lax.broadcasted_iota(I32, (n, LANES), 1)
        valid = lane < experts
        z = jnp.where(valid, logits, -jnp.inf)
        ez = jnp.exp(z - jnp.max(z, axis=-1, keepdims=True))
        p = ez / jnp.sum(ez, axis=-1, keepdims=True)
        p = jnp.where(valid, p, -1.0)
        lane_f = lane.astype(F32)
        v1 = jnp.max(p, axis=-1, keepdims=True)
        i1 = jnp.min(jnp.where(p == v1, lane_f, float(LANES)), axis=-1, keepdims=True)
        p2 = jnp.where(lane_f == i1, -1.0, p)
        v2 = jnp.max(p2, axis=-1, keepdims=True)
        i2 = jnp.min(jnp.where(p2 == v2, lane_f, float(LANES)), axis=-1, keepdims=True)
        den = v1 + v2
        m0 = jnp.where(lane_f == i1, 1.0, 0.0)
        m1 = jnp.where(lane_f == i2, 1.0, 0.0)
        msum = m0 + m1
        ri = lax.broadcasted_iota(I32, (n, n), 0)
        ci = lax.broadcasted_iota(I32, (n, n), 1)
        strict = jnp.where(ri > ci, 1.0, 0.0).astype(BF16)
        before = jnp.dot(strict, msum.astype(BF16), preferred_element_type=F32) + total
        rank0 = jnp.sum(m0 * before, axis=-1, keepdims=True)
        rank1 = jnp.sum(m1 * before, axis=-1, keepdims=True)
        total = total + jnp.sum(msum, axis=0, keepdims=True)
        route_ref[rows] = jnp.where(lane == 0, i1,
                          jnp.where(lane == 1, i2,
                          jnp.where(lane == 2, v1 / den,
                          jnp.where(lane == 3, v2 / den,
                          jnp.where(lane == 4, rank0,
                          jnp.where(lane == 5, rank1, 0.0))))))
    carry_ref[...] = total
    cnt_ref[...] = jnp.broadcast_to(total, cnt_ref.shape)


def outproj_router(a, w, x, g, router_w, tm):
    t, d = x.shape
    k = a.shape[1]
    experts = router_w.shape[1]
    rpad = jnp.zeros((d, LANES), F32).at[:, :experts].set(router_w)
    kern = functools.partial(_router_kernel, tm=tm, experts=experts)
    return pl.pallas_call(
        kern,
        grid=(t // tm,),
        in_specs=[pl.BlockSpec((tm, k), lambda i: (i, 0)),
                  pl.BlockSpec((k, d), lambda i: (0, 0)),
                  pl.BlockSpec((tm, d), lambda i: (i, 0)),
                  pl.BlockSpec((1, d), lambda i: (0, 0)),
                  pl.BlockSpec((d, LANES), lambda i: (0, 0))],
        out_specs=[pl.BlockSpec((tm, d), lambda i: (i, 0)),
                   pl.BlockSpec((tm, d), lambda i: (i, 0)),
                   pl.BlockSpec((tm, LANES), lambda i: (i, 0)),
                   pl.BlockSpec((8, LANES), lambda i: (0, 0))],
        out_shape=[jax.ShapeDtypeStruct((t, d), F32),
                   jax.ShapeDtypeStruct((t, d), F32),
                   jax.ShapeDtypeStruct((t, LANES), F32),
                   jax.ShapeDtypeStruct((8, LANES), F32)],
        scratch_shapes=[pltpu.VMEM((1, LANES), F32),
                        pltpu.VMEM((d, LANES), BF16),
                        pltpu.VMEM((d, LANES), BF16)],
        compiler_params=_cparams("arbitrary"),
        name="outproj_router",
    )(a, w, x, g.reshape(1, d), rpad)


def _plan_kernel(cnt_ref, route_ref, frow_ref, fexp_ref, nfull_ref, trow_ref, tsub_ref, zrow_ref, pos_ref,
                 *, experts, nfull_max, dummy, row_tile, sub):
    tile0 = jnp.int32(0)
    nfull = jnp.int32(0)
    last = jnp.int32(0)
    route = route_ref[...]
    slot_expert = pltpu.roll(route, 4, axis=1)
    first_row = jnp.zeros_like(route)
    for e in range(experts):
        n = cnt_ref[e]
        first_row = jnp.where(slot_expert == float(e), (tile0 * row_tile).astype(F32), first_row)
        full = n // row_tile
        rem = n - full * row_tile

        def fill(j, carry, e=e, tile0=tile0, nfull=nfull):
            frow_ref[nfull + j] = tile0 + j
            fexp_ref[nfull + j] = e
            return carry

        lax.fori_loop(0, full, fill, 0)
        trow_ref[e] = jnp.where(rem > 0, tile0 + full, dummy)
        tsub_ref[e] = (rem + (sub - 1)) // sub
        zrow_ref[e] = tile0 * row_tile + (n // sub) * sub
        last = jnp.where(full > 0, e, last)
        nfull = nfull + full
        tile0 = tile0 + full + jnp.where(rem > 0, 1, 0)
    nfull_ref[0] = nfull
    pos_ref[...] = (first_row + route).astype(I32)

    def unused(j, carry):
        frow_ref[j] = dummy
        fexp_ref[j] = last
        return carry

    lax.fori_loop(nfull, nfull_max, unused, 0)


def plan(counts, route, ntiles, nfull_max):
    experts = counts.shape[0]
    kern = functools.partial(_plan_kernel, experts=experts, nfull_max=nfull_max, dummy=ntiles - 1,
                             row_tile=MOE_ROW_TILE, sub=MOE_SUB)
    smem = pl.BlockSpec(memory_space=pltpu.SMEM)
    vmem = pl.BlockSpec(memory_space=pltpu.VMEM)
    return pl.pallas_call(
        kern,
        in_specs=[smem, vmem],
        out_specs=[smem] * 6 + [vmem],
        out_shape=[jax.ShapeDtypeStruct((nfull_max,), I32),
                   jax.ShapeDtypeStruct((nfull_max,), I32),
                   jax.ShapeDtypeStruct((1,), I32),
                   jax.ShapeDtypeStruct((experts,), I32),
                   jax.ShapeDtypeStruct((experts,), I32),
                   jax.ShapeDtypeStruct((experts,), I32),
                   jax.ShapeDtypeStruct(route.shape, I32)],
        compiler_params=pltpu.CompilerParams(vmem_limit_bytes=VMEM_LIMIT),
        name="moe_plan",
    )(counts, route)


def _dispatch_kernel(zrow_ref, p_ref, h_ref, out_hbm, zero_ref, sem, zsem, *, tm, experts):
    @pl.when(pl.program_id(0) == 0)
    def _():
        zero_ref[...] = jnp.zeros_like(zero_ref)

        def zero_copy(e):
            row = pl.multiple_of(zrow_ref[e], MOE_SUB)
            return pltpu.make_async_copy(zero_ref, out_hbm.at[pl.ds(row, MOE_SUB)], zsem)

        for e in range(experts):
            zero_copy(e).start()
        for e in range(experts):
            zero_copy(e).wait()

    def row_copy(t, dst):
        return pltpu.make_async_copy(h_ref.at[pl.ds(t, 1)], out_hbm.at[pl.ds(dst, 1)], sem)

    def issue(t, carry):
        for k in range(TOP_K):
            row_copy(t, p_ref[0, 0, TOP_K * t + k]).start()
        return carry

    lax.fori_loop(0, tm, issue, 0, unroll=DMA_UNROLL)

    def drain(t, carry):
        for k in range(TOP_K):
            row_copy(0, 0).wait()
        return carry

    lax.fori_loop(0, tm, drain, 0, unroll=DMA_UNROLL)


def dispatch(h, pos, zrow, rows, tm):
    t, d = h.shape
    nt = t // tm
    kern = functools.partial(_dispatch_kernel, tm=tm, experts=zrow.shape[0])
    return pl.pallas_call(
        kern,
        grid_spec=pltpu.PrefetchScalarGridSpec(
            num_scalar_prefetch=1,
            grid=(nt,),
            in_specs=[pl.BlockSpec((1, 1, TOP_K * tm), lambda i, zrow: (i, 0, 0), memory_space=pltpu.SMEM),
                      pl.BlockSpec((tm, d), lambda i, zrow: (i, 0))],
            out_specs=pl.BlockSpec(memory_space=pl.ANY),
            scratch_shapes=[pltpu.VMEM((MOE_SUB, d), h.dtype),
                            pltpu.SemaphoreType.DMA(()),
                            pltpu.SemaphoreType.DMA(())]),
        out_shape=jax.ShapeDtypeStruct((rows, d), h.dtype),
        compiler_params=_cparams("arbitrary"),
        name="moe_dispatch",
    )(zrow, pos.reshape(nt, 1, TOP_K * tm), h)


def _expert_full_kernel(nfull_ref, frow_ref, fexp_ref, x_ref, wg_ref, wu_ref, wd_ref, o_ref, h_ref):
    del frow_ref, fexp_ref
    j = pl.program_id(0)
    f = pl.program_id(1)
    active = j < nfull_ref[0]

    @pl.when(f == 0)
    def _():
        o_ref[...] = jnp.zeros_like(o_ref)

    @pl.when(active & (f == 0))
    def _():
        h_ref[...] = x_ref[...].astype(BF16)

    @pl.when(active)
    def _():
        h = h_ref[...]
        a = jnp.dot(h, wg_ref[0], preferred_element_type=F32)
        u = jnp.dot(h, wu_ref[0], preferred_element_type=F32)
        act = (_silu(a) * u).astype(BF16)
        o_ref[...] += jnp.dot(act, wd_ref[0], preferred_element_type=F32)


def _expert_tail_kernel(trow_ref, tsub_ref, x_ref, wg_ref, wu_ref, wd_ref, prev_ref, o_ref, h_ref,
                        *, sub, nsub_max):
    del trow_ref, prev_ref
    f = pl.program_id(1)
    nsub = tsub_ref[pl.program_id(0)]

    @pl.when(f == 0)
    def _():
        o_ref[...] = jnp.zeros_like(o_ref)

    for sb in range(nsub_max):
        @pl.when(sb < nsub)
        def _():
            rows = slice(sb * sub, (sb + 1) * sub)

            @pl.when(f == 0)
            def _():
                h_ref[rows] = x_ref[rows].astype(BF16)

            h = h_ref[rows]
            a = jnp.dot(h, wg_ref[0], preferred_element_type=F32)
            u = jnp.dot(h, wu_ref[0], preferred_element_type=F32)
            act = (_silu(a) * u).astype(BF16)
            o_ref[rows] += jnp.dot(act, wd_ref[0], preferred_element_type=F32)


def expert_ffn(xs, nfull, frow, fexp, trow, tsub, wg, wu, wd):
    rows, d = xs.shape
    experts, _, fdim = wg.shape
    tf = MOE_F_TILE
    nf = fdim // tf
    nfull_max = frow.shape[0]

    def fcol_full(j, f, nfull):
        return jnp.where(j < nfull[0], f, nf - 1)

    ys = pl.pallas_call(
        _expert_full_kernel,
        grid_spec=pltpu.PrefetchScalarGridSpec(
            num_scalar_prefetch=3,
            grid=(nfull_max, nf),
            in_specs=[pl.BlockSpec((MOE_ROW_TILE, d), lambda j, f, nfull, frow, fexp: (frow[j], 0)),
                      pl.BlockSpec((1, d, tf), lambda j, f, nfull, frow, fexp: (fexp[j], 0, fcol_full(j, f, nfull))),
                      pl.BlockSpec((1, d, tf), lambda j, f, nfull, frow, fexp: (fexp[j], 0, fcol_full(j, f, nfull))),
                      pl.BlockSpec((1, tf, d), lambda j, f, nfull, frow, fexp: (fexp[j], fcol_full(j, f, nfull), 0))],
            out_specs=pl.BlockSpec((MOE_ROW_TILE, d), lambda j, f, nfull, frow, fexp: (frow[j], 0)),
            scratch_shapes=[pltpu.VMEM((MOE_ROW_TILE, d), BF16)]),
        out_shape=jax.ShapeDtypeStruct((rows, d), F32),
        compiler_params=_cparams("arbitrary", "arbitrary"),
        name="expert_ffn_full",
    )(nfull, frow, fexp, xs, wg, wu, wd)

    def fcol_tail(e, f, tsub):
        return jnp.where(tsub[e] > 0, f, nf - 1)

    kern = functools.partial(_expert_tail_kernel, sub=MOE_SUB, nsub_max=MOE_ROW_TILE // MOE_SUB)
    return pl.pallas_call(
        kern,
        grid_spec=pltpu.PrefetchScalarGridSpec(
            num_scalar_prefetch=2,
            grid=(experts, nf),
            in_specs=[pl.BlockSpec((MOE_ROW_TILE, d), lambda e, f, trow, tsub: (trow[e], 0)),
                      pl.BlockSpec((1, d, tf), lambda e, f, trow, tsub: (e, 0, fcol_tail(e, f, tsub))),
                      pl.BlockSpec((1, d, tf), lambda e, f, trow, tsub: (e, 0, fcol_tail(e, f, tsub))),
                      pl.BlockSpec((1, tf, d), lambda e, f, trow, tsub: (e, fcol_tail(e, f, tsub), 0)),
                      pl.BlockSpec(memory_space=pl.ANY)],
            out_specs=pl.BlockSpec((MOE_ROW_TILE, d), lambda e, f, trow, tsub: (trow[e], 0)),
            scratch_shapes=[pltpu.VMEM((MOE_ROW_TILE, d), BF16)]),
        out_shape=jax.ShapeDtypeStruct((rows, d), F32),
        input_output_aliases={6: 0},
        compiler_params=_cparams("arbitrary", "arbitrary"),
        name="expert_ffn_tail",
    )(trow, tsub, xs, wg, wu, wd, ys)


def _combine_kernel(p_ref, x_ref, route_ref, fg_ref, ys_hbm, op_ref, os_ref, gath_ref, sem, *, tm):
    def row_copy(src, k, t):
        return pltpu.make_async_copy(ys_hbm.at[pl.ds(src, 1)], gath_ref.at[k, pl.ds(t, 1)], sem)

    def issue(t, carry):
        for k in range(TOP_K):
            row_copy(p_ref[0, 0, TOP_K * t + k], k, t).start()
        return carry

    lax.fori_loop(0, tm, issue, 0, unroll=DMA_UNROLL)

    def drain(t, carry):
        for k in range(TOP_K):
            row_copy(0, 0, 0).wait()
        return carry

    lax.fori_loop(0, tm, drain, 0, unroll=DMA_UNROLL)
    route = route_ref[...]
    x = x_ref[...] + route[:, 2:3] * gath_ref[0] + route[:, 3:4] * gath_ref[1]
    y = _rms(x, fg_ref[...])
    op_ref[...] = y

    @pl.when(pl.program_id(0) == pl.num_programs(0) - 1)
    def _():
        os_ref[...] = y[tm - os_ref.shape[0]:]


def combine(x, route, ys, pos, final_g, tm, n_sample):
    t, d = x.shape
    nt = t // tm
    assert n_sample <= tm
    kern = functools.partial(_combine_kernel, tm=tm)
    return pl.pallas_call(
        kern,
        grid=(nt,),
        in_specs=[pl.BlockSpec((1, 1, TOP_K * tm), lambda i: (i, 0, 0), memory_space=pltpu.SMEM),
                  pl.BlockSpec((tm, d), lambda i: (i, 0)),
                  pl.BlockSpec((tm, LANES), lambda i: (i, 0)),
                  pl.BlockSpec((1, d), lambda i: (0, 0)),
                  pl.BlockSpec(memory_space=pl.ANY)],
        out_specs=[pl.BlockSpec((tm, d), lambda i: (i, 0)),
                   pl.BlockSpec((n_sample, d), lambda i: (0, 0))],
        out_shape=[jax.ShapeDtypeStruct((t - n_sample, d), F32),
                   jax.ShapeDtypeStruct((n_sample, d), F32)],
        scratch_shapes=[pltpu.VMEM((TOP_K, tm, d), F32),
                        pltpu.SemaphoreType.DMA(())],
        compiler_params=_cparams("arbitrary"),
        name="moe_combine",
    )(pos.reshape(nt, 1, TOP_K * tm), x, route, final_g.reshape(1, d), ys)


def moe_residual_final_norm(a, w_out, x_in, norm_g, router_w, wg, wu, wd, final_g, tm, n_sample):
    t, d = x_in.shape
    experts = router_w.shape[1]
    x, h, route, cnt = outproj_router(a, w_out, x_in, norm_g, router_w, tm)
    counts = cnt[0, :experts].astype(I32)
    ntiles = (TOP_K * t + experts * (MOE_ROW_TILE - 1)) // MOE_ROW_TILE + 1
    nfull_max = max(TOP_K * t // MOE_ROW_TILE, 1)
    frow, fexp, nfull, trow, tsub, zrow, pos = plan(counts, route, ntiles, nfull_max)
    pos = pos[:, 4:4 + TOP_K]
    xs = dispatch(h, pos, zrow, ntiles * MOE_ROW_TILE, tm)
    ys = expert_ffn(xs, nfull, frow, fexp, trow, tsub, wg, wu, wd)
    return combine(x, route, ys, pos, final_g, tm, n_sample)


def _rope_tables(pos, half):
    inv = jnp.power(ROPE_THETA, -jnp.arange(half, dtype=F32) / half)
    ang = pos[:, None] * inv[None, :]
    return jnp.cos(ang), jnp.sin(ang)


def kernel(x_prompt, x_sample, state_retention, state_hgrn, norm_mix_g, norm_ffn_g, final_norm_g, ret_w_in, ret_gn_g, ret_w_out, hg_w_in, hg_lb_param, hg_norm_g, hg_w_out, ffn_w_gate, ffn_w_up, ffn_w_down, moe_router, moe_w_gate, moe_w_up, moe_w_down):
    bp, lp, d = x_prompt.shape
    bs, ls, _ = x_sample.shape
    assert ls == 1 and norm_mix_g.shape[0] == 2
    _, _, ret_heads, ret_dk, ret_dv = state_retention.shape
    _, _, hg_heads, hg_dk, _ = state_hgrn.shape
    tp = bp * lp
    t = tp + bs
    tm = _pick_tile(t, 768, 16)

    xp = x_prompt.reshape(tp, d)
    xs = x_sample.reshape(bs, d)

    log_gamma = jnp.log1p(-jnp.exp2(-5.0 - jnp.arange(ret_heads, dtype=F32)))
    cos_p, sin_p = _rope_tables(jnp.arange(lp, dtype=F32), ret_dk // 2)
    cos_s, sin_s = _rope_tables(PAST_LEN + jnp.arange(ls, dtype=F32), ret_dk // 2)
    tm_proj = _pick_tile(t, 1536, 16)
    proj = norm_matmul_merged(xp, xs, norm_mix_g[0], ret_w_in[0].astype(BF16), tm_proj,
                              _pick_tile(ret_w_in.shape[2], 1024, LANES))
    gated, ret_p = retention_scan(proj, log_gamma, cos_p, sin_p, ret_gn_g[0], batch=bp, seq=lp,
                                  heads=ret_heads, dk=ret_dk, dv=ret_dv, rows_total=t)
    gated, ret_s = retention_step(proj, gated, state_retention[0], log_gamma, cos_s, sin_s, ret_gn_g[0],
                                  row0=tp, nb=bs, heads=ret_heads, dk=ret_dk, dv=ret_dv)
    x = matmul_residual_merged(gated, ret_w_out[0].astype(BF16), xp, xs, tm)
    ff = ffn_w_gate.shape[2]
    x = ffn_residual(x, norm_ffn_g[0], ffn_w_gate[0].astype(BF16), ffn_w_up[0].astype(BF16),
                     ffn_w_down[0].astype(BF16), tm, _pick_tile(ff, 1536, LANES))

    proj = norm_matmul(x, norm_mix_g[1], hg_w_in[0].astype(BF16), tm_proj,
                       _pick_tile(hg_w_in.shape[2], 1024, LANES))
    experts, _, fe = moe_w_gate.shape[1:]
    gated, hg_p, wg, wu, wd = hgrn_scan(
        proj, hg_lb_param, hg_norm_g[0],
        (moe_w_gate[0].reshape(experts * d, fe), moe_w_up[0].reshape(experts * d, fe),
         moe_w_down[0].reshape(experts * fe, d)),
        batch=bp, seq=lp, heads=hg_heads, dk=hg_dk, rows_total=t, layer=1)
    gated, hg_s = hgrn_step(proj, gated, state_hgrn[0], hg_lb_param, hg_norm_g[0], row0=tp, nb=bs,
                            heads=hg_heads, dk=hg_dk, layer=1)
    y_p, y_s = moe_residual_final_norm(gated, hg_w_out[0].astype(BF16), x, norm_ffn_g[1], moe_router[0],
                                       wg.reshape(experts, d, fe), wu.reshape(experts, d, fe),
                                       wd.reshape(experts, fe, d), final_norm_g, tm, bs)

    return (y_p.reshape(bp, lp, d), y_s.reshape(bs, ls, d),
            ret_p[None], ret_s[None], hg_p[None], hg_s[None])
```

```python
import functools

import jax
import jax.numpy as jnp
from jax import lax
from jax.experimental import pallas as pl
from jax.experimental.pallas import tpu as pltpu

F32 = jnp.float32
BF16 = jnp.bfloat16
I32 = jnp.int32

NORM_EPS = 1e-6
GN_EPS = 1e-5
ROPE_THETA = 10000.0
PAST_LEN = 16384
TOP_K = 2

LANES = 128
VMEM_LIMIT = 56 * 1024 * 1024

RET_CHUNK = 256
HG_CHUNK = 128
HG_DIAG = 4
RET_SCAN_HEADS = 2
RET_STEP_SEQS = 2
HG_STEP_SEQS = 4
MOE_ROW_TILE = 1024
MOE_SUB = 256
MOE_F_TILE = 512
ROUTER_GROUPS = 2
DMA_UNROLL = 8

NT_DIMS = (((1,), (1,)), ((), ()))
TN_DIMS = (((0,), (0,)), ((), ()))


def _cparams(*sem):
    return pltpu.CompilerParams(dimension_semantics=sem, vmem_limit_bytes=VMEM_LIMIT)


def _pick_tile(n, target, mult):
    best = None
    for t in range(mult, min(n, target) + 1, mult):
        if n % t == 0:
            best = t
    assert best is not None, (n, target, mult)
    return best


def _sigmoid(x):
    return 0.5 * jnp.tanh(0.5 * x) + 0.5


def _silu(x):
    h = 0.5 * x
    return h * jnp.tanh(h) + h


def _rms(x, g):
    return x * lax.rsqrt(jnp.mean(x * x, axis=-1, keepdims=True) + NORM_EPS) * g


def _row_group(b):
    return pl.multiple_of(lax.shift_left(lax.shift_right_logical(b, 3), 3), 8), b & 7


def _load_row(ref, b, cols):
    base, r = _row_group(b)
    blk = ref[pl.ds(base, 8), cols]
    rows = lax.broadcasted_iota(I32, blk.shape, 0)
    return jnp.sum(jnp.where(rows == r, blk, 0.0), axis=0, keepdims=True)


def _store_row(ref, b, cols, row):
    base, r = _row_group(b)
    blk = ref[pl.ds(base, 8), cols]
    rows = lax.broadcasted_iota(I32, blk.shape, 0)
    ref[pl.ds(base, 8), cols] = jnp.where(rows == r, row, blk)


def _norm_matmul_kernel(x_ref, g_ref, w_ref, o_ref, h_ref):
    @pl.when(pl.program_id(1) == 0)
    def _():
        h_ref[...] = _rms(x_ref[...], g_ref[...]).astype(BF16)

    o_ref[...] = jnp.dot(h_ref[...], w_ref[...], preferred_element_type=F32).astype(o_ref.dtype)


def norm_matmul(x, g, w, tm, tn):
    t, d = x.shape
    n = w.shape[1]
    return pl.pallas_call(
        _norm_matmul_kernel,
        grid=(t // tm, n // tn),
        in_specs=[pl.BlockSpec((tm, d), lambda i, j: (i, 0)),
                  pl.BlockSpec((1, d), lambda i, j: (0, 0)),
                  pl.BlockSpec((d, tn), lambda i, j: (0, j))],
        out_specs=pl.BlockSpec((tm, tn), lambda i, j: (i, j)),
        out_shape=jax.ShapeDtypeStruct((t, n), BF16),
        scratch_shapes=[pltpu.VMEM((tm, d), BF16)],
        compiler_params=_cparams("parallel", "arbitrary"),
        name="norm_matmul",
    )(x, g.reshape(1, d), w)


def _with_merged_rows(xp_ref, xs_ref, fn):
    i = pl.program_id(0)
    last = pl.num_programs(0) - 1
    n_prompt = xp_ref.shape[0] - xs_ref.shape[0]

    @pl.when(i < last)
    def _():
        fn(xp_ref[...])

    @pl.when(i == last)
    def _():
        fn(jnp.concatenate([xp_ref[:n_prompt], xs_ref[...]], axis=0))


def _merged_specs(tm, d, bs, nidx):
    if nidx == 1:
        return [pl.BlockSpec((tm, d), lambda i: (i, 0)), pl.BlockSpec((bs, d), lambda i: (0, 0))]
    return [pl.BlockSpec((tm, d), lambda i, j: (i, 0)), pl.BlockSpec((bs, d), lambda i, j: (0, 0))]


def _norm_matmul2_kernel(xp_ref, xs_ref, g_ref, w_ref, o_ref, h_ref):
    @pl.when(pl.program_id(1) == 0)
    def _():
        def fill(x):
            h_ref[...] = _rms(x, g_ref[...]).astype(BF16)

        _with_merged_rows(xp_ref, xs_ref, fill)

    o_ref[...] = jnp.dot(h_ref[...], w_ref[...], preferred_element_type=F32).astype(o_ref.dtype)


def norm_matmul_merged(xp, xs, g, w, tm, tn):
    tp, d = xp.shape
    bs = xs.shape[0]
    t = tp + bs
    n = w.shape[1]
    assert t % tm == 0 and bs <= tm
    return pl.pallas_call(
        _norm_matmul2_kernel,
        grid=(t // tm, n // tn),
        in_specs=_merged_specs(tm, d, bs, 2) + [pl.BlockSpec((1, d), lambda i, j: (0, 0)),
                                                 pl.BlockSpec((d, tn), lambda i, j: (0, j))],
        out_specs=pl.BlockSpec((tm, tn), lambda i, j: (i, j)),
        out_shape=jax.ShapeDtypeStruct((t, n), BF16),
        scratch_shapes=[pltpu.VMEM((tm, d), BF16)],
        compiler_params=_cparams("parallel", "arbitrary"),
        name="norm_matmul_merged",
    )(xp, xs, g.reshape(1, d), w)


def _matmul_res2_kernel(a_ref, w_ref, xp_ref, xs_ref, o_ref):
    def fill(x):
        o_ref[...] = x + jnp.dot(a_ref[...], w_ref[...], preferred_element_type=F32)

    _with_merged_rows(xp_ref, xs_ref, fill)


def matmul_residual_merged(a, w, xp, xs, tm):
    t, k = a.shape
    n = w.shape[1]
    bs = xs.shape[0]
    assert t % tm == 0 and bs <= tm and xp.shape[0] + bs == t
    return pl.pallas_call(
        _matmul_res2_kernel,
        grid=(t // tm,),
        in_specs=[pl.BlockSpec((tm, k), lambda i: (i, 0)),
                  pl.BlockSpec((k, n), lambda i: (0, 0))] + _merged_specs(tm, n, bs, 1),
        out_specs=pl.BlockSpec((tm, n), lambda i: (i, 0)),
        out_shape=jax.ShapeDtypeStruct((t, n), F32),
        compiler_params=_cparams("parallel"),
        name="matmul_residual_merged",
    )(a, w, xp, xs)


def _ffn_kernel(x_ref, g_ref, wg_ref, wu_ref, wd_ref, o_ref, h_ref):
    f = pl.program_id(1)

    @pl.when(f == 0)
    def _():
        x = x_ref[...]
        h_ref[...] = _rms(x, g_ref[...]).astype(BF16)
        o_ref[...] = x

    h = h_ref[...]
    a = jnp.dot(h, wg_ref[...], preferred_element_type=F32)
    u = jnp.dot(h, wu_ref[...], preferred_element_type=F32)
    act = (_silu(a) * u).astype(BF16)
    o_ref[...] += jnp.dot(act, wd_ref[...], preferred_element_type=F32)


def ffn_residual(x, g, wg, wu, wd, tm, tf):
    t, d = x.shape
    f = wg.shape[1]
    return pl.pallas_call(
        _ffn_kernel,
        grid=(t // tm, f // tf),
        in_specs=[pl.BlockSpec((tm, d), lambda i, j: (i, 0)),
                  pl.BlockSpec((1, d), lambda i, j: (0, 0)),
                  pl.BlockSpec((d, tf), lambda i, j: (0, j)),
                  pl.BlockSpec((d, tf), lambda i, j: (0, j)),
                  pl.BlockSpec((tf, d), lambda i, j: (j, 0))],
        out_specs=pl.BlockSpec((tm, d), lambda i, j: (i, 0)),
        out_shape=jax.ShapeDtypeStruct((t, d), F32),
        scratch_shapes=[pltpu.VMEM((tm, d), BF16)],
        compiler_params=_cparams("parallel", "arbitrary"),
        name="ffn_residual",
    )(x, g.reshape(1, d), wg, wu, wd)


def _rotary(x, cos, sin, half):
    x1 = x[:, :half]
    x2 = x[:, half:]
    return jnp.concatenate([x1 * cos - x2 * sin, x2 * cos + x1 * sin], axis=-1)


def _group_norm_gate(o, gate, gain):
    mu = jnp.mean(o, axis=-1, keepdims=True)
    d = o - mu
    var = jnp.mean(d * d, axis=-1, keepdims=True)
    return _silu(gate) * (d * lax.rsqrt(var + GN_EPS) * gain)


def _ret_scan_kernel(lg_ref, q_ref, k_ref, v_ref, g_ref, cos_ref, sin_ref, gn_ref,
                     o_ref, sfin_ref, s_ref, mask_ref, din_ref, dout_ref, *, chunk, dk, dv, hp):
    c = pl.program_id(2)
    head0 = pl.program_id(1) * hp

    @pl.when(c == 0)
    def _():
        s_ref[...] = jnp.zeros_like(s_ref)
        t = lax.broadcasted_iota(I32, (chunk, LANES), 0).astype(F32)
        ti = lax.broadcasted_iota(I32, (chunk, chunk), 0)
        si = lax.broadcasted_iota(I32, (chunk, chunk), 1)
        causal = ti >= si
        rel = jnp.where(causal, (ti - si).astype(F32), 0.0)
        for j in range(hp):
            lg = lg_ref[head0 + j]
            din_ref[j] = jnp.exp((t + 1.0) * lg)
            dout_ref[j] = jnp.exp((chunk - 1.0 - t) * lg)
            mask_ref[j] = jnp.where(causal, jnp.exp(rel * lg), 0.0)

    half = dk // 2
    cos = cos_ref[...]
    sin = sin_ref[...]
    for j in range(hp):
        qk_cols = slice(j * dk, (j + 1) * dk)
        v_cols = slice(j * dv, (j + 1) * dv)
        q = _rotary(q_ref[:, qk_cols].astype(F32), cos, sin, half)
        k = _rotary(k_ref[:, qk_cols].astype(F32), cos, sin, half) * (dk ** -0.5)
        v = v_ref[:, v_cols]
        decay_in = _lane_tile(din_ref[j], dk)
        decay_out = _lane_tile(dout_ref[j], dk)
        decay_chunk = jnp.exp(jnp.zeros((1, dv), F32) + chunk * lg_ref[head0 + j])

        s = s_ref[j]
        scores = (lax.dot_general(q.astype(BF16), k.astype(BF16), NT_DIMS, preferred_element_type=F32)
                  * mask_ref[j])
        o = (jnp.dot(scores.astype(BF16), v, preferred_element_type=F32)
             + jnp.dot((q * decay_in).astype(BF16), s.astype(BF16), preferred_element_type=F32))
        kd_t = (k * decay_out).T.astype(BF16)
        s_ref[j] = s * decay_chunk + jnp.dot(kd_t, v, preferred_element_type=F32)
        o_ref[:, v_cols] = _group_norm_gate(o, g_ref[:, v_cols].astype(F32), gn_ref[:, v_cols]).astype(BF16)

    @pl.when(c == pl.num_programs(2) - 1)
    def _():
        sfin_ref[0] = s_ref[...]


def retention_scan(proj, log_gamma, cos, sin, gn_g, *, batch, seq, heads, dk, dv, rows_total):
    chunk = min(RET_CHUNK, seq)
    nc = seq // chunk
    hp = RET_SCAN_HEADS
    assert heads % hp == 0
    ng = heads // hp
    kern = functools.partial(_ret_scan_kernel, chunk=chunk, dk=dk, dv=dv, hp=hp)
    v0 = 2 * heads * dk // (hp * dv)
    return pl.pallas_call(
        kern,
        grid=(batch, ng, nc),
        in_specs=[pl.BlockSpec(memory_space=pltpu.SMEM),
                  pl.BlockSpec((chunk, hp * dk), lambda b, h, c: (b * nc + c, h)),
                  pl.BlockSpec((chunk, hp * dk), lambda b, h, c: (b * nc + c, ng + h)),
                  pl.BlockSpec((chunk, hp * dv), lambda b, h, c: (b * nc + c, v0 + h)),
                  pl.BlockSpec((chunk, hp * dv), lambda b, h, c: (b * nc + c, v0 + ng + h)),
                  pl.BlockSpec((chunk, dk // 2), lambda b, h, c: (c, 0)),
                  pl.BlockSpec((chunk, dk // 2), lambda b, h, c: (c, 0)),
                  pl.BlockSpec((1, hp * dv), lambda b, h, c: (0, h))],
        out_specs=[pl.BlockSpec((chunk, hp * dv), lambda b, h, c: (b * nc + c, h)),
                   pl.BlockSpec((1, hp, dk, dv), lambda b, h, c: (b, h, 0, 0))],
        out_shape=[jax.ShapeDtypeStruct((rows_total, heads * dv), BF16),
                   jax.ShapeDtypeStruct((batch, heads, dk, dv), F32)],
        scratch_shapes=[pltpu.VMEM((hp, dk, dv), F32),
                        pltpu.VMEM((hp, chunk, chunk), F32),
                        pltpu.VMEM((hp, chunk, LANES), F32),
                        pltpu.VMEM((hp, chunk, LANES), F32)],
        compiler_params=_cparams("parallel", "parallel", "arbitrary"),
        name="retention_scan",
    )(log_gamma, proj, proj, proj, proj, cos, sin, gn_g.reshape(1, heads * dv))


def _one_hot_rows(nb, b):
    return jnp.where(lax.broadcasted_iota(I32, (nb, LANES), 0) == b, 1.0, 0.0).astype(BF16)


def _lane_tile(x, width):
    return jnp.concatenate([x] * (width // x.shape[1]), axis=1)


def _ret_step_kernel(lg_ref, q_ref, k_ref, v_ref, g_ref, cos_ref, sin_ref, gn_ref, s_ref, prev_ref,
                     o_ref, snew_ref, qt_ref, kt_ref, vf_ref, gf_ref, orow_ref, *, heads, dk, dv, nb, per_step):
    del prev_ref
    step = pl.program_id(0)
    half = dk // 2

    @pl.when(step == 0)
    def _():
        orow_ref[...] = jnp.zeros_like(orow_ref)
        vf_ref[...] = v_ref[...].astype(F32)
        gf_ref[...] = g_ref[...].astype(F32)
        cos = cos_ref[...]
        sin = sin_ref[...]
        for h in range(heads):
            qh = _rotary(q_ref[:, h * dk:(h + 1) * dk].astype(F32), cos, sin, half)
            kh = _rotary(k_ref[:, h * dk:(h + 1) * dk].astype(F32), cos, sin, half) * (dk ** -0.5)
            qt_ref[h] = qh.T.astype(BF16)
            kt_ref[h] = kh.T.astype(BF16)

    for i in range(per_step):
        b = step * per_step + i
        onehot = _one_hot_rows(nb, b)
        for h in range(heads):
            cols = slice(h * dv, (h + 1) * dv)
            kcol = _lane_tile(jnp.dot(kt_ref[h], onehot, preferred_element_type=F32), dv)
            qcol = _lane_tile(jnp.dot(qt_ref[h], onehot, preferred_element_type=F32), dv)
            vrow = _load_row(vf_ref, b, cols)
            gamma = jnp.exp(jnp.zeros((1, dv), F32) + lg_ref[h])
            s_new = s_ref[i, h] * gamma + kcol * vrow
            snew_ref[i, h] = s_new
            o = jnp.sum(s_new * qcol, axis=0, keepdims=True)
            _store_row(orow_ref, b, cols, _group_norm_gate(o, _load_row(gf_ref, b, cols), gn_ref[:, cols]))

    @pl.when(step == pl.num_programs(0) - 1)
    def _():
        o_ref[...] = orow_ref[...].astype(BF16)


def retention_step(proj, gated, state, log_gamma, cos, sin, gn_g, *, row0, nb, heads, dk, dv):
    per_step = RET_STEP_SEQS
    kern = functools.partial(_ret_step_kernel, heads=heads, dk=dk, dv=dv, nb=nb, per_step=per_step)
    rb = row0 // nb
    wq = heads * dk
    wv = heads * dv
    return pl.pallas_call(
        kern,
        grid=(nb // per_step,),
        in_specs=[pl.BlockSpec(memory_space=pltpu.SMEM),
                  pl.BlockSpec((nb, wq), lambda b: (rb, 0)),
                  pl.BlockSpec((nb, wq), lambda b: (rb, 1)),
                  pl.BlockSpec((nb, wv), lambda b: (rb, 2 * wq // wv)),
                  pl.BlockSpec((nb, wv), lambda b: (rb, 2 * wq // wv + 1)),
                  pl.BlockSpec((1, dk // 2), lambda b: (0, 0)),
                  pl.BlockSpec((1, dk // 2), lambda b: (0, 0)),
                  pl.BlockSpec((1, wv), lambda b: (0, 0)),
                  pl.BlockSpec((per_step, heads, dk, dv), lambda b: (b, 0, 0, 0)),
                  pl.BlockSpec(memory_space=pl.ANY)],
        out_specs=[pl.BlockSpec((nb, wv), lambda b: (rb, 0)),
                   pl.BlockSpec((per_step, heads, dk, dv), lambda b: (b, 0, 0, 0))],
        out_shape=[jax.ShapeDtypeStruct(gated.shape, gated.dtype),
                   jax.ShapeDtypeStruct(state.shape, F32)],
        scratch_shapes=[pltpu.VMEM((heads, dk, nb), BF16),
                        pltpu.VMEM((heads, dk, nb), BF16),
                        pltpu.VMEM((nb, wv), F32),
                        pltpu.VMEM((nb, wv), F32),
                        pltpu.VMEM((nb, wv), F32)],
        input_output_aliases={9: 0},
        compiler_params=_cparams("arbitrary"),
        name="retention_step",
    )(log_gamma, proj, proj, proj, proj, cos, sin, gn_g.reshape(1, wv), state, gated)


def _lower_bound(lbp, layer):
    m = jnp.max(lbp, axis=0, keepdims=True)
    e = jnp.exp(lbp - m)
    p = e / jnp.sum(e, axis=0, keepdims=True)
    return jnp.sum(p[:layer + 1], axis=0, keepdims=True) - p[0:1]


def _split_dot(mat_bf16, x):
    hi = x.astype(BF16)
    r1 = x - hi.astype(F32)
    mid = r1.astype(BF16)
    lo = (r1 - mid.astype(F32)).astype(BF16)
    return (jnp.dot(mat_bf16, hi, preferred_element_type=F32)
            + jnp.dot(mat_bf16, mid, preferred_element_type=F32)
            + jnp.dot(mat_bf16, lo, preferred_element_type=F32))


def _rms_gate(o, gate, gain):
    return _silu(gate) * (o * lax.rsqrt(jnp.mean(o * o, axis=-1, keepdims=True) + NORM_EPS) * gain)


def _group_row(x, s, group):
    n, w = x.shape
    x3 = x.reshape(n // group, group, w)
    return jnp.broadcast_to(x3[:, s:s + 1, :], x3.shape).reshape(n, w)


def _hgrn_scan_kernel(*refs, chunk, heads, dk, diag, layer, ncast):
    lbp_ref, q_ref, f_ref, i_ref, g_ref, ng_ref = refs[:6]
    cast_in = refs[6:6 + ncast]
    o_ref, sfin_ref = refs[6 + ncast:8 + ncast]
    cast_out = refs[8 + ncast:8 + 2 * ncast]
    st_ref = refs[8 + 2 * ncast]
    c = pl.program_id(1)

    @pl.when(c == 0)
    def _():
        st_ref[...] = jnp.zeros_like(st_ref)

    for src, dst in zip(cast_in, cast_out):
        dst[...] = src[...].astype(BF16)

    w = heads * dk
    lb = _lower_bound(lbp_ref[...], layer)
    qs = _silu(q_ref[...].astype(F32))
    forget = lb + (1.0 - lb) * _sigmoid(f_ref[...].astype(F32))
    kk = 1.0 - forget
    logf = jnp.log(forget)

    row = lax.broadcasted_iota(I32, (chunk, chunk), 0)
    col = lax.broadcasted_iota(I32, (chunk, chunk), 1)
    tril = jnp.where(row >= col, 1.0, 0.0).astype(BF16)
    bcum = _split_dot(tril, logf)
    blast = bcum[chunk - 1:chunk, :]
    qe = (qs * jnp.exp(bcum)).astype(BF16)
    kdec = (kk * jnp.exp(blast - bcum)).astype(BF16)
    lk = jnp.log(kk) - bcum
    iv = i_ref[...]
    gate = g_ref[...].astype(F32)
    gain = ng_ref[...]

    u = jnp.where(row > col, row ^ col, 0)
    rowv = lax.broadcasted_iota(I32, (chunk, 1), 0)
    levels = []
    m = chunk // 2
    while m >= diag:
        span = 2 * m
        bref = jnp.concatenate(
            [jnp.broadcast_to(bcum[g * span + m - 1:g * span + m], (span, w)) for g in range(chunk // span)],
            axis=0)
        upper = (rowv & m) != 0
        d = bcum - bref
        x = (jnp.where(upper, qs, kk) * jnp.exp(jnp.where(upper, d, -d))).astype(BF16)
        levels.append((x, lax.shift_right_logical(u, m.bit_length() - 1) == 1))
        m //= 2
    in_block = row & (diag - 1)
    block_col0 = row - in_block
    diag_masks = [(col == block_col0 + s) & (in_block >= s) for s in range(diag)]
    ones = jnp.ones((dk, LANES), BF16)

    for h in range(heads):
        cols = slice(h * dk, (h + 1) * dk)
        qs_h = qs[:, cols]
        lk_h = lk[:, cols]
        b_h = bcum[:, cols]
        a = jnp.zeros((chunk, chunk), F32)
        for x, mask in levels:
            x_h = x[:, cols]
            a = jnp.where(mask, lax.dot_general(x_h, x_h, NT_DIMS, preferred_element_type=F32), a)
        ws = [qs_h * jnp.exp(b_h + _group_row(lk_h, s, diag)) for s in range(diag)]
        rsum = jnp.dot(jnp.concatenate(ws, axis=0).astype(BF16), ones, preferred_element_type=F32)
        for s in range(diag):
            a = jnp.where(diag_masks[s], rsum[s * chunk:(s + 1) * chunk], a)
        st = st_ref[h]
        i_h = iv[:, cols]
        o_h = (jnp.dot(a.astype(BF16), i_h, preferred_element_type=F32)
               + lax.dot_general(qe[:, cols], st.astype(BF16), NT_DIMS, preferred_element_type=F32))
        st_new = (st * jnp.exp(blast[:, cols])
                  + lax.dot_general(i_h, kdec[:, cols], TN_DIMS, preferred_element_type=F32))
        st_ref[h] = st_new
        o_ref[:, cols] = _rms_gate(o_h, gate[:, cols], gain[:, cols]).astype(BF16)

    @pl.when(c == pl.num_programs(1) - 1)
    def _():
        for h in range(heads):
            sfin_ref[0, h] = st_ref[h].T


def hgrn_scan(proj, lb_param, norm_g, side_f32, *, batch, seq, heads, dk, rows_total, layer):
    chunk = min(HG_CHUNK, seq)
    assert chunk == LANES and dk == LANES
    nc = seq // chunk
    w = heads * dk
    steps = batch * nc
    slabs = []
    for a in side_f32:
        assert a.shape[0] % (steps * 16) == 0, (a.shape, steps)
        slabs.append(pl.BlockSpec((a.shape[0] // steps, a.shape[1]), lambda b, c: (b * nc + c, 0)))
    kern = functools.partial(_hgrn_scan_kernel, chunk=chunk, heads=heads, dk=dk, diag=HG_DIAG, layer=layer,
                             ncast=len(side_f32))
    return pl.pallas_call(
        kern,
        grid=(batch, nc),
        in_specs=[pl.BlockSpec(lb_param.shape, lambda b, c: (0, 0)),
                  pl.BlockSpec((chunk, w), lambda b, c: (b * nc + c, 0)),
                  pl.BlockSpec((chunk, w), lambda b, c: (b * nc + c, 1)),
                  pl.BlockSpec((chunk, w), lambda b, c: (b * nc + c, 2)),
                  pl.BlockSpec((chunk, w), lambda b, c: (b * nc + c, 3)),
                  pl.BlockSpec((1, w), lambda b, c: (0, 0))] + slabs,
        out_specs=[pl.BlockSpec((chunk, w), lambda b, c: (b * nc + c, 0)),
                   pl.BlockSpec((1, heads, dk, dk), lambda b, c: (b, 0, 0, 0))] + slabs,
        out_shape=[jax.ShapeDtypeStruct((rows_total, w), BF16),
                   jax.ShapeDtypeStruct((batch, heads, dk, dk), F32)]
                  + [jax.ShapeDtypeStruct(a.shape, BF16) for a in side_f32],
        scratch_shapes=[pltpu.VMEM((heads, dk, dk), F32)],
        compiler_params=_cparams("parallel", "arbitrary"),
        name="hgrn_scan",
    )(lb_param, proj, proj, proj, proj, norm_g.reshape(1, w), *side_f32)


def _hgrn_step_kernel(lbp_ref, q_ref, f_ref, i_ref, g_ref, ng_ref, s_ref, prev_ref,
                      o_ref, snew_ref, qt_ref, ft_ref, if_ref, gf_ref, orow_ref,
                      *, heads, dk, nb, layer, per_step):
    del prev_ref
    step = pl.program_id(0)

    @pl.when(step == 0)
    def _():
        orow_ref[...] = jnp.zeros_like(orow_ref)
        if_ref[...] = i_ref[...].astype(F32)
        gf_ref[...] = g_ref[...].astype(F32)
        lb = _lower_bound(lbp_ref[...], layer)
        qs = _silu(q_ref[...].astype(F32))
        forget = lb + (1.0 - lb) * _sigmoid(f_ref[...].astype(F32))
        for h in range(heads):
            cols = slice(h * dk, (h + 1) * dk)
            qt_ref[h] = qs[:, cols].T.astype(BF16)
            ft = forget[:, cols].T
            hi = ft.astype(BF16)
            r1 = ft - hi.astype(F32)
            mid = r1.astype(BF16)
            ft_ref[0, h] = hi
            ft_ref[1, h] = mid
            ft_ref[2, h] = (r1 - mid.astype(F32)).astype(BF16)

    for i in range(per_step):
        b = step * per_step + i
        onehot = _one_hot_rows(nb, b)
        for h in range(heads):
            cols = slice(h * dk, (h + 1) * dk)
            fcol = (jnp.dot(ft_ref[0, h], onehot, preferred_element_type=F32)
                    + jnp.dot(ft_ref[1, h], onehot, preferred_element_type=F32)
                    + jnp.dot(ft_ref[2, h], onehot, preferred_element_type=F32))
            qcol = jnp.dot(qt_ref[h], onehot, preferred_element_type=F32)
            irow = _load_row(if_ref, b, cols)
            s_new = s_ref[i, h] * fcol + (1.0 - fcol) * irow
            snew_ref[i, h] = s_new
            o = jnp.sum(s_new * qcol, axis=0, keepdims=True)
            _store_row(orow_ref, b, cols, _rms_gate(o, _load_row(gf_ref, b, cols), ng_ref[:, cols]))

    @pl.when(step == pl.num_programs(0) - 1)
    def _():
        o_ref[...] = orow_ref[...].astype(BF16)


def hgrn_step(proj, gated, state, lb_param, norm_g, *, row0, nb, heads, dk, layer):
    per_step = HG_STEP_SEQS
    kern = functools.partial(_hgrn_step_kernel, heads=heads, dk=dk, nb=nb, layer=layer, per_step=per_step)
    rb = row0 // nb
    w = heads * dk
    return pl.pallas_call(
        kern,
        grid=(nb // per_step,),
        in_specs=[pl.BlockSpec(lb_param.shape, lambda b: (0, 0)),
                  pl.BlockSpec((nb, w), lambda b: (rb, 0)),
                  pl.BlockSpec((nb, w), lambda b: (rb, 1)),
                  pl.BlockSpec((nb, w), lambda b: (rb, 2)),
                  pl.BlockSpec((nb, w), lambda b: (rb, 3)),
                  pl.BlockSpec((1, w), lambda b: (0, 0)),
                  pl.BlockSpec((per_step, heads, dk, dk), lambda b: (b, 0, 0, 0)),
                  pl.BlockSpec(memory_space=pl.ANY)],
        out_specs=[pl.BlockSpec((nb, w), lambda b: (rb, 0)),
                   pl.BlockSpec((per_step, heads, dk, dk), lambda b: (b, 0, 0, 0))],
        out_shape=[jax.ShapeDtypeStruct(gated.shape, gated.dtype),
                   jax.ShapeDtypeStruct(state.shape, F32)],
        scratch_shapes=[pltpu.VMEM((heads, dk, nb), BF16),
                        pltpu.VMEM((3, heads, dk, nb), BF16),
                        pltpu.VMEM((nb, w), F32),
                        pltpu.VMEM((nb, w), F32),
                        pltpu.VMEM((nb, w), F32)],
        input_output_aliases={7: 0},
        compiler_params=_cparams("arbitrary"),
        name="hgrn_step",
    )(lb_param, proj, proj, proj, proj, norm_g.reshape(1, w), state, gated)


def _store_row_tiles(ref, r0, x):
    n, d = x.shape
    s = d // LANES
    for c in range(s):
        ref[pl.ds(s * r0 + c, n, stride=s), :] = x[:, c * LANES:(c + 1) * LANES]


def _load_row_tiles(ref, r0, n, s):
    return jnp.concatenate([ref[pl.ds(s * r0 + c, n, stride=s), :] for c in range(s)], axis=1)


def _router_kernel(a_ref, w_ref, x_ref, g_ref, r_ref, xo_ref, h_ref, route_ref, cnt_ref,
                   carry_ref, rhi_ref, rlo_ref, *, tm, experts):
    i = pl.program_id(0)

    @pl.when(i == 0)
    def _():
        carry_ref[...] = jnp.zeros_like(carry_ref)
        r = r_ref[...]
        rhi = r.astype(BF16)
        rhi_ref[...] = rhi
        rlo_ref[...] = (r - rhi.astype(F32)).astype(BF16)

    step = (tm // ROUTER_GROUPS + 15) // 16 * 16
    groups = tuple((r, min(r + step, tm)) for r in range(0, tm, step))
    all_logits = []
    for r0, r1 in groups:
        rows = slice(r0, r1)
        x = x_ref[rows] + jnp.dot(a_ref[rows], w_ref[...], preferred_element_type=F32)
        xo_ref[rows] = x
        h = _rms(x, g_ref[...])
        _store_row_tiles(h_ref, r0, h)
        hhi = h.astype(BF16)
        hlo = (h - hhi.astype(F32)).astype(BF16)
        all_logits.append(jnp.dot(hhi, rhi_ref[...], preferred_element_type=F32)
                          + jnp.dot(hlo, rhi_ref[...], preferred_element_type=F32)
                          + jnp.dot(hhi, rlo_ref[...], preferred_element_type=F32))
    total = carry_ref[...]
    for (r0, r1), logits in zip(groups, all_logits):
        n = r1 - r0
        rows = slice(r0, r1)
        lane = lax.broadcasted_iota(I32, (n, LANES), 1)
        valid = lane < experts
        z = jnp.where(valid, logits, -jnp.inf)
        ez = jnp.exp(z - jnp.max(z, axis=-1, keepdims=True))
        p = ez / jnp.sum(ez, axis=-1, keepdims=True)
        p = jnp.where(valid, p, -1.0)
        lane_f = lane.astype(F32)
        v1 = jnp.max(p, axis=-1, keepdims=True)
        i1 = jnp.min(jnp.where(p == v1, lane_f, float(LANES)), axis=-1, keepdims=True)
        p2 = jnp.where(lane_f == i1, -1.0, p)
        v2 = jnp.max(p2, axis=-1, keepdims=True)
        i2 = jnp.min(jnp.where(p2 == v2, lane_f, float(LANES)), axis=-1, keepdims=True)
        den = v1 + v2
        m0 = jnp.where(lane_f == i1, 1.0, 0.0)
        m1 = jnp.where(lane_f == i2, 1.0, 0.0)
        msum = m0 + m1
        ri = lax.broadcasted_iota(I32, (n, n), 0)
        ci = lax.broadcasted_iota(I32, (n, n), 1)
        strict = jnp.where(ri > ci, 1.0, 0.0).astype(BF16)
        before = jnp.dot(strict, msum.astype(BF16), preferred_element_type=F32) + total
        rank0 = jnp.sum(m0 * before, axis=-1, keepdims=True)
        rank1 = jnp.sum(m1 * before, axis=-1, keepdims=True)
        total = total + jnp.sum(msum, axis=0, keepdims=True)
        route_ref[rows] = jnp.where(lane == 0, i1,
                          jnp.where(lane == 1, i2,
                          jnp.where(lane == 2, v1 / den,
                          jnp.where(lane == 3, v2 / den,
                          jnp.where(lane == 4, rank0,
                          jnp.where(lane == 5, rank1, 0.0))))))
    carry_ref[...] = total
    cnt_ref[...] = jnp.broadcast_to(total, cnt_ref.shape)


def outproj_router(a, w, x, g, router_w, tm):
    t, d = x.shape
    k = a.shape[1]
    experts = router_w.shape[1]
    rpad = jnp.zeros((d, LANES), F32).at[:, :experts].set(router_w)
    kern = functools.partial(_router_kernel, tm=tm, experts=experts)
    return pl.pallas_call(
        kern,
        grid=(t // tm,),
        in_specs=[pl.BlockSpec((tm, k), lambda i: (i, 0)),
                  pl.BlockSpec((k, d), lambda i: (0, 0)),
                  pl.BlockSpec((tm, d), lambda i: (i, 0)),
                  pl.BlockSpec((1, d), lambda i: (0, 0)),
                  pl.BlockSpec((d, LANES), lambda i: (0, 0))],
        out_specs=[pl.BlockSpec((tm, d), lambda i: (i, 0)),
                   pl.BlockSpec((tm * (d // LANES), LANES), lambda i: (i, 0)),
                   pl.BlockSpec((tm, LANES), lambda i: (i, 0)),
                   pl.BlockSpec((8, LANES), lambda i: (0, 0))],
        out_shape=[jax.ShapeDtypeStruct((t, d), F32),
                   jax.ShapeDtypeStruct((t * (d // LANES), LANES), F32),
                   jax.ShapeDtypeStruct((t, LANES), F32),
                   jax.ShapeDtypeStruct((8, LANES), F32)],
        scratch_shapes=[pltpu.VMEM((1, LANES), F32),
                        pltpu.VMEM((d, LANES), BF16),
                        pltpu.VMEM((d, LANES), BF16)],
        compiler_params=_cparams("arbitrary"),
        name="outproj_router",
    )(a, w, x, g.reshape(1, d), rpad)


def _plan_kernel(cnt_ref, route_ref, frow_ref, fexp_ref, nfull_ref, trow_ref, tsub_ref, zrow_ref, pos_ref,
                 *, experts, nfull_max, dummy, row_tile, sub, row_scale):
    tile0 = jnp.int32(0)
    nfull = jnp.int32(0)
    last = jnp.int32(0)
    route = route_ref[...]
    slot_expert = pltpu.roll(route, 4, axis=1)
    first_row = jnp.zeros_like(route)
    for e in range(experts):
        n = cnt_ref[e]
        first_row = jnp.where(slot_expert == float(e), (tile0 * row_tile).astype(F32), first_row)
        full = n // row_tile
        rem = n - full * row_tile

        def fill(j, carry, e=e, tile0=tile0, nfull=nfull):
            frow_ref[nfull + j] = tile0 + j
            fexp_ref[nfull + j] = e
            return carry

        lax.fori_loop(0, full, fill, 0)
        trow_ref[e] = jnp.where(rem > 0, tile0 + full, dummy)
        tsub_ref[e] = (rem + (sub - 1)) // sub
        zrow_ref[e] = tile0 * row_tile + (n // sub) * sub
        last = jnp.where(full > 0, e, last)
        nfull = nfull + full
        tile0 = tile0 + full + jnp.where(rem > 0, 1, 0)
    nfull_ref[0] = nfull
    pos_ref[...] = ((first_row + route) * float(row_scale)).astype(I32)

    def unused(j, carry):
        frow_ref[j] = dummy
        fexp_ref[j] = last
        return carry

    lax.fori_loop(nfull, nfull_max, unused, 0)


def plan(counts, route, ntiles, nfull_max, row_scale):
    experts = counts.shape[0]
    kern = functools.partial(_plan_kernel, experts=experts, nfull_max=nfull_max, dummy=ntiles - 1,
                             row_tile=MOE_ROW_TILE, sub=MOE_SUB, row_scale=row_scale)
    smem = pl.BlockSpec(memory_space=pltpu.SMEM)
    vmem = pl.BlockSpec(memory_space=pltpu.VMEM)
    return pl.pallas_call(
        kern,
        in_specs=[smem, vmem],
        out_specs=[smem] * 6 + [vmem],
        out_shape=[jax.ShapeDtypeStruct((nfull_max,), I32),
                   jax.ShapeDtypeStruct((nfull_max,), I32),
                   jax.ShapeDtypeStruct((1,), I32),
                   jax.ShapeDtypeStruct((experts,), I32),
                   jax.ShapeDtypeStruct((experts,), I32),
                   jax.ShapeDtypeStruct((experts,), I32),
                   jax.ShapeDtypeStruct(route.shape, I32)],
        compiler_params=pltpu.CompilerParams(vmem_limit_bytes=VMEM_LIMIT),
        name="moe_plan",
    )(counts, route)


def _dispatch_kernel(zrow_ref, p_ref, h_ref, out_hbm, zero_ref, sem, zsem, *, tm, experts, s):
    @pl.when(pl.program_id(0) == 0)
    def _():
        zero_ref[...] = jnp.zeros_like(zero_ref)

        def zero_copy(e):
            row = pl.multiple_of(zrow_ref[e] * s, MOE_SUB * s)
            return pltpu.make_async_copy(zero_ref, out_hbm.at[pl.ds(row, MOE_SUB * s)], zsem)

        for e in range(experts):
            zero_copy(e).start()
        for e in range(experts):
            zero_copy(e).wait()

    def row_copy(t, dst):
        return pltpu.make_async_copy(h_ref.at[pl.ds(pl.multiple_of(t * s, s), s)],
                                     out_hbm.at[pl.ds(pl.multiple_of(dst, s), s)], sem)

    def issue(t, carry):
        for k in range(TOP_K):
            row_copy(t, p_ref[0, 0, TOP_K * t + k]).start()
        return carry

    lax.fori_loop(0, tm, issue, 0, unroll=DMA_UNROLL)

    def drain(t, carry):
        for k in range(TOP_K):
            row_copy(0, 0).wait()
        return carry

    lax.fori_loop(0, tm, drain, 0, unroll=DMA_UNROLL)


def dispatch(h, pos, zrow, rows, tm, s):
    nt = h.shape[0] // (tm * s)
    kern = functools.partial(_dispatch_kernel, tm=tm, experts=zrow.shape[0], s=s)
    return pl.pallas_call(
        kern,
        grid_spec=pltpu.PrefetchScalarGridSpec(
            num_scalar_prefetch=1,
            grid=(nt,),
            in_specs=[pl.BlockSpec((1, 1, TOP_K * tm), lambda i, zrow: (i, 0, 0), memory_space=pltpu.SMEM),
                      pl.BlockSpec((tm * s, LANES), lambda i, zrow: (i, 0))],
            out_specs=pl.BlockSpec(memory_space=pl.ANY),
            scratch_shapes=[pltpu.VMEM((MOE_SUB * s, LANES), h.dtype),
                            pltpu.SemaphoreType.DMA(()),
                            pltpu.SemaphoreType.DMA(())]),
        out_shape=jax.ShapeDtypeStruct((rows * s, LANES), h.dtype),
        compiler_params=_cparams("arbitrary"),
        name="moe_dispatch",
    )(zrow, pos.reshape(nt, 1, TOP_K * tm), h)


def _expert_full_kernel(nfull_ref, frow_ref, fexp_ref, x_ref, wg_ref, wu_ref, wd_ref, o_ref, h_ref, acc_ref,
                        *, s):
    del frow_ref, fexp_ref
    j = pl.program_id(0)
    f = pl.program_id(1)
    active = j < nfull_ref[0]
    rows = h_ref.shape[0]

    @pl.when(jnp.logical_not(active) & (f == 0))
    def _():
        o_ref[...] = jnp.zeros_like(o_ref)

    @pl.when(active & (f == 0))
    def _():
        h_ref[...] = _load_row_tiles(x_ref, 0, rows, s).astype(BF16)
        acc_ref[...] = jnp.zeros_like(acc_ref)

    @pl.when(active)
    def _():
        h = h_ref[...]
        a = jnp.dot(h, wg_ref[0], preferred_element_type=F32)
        u = jnp.dot(h, wu_ref[0], preferred_element_type=F32)
        act = (_silu(a) * u).astype(BF16)
        acc_ref[...] += jnp.dot(act, wd_ref[0], preferred_element_type=F32)

    @pl.when(active & (f == pl.num_programs(1) - 1))
    def _():
        _store_row_tiles(o_ref, 0, acc_ref[...])


def _expert_tail_kernel(trow_ref, tsub_ref, x_ref, wg_ref, wu_ref, wd_ref, prev_ref, o_ref, h_ref, acc_ref,
                        *, sub, nsub_max, s):
    del trow_ref, prev_ref
    f = pl.program_id(1)
    nsub = tsub_ref[pl.program_id(0)]

    @pl.when(f == 0)
    def _():
        o_ref[...] = jnp.zeros_like(o_ref)
        acc_ref[...] = jnp.zeros_like(acc_ref)

    for sb in range(nsub_max):
        @pl.when(sb < nsub)
        def _():
            rows = slice(sb * sub, (sb + 1) * sub)

            @pl.when(f == 0)
            def _():
                h_ref[rows] = _load_row_tiles(x_ref, sb * sub, sub, s).astype(BF16)

            h = h_ref[rows]
            a = jnp.dot(h, wg_ref[0], preferred_element_type=F32)
            u = jnp.dot(h, wu_ref[0], preferred_element_type=F32)
            act = (_silu(a) * u).astype(BF16)
            acc_ref[rows] += jnp.dot(act, wd_ref[0], preferred_element_type=F32)

            @pl.when(f == pl.num_programs(1) - 1)
            def _():
                _store_row_tiles(o_ref, sb * sub, acc_ref[rows])


def expert_ffn(xs, nfull, frow, fexp, trow, tsub, wg, wu, wd):
    experts, d, fdim = wg.shape
    s = d // LANES
    tf = MOE_F_TILE
    nf = fdim // tf
    nfull_max = frow.shape[0]
    tile = (MOE_ROW_TILE * s, LANES)
    scratch = [pltpu.VMEM((MOE_ROW_TILE, d), BF16), pltpu.VMEM((MOE_ROW_TILE, d), F32)]

    def fcol_full(j, f, nfull):
        return jnp.where(j < nfull[0], f, nf - 1)

    ys = pl.pallas_call(
        functools.partial(_expert_full_kernel, s=s),
        grid_spec=pltpu.PrefetchScalarGridSpec(
            num_scalar_prefetch=3,
            grid=(nfull_max, nf),
            in_specs=[pl.BlockSpec(tile, lambda j, f, nfull, frow, fexp: (frow[j], 0)),
                      pl.BlockSpec((1, d, tf), lambda j, f, nfull, frow, fexp: (fexp[j], 0, fcol_full(j, f, nfull))),
                      pl.BlockSpec((1, d, tf), lambda j, f, nfull, frow, fexp: (fexp[j], 0, fcol_full(j, f, nfull))),
                      pl.BlockSpec((1, tf, d), lambda j, f, nfull, frow, fexp: (fexp[j], fcol_full(j, f, nfull), 0))],
            out_specs=pl.BlockSpec(tile, lambda j, f, nfull, frow, fexp: (frow[j], 0)),
            scratch_shapes=scratch),
        out_shape=jax.ShapeDtypeStruct(xs.shape, F32),
        compiler_params=_cparams("arbitrary", "arbitrary"),
        name="expert_ffn_full",
    )(nfull, frow, fexp, xs, wg, wu, wd)

    def fcol_tail(e, f, tsub):
        return jnp.where(tsub[e] > 0, f, nf - 1)

    kern = functools.partial(_expert_tail_kernel, sub=MOE_SUB, nsub_max=MOE_ROW_TILE // MOE_SUB, s=s)
    return pl.pallas_call(
        kern,
        grid_spec=pltpu.PrefetchScalarGridSpec(
            num_scalar_prefetch=2,
            grid=(experts, nf),
            in_specs=[pl.BlockSpec(tile, lambda e, f, trow, tsub: (trow[e], 0)),
                      pl.BlockSpec((1, d, tf), lambda e, f, trow, tsub: (e, 0, fcol_tail(e, f, tsub))),
                      pl.BlockSpec((1, d, tf), lambda e, f, trow, tsub: (e, 0, fcol_tail(e, f, tsub))),
                      pl.BlockSpec((1, tf, d), lambda e, f, trow, tsub: (e, fcol_tail(e, f, tsub), 0)),
                      pl.BlockSpec(memory_space=pl.ANY)],
            out_specs=pl.BlockSpec(tile, lambda e, f, trow, tsub: (trow[e], 0)),
            scratch_shapes=scratch),
        out_shape=jax.ShapeDtypeStruct(xs.shape, F32),
        input_output_aliases={6: 0},
        compiler_params=_cparams("arbitrary", "arbitrary"),
        name="expert_ffn_tail",
    )(trow, tsub, xs, wg, wu, wd, ys)


def _combine_kernel(p_ref, x_ref, route_ref, fg_ref, ys_hbm, op_ref, os_ref, gath_ref, sem, *, tm, s):
    def row_copy(src, k, t):
        return pltpu.make_async_copy(ys_hbm.at[pl.ds(pl.multiple_of(src, s), s)],
                                     gath_ref.at[k, pl.ds(pl.multiple_of(t * s, s), s)], sem)

    def issue(t, carry):
        for k in range(TOP_K):
            row_copy(p_ref[0, 0, TOP_K * t + k], k, t).start()
        return carry

    lax.fori_loop(0, tm, issue, 0, unroll=DMA_UNROLL)

    def drain(t, carry):
        for k in range(TOP_K):
            row_copy(0, 0, 0).wait()
        return carry

    lax.fori_loop(0, tm, drain, 0, unroll=DMA_UNROLL)
    route = route_ref[...]
    x = (x_ref[...] + route[:, 2:3] * _load_row_tiles(gath_ref.at[0], 0, tm, s)
         + route[:, 3:4] * _load_row_tiles(gath_ref.at[1], 0, tm, s))
    y = _rms(x, fg_ref[...])
    op_ref[...] = y

    @pl.when(pl.program_id(0) == pl.num_programs(0) - 1)
    def _():
        os_ref[...] = y[tm - os_ref.shape[0]:]


def combine(x, route, ys, pos, final_g, tm, n_sample):
    t, d = x.shape
    nt = t // tm
    s = d // LANES
    assert n_sample <= tm
    kern = functools.partial(_combine_kernel, tm=tm, s=s)
    return pl.pallas_call(
        kern,
        grid=(nt,),
        in_specs=[pl.BlockSpec((1, 1, TOP_K * tm), lambda i: (i, 0, 0), memory_space=pltpu.SMEM),
                  pl.BlockSpec((tm, d), lambda i: (i, 0)),
                  pl.BlockSpec((tm, LANES), lambda i: (i, 0)),
                  pl.BlockSpec((1, d), lambda i: (0, 0)),
                  pl.BlockSpec(memory_space=pl.ANY)],
        out_specs=[pl.BlockSpec((tm, d), lambda i: (i, 0)),
                   pl.BlockSpec((n_sample, d), lambda i: (0, 0))],
        out_shape=[jax.ShapeDtypeStruct((t - n_sample, d), F32),
                   jax.ShapeDtypeStruct((n_sample, d), F32)],
        scratch_shapes=[pltpu.VMEM((TOP_K, tm * s, LANES), F32),
                        pltpu.SemaphoreType.DMA(())],
        compiler_params=_cparams("arbitrary"),
        name="moe_combine",
    )(pos.reshape(nt, 1, TOP_K * tm), x, route, final_g.reshape(1, d), ys)


def moe_residual_final_norm(a, w_out, x_in, norm_g, router_w, wg, wu, wd, final_g, tm, n_sample):
    t, d = x_in.shape
    experts = router_w.shape[1]
    x, h, route, cnt = outproj_router(a, w_out, x_in, norm_g, router_w, tm)
    counts = cnt[0, :experts].astype(I32)
    ntiles = (TOP_K * t + experts * (MOE_ROW_TILE - 1)) // MOE_ROW_TILE + 1
    nfull_max = max(TOP_K * t // MOE_ROW_TILE, 1)
    s = d // LANES
    frow, fexp, nfull, trow, tsub, zrow, pos = plan(counts, route, ntiles, nfull_max, s)
    pos = pos[:, 4:4 + TOP_K]
    xs = dispatch(h, pos, zrow, ntiles * MOE_ROW_TILE, tm, s)
    ys = expert_ffn(xs, nfull, frow, fexp, trow, tsub, wg, wu, wd)
    return combine(x, route, ys, pos, final_g, tm, n_sample)


def _rope_tables(pos, half):
    inv = jnp.power(ROPE_THETA, -jnp.arange(half, dtype=F32) / half)
    ang = pos[:, None] * inv[None, :]
    return jnp.cos(ang), jnp.sin(ang)


def kernel(x_prompt, x_sample, state_retention, state_hgrn, norm_mix_g, norm_ffn_g, final_norm_g, ret_w_in, ret_gn_g, ret_w_out, hg_w_in, hg_lb_param, hg_norm_g, hg_w_out, ffn_w_gate, ffn_w_up, ffn_w_down, moe_router, moe_w_gate, moe_w_up, moe_w_down):
    bp, lp, d = x_prompt.shape
    bs, ls, _ = x_sample.shape
    assert ls == 1 and norm_mix_g.shape[0] == 2
    _, _, ret_heads, ret_dk, ret_dv = state_retention.shape
    _, _, hg_heads, hg_dk, _ = state_hgrn.shape
    tp = bp * lp
    t = tp + bs
    tm = _pick_tile(t, 768, 16)

    xp = x_prompt.reshape(tp, d)
    xs = x_sample.reshape(bs, d)

    log_gamma = jnp.log1p(-jnp.exp2(-5.0 - jnp.arange(ret_heads, dtype=F32)))
    cos_p, sin_p = _rope_tables(jnp.arange(lp, dtype=F32), ret_dk // 2)
    cos_s, sin_s = _rope_tables(PAST_LEN + jnp.arange(ls, dtype=F32), ret_dk // 2)
    tm_proj = _pick_tile(t, 1536, 16)
    proj = norm_matmul_merged(xp, xs, norm_mix_g[0], ret_w_in[0].astype(BF16), tm_proj,
                              _pick_tile(ret_w_in.shape[2], 1024, LANES))
    gated, ret_p = retention_scan(proj, log_gamma, cos_p, sin_p, ret_gn_g[0], batch=bp, seq=lp,
                                  heads=ret_heads, dk=ret_dk, dv=ret_dv, rows_total=t)
    gated, ret_s = retention_step(proj, gated, state_retention[0], log_gamma, cos_s, sin_s, ret_gn_g[0],
                                  row0=tp, nb=bs, heads=ret_heads, dk=ret_dk, dv=ret_dv)
    x = matmul_residual_merged(gated, ret_w_out[0].astype(BF16), xp, xs, tm)
    ff = ffn_w_gate.shape[2]
    x = ffn_residual(x, norm_ffn_g[0], ffn_w_gate[0].astype(BF16), ffn_w_up[0].astype(BF16),
                     ffn_w_down[0].astype(BF16), tm, _pick_tile(ff, 1536, LANES))

    proj = norm_matmul(x, norm_mix_g[1], hg_w_in[0].astype(BF16), tm_proj,
                       _pick_tile(hg_w_in.shape[2], 1024, LANES))
    experts, _, fe = moe_w_gate.shape[1:]
    gated, hg_p, wg, wu, wd = hgrn_scan(
        proj, hg_lb_param, hg_norm_g[0],
        (moe_w_gate[0].reshape(experts * d, fe), moe_w_up[0].reshape(experts * d, fe),
         moe_w_down[0].reshape(experts * fe, d)),
        batch=bp, seq=lp, heads=hg_heads, dk=hg_dk, rows_total=t, layer=1)
    gated, hg_s = hgrn_step(proj, gated, state_hgrn[0], hg_lb_param, hg_norm_g[0], row0=tp, nb=bs,
                            heads=hg_heads, dk=hg_dk, layer=1)
    y_p, y_s = moe_residual_final_norm(gated, hg_w_out[0].astype(BF16), x, norm_ffn_g[1], moe_router[0],
                                       wg.reshape(experts, d, fe), wu.reshape(experts, d, fe),
                                       wd.reshape(experts, fe, d), final_norm_g, tm, bs)

    return (y_p.reshape(bp, lp, d), y_s.reshape(bs, ls, d),
            ret_p[None], ret_s[None], hg_p[None], hg_s[None])
```

```python
import functools

import jax
import jax.numpy as jnp
from jax import lax
from jax.experimental import pallas as pl
from jax.experimental.pallas import tpu as pltpu

F32 = jnp.float32
BF16 = jnp.bfloat16
I32 = jnp.int32

NORM_EPS = 1e-6
GN_EPS = 1e-5
ROPE_THETA = 10000.0
PAST_LEN = 16384
TOP_K = 2

LANES = 128
VMEM_LIMIT = 56 * 1024 * 1024

RET_CHUNK = 256
HG_CHUNK = 128
HG_DIAG = 4
RET_SCAN_HEADS = 2
RET_STEP_SEQS = 2
HG_STEP_SEQS = 4
MOE_ROW_TILE = 1024
MOE_SUB = 256
MOE_F_TILE = 512
ROUTER_GROUPS = 2
DMA_UNROLL = 8

NT_DIMS = (((1,), (1,)), ((), ()))
TN_DIMS = (((0,), (0,)), ((), ()))


def _cparams(*sem):
    return pltpu.CompilerParams(dimension_semantics=sem, vmem_limit_bytes=VMEM_LIMIT)


def _pick_tile(n, target, mult):
    best = None
    for t in range(mult, min(n, target) + 1, mult):
        if n % t == 0:
            best = t
    assert best is not None, (n, target, mult)
    return best


def _sigmoid(x):
    return 0.5 * jnp.tanh(0.5 * x) + 0.5


def _silu(x):
    h = 0.5 * x
    return h * jnp.tanh(h) + h


def _rms(x, g):
    return x * lax.rsqrt(jnp.mean(x * x, axis=-1, keepdims=True) + NORM_EPS) * g


def _row_group(b):
    return pl.multiple_of(lax.shift_left(lax.shift_right_logical(b, 3), 3), 8), b & 7


def _load_row(ref, b, cols):
    base, r = _row_group(b)
    blk = ref[pl.ds(base, 8), cols]
    rows = lax.broadcasted_iota(I32, blk.shape, 0)
    return jnp.sum(jnp.where(rows == r, blk, 0.0), axis=0, keepdims=True)


def _store_row(ref, b, cols, row):
    base, r = _row_group(b)
    blk = ref[pl.ds(base, 8), cols]
    rows = lax.broadcasted_iota(I32, blk.shape, 0)
    ref[pl.ds(base, 8), cols] = jnp.where(rows == r, row, blk)


def _norm_matmul_kernel(x_ref, g_ref, w_ref, o_ref, h_ref):
    @pl.when(pl.program_id(1) == 0)
    def _():
        h_ref[...] = _rms(x_ref[...], g_ref[...]).astype(BF16)

    o_ref[...] = jnp.dot(h_ref[...], w_ref[...], preferred_element_type=F32).astype(o_ref.dtype)


def norm_matmul(x, g, w, tm, tn):
    t, d = x.shape
    n = w.shape[1]
    return pl.pallas_call(
        _norm_matmul_kernel,
        grid=(t // tm, n // tn),
        in_specs=[pl.BlockSpec((tm, d), lambda i, j: (i, 0)),
                  pl.BlockSpec((1, d), lambda i, j: (0, 0)),
                  pl.BlockSpec((d, tn), lambda i, j: (0, j))],
        out_specs=pl.BlockSpec((tm, tn), lambda i, j: (i, j)),
        out_shape=jax.ShapeDtypeStruct((t, n), BF16),
        scratch_shapes=[pltpu.VMEM((tm, d), BF16)],
        compiler_params=_cparams("parallel", "arbitrary"),
        name="norm_matmul",
    )(x, g.reshape(1, d), w)


def _with_merged_rows(xp_ref, xs_ref, fn):
    i = pl.program_id(0)
    last = pl.num_programs(0) - 1
    n_prompt = xp_ref.shape[0] - xs_ref.shape[0]

    @pl.when(i < last)
    def _():
        fn(xp_ref[...])

    @pl.when(i == last)
    def _():
        fn(jnp.concatenate([xp_ref[:n_prompt], xs_ref[...]], axis=0))


def _merged_specs(tm, d, bs, nidx):
    if nidx == 1:
        return [pl.BlockSpec((tm, d), lambda i: (i, 0)), pl.BlockSpec((bs, d), lambda i: (0, 0))]
    return [pl.BlockSpec((tm, d), lambda i, j: (i, 0)), pl.BlockSpec((bs, d), lambda i, j: (0, 0))]


def _norm_matmul2_kernel(xp_ref, xs_ref, g_ref, w_ref, o_ref, h_ref):
    @pl.when(pl.program_id(1) == 0)
    def _():
        def fill(x):
            h_ref[...] = _rms(x, g_ref[...]).astype(BF16)

        _with_merged_rows(xp_ref, xs_ref, fill)

    o_ref[...] = jnp.dot(h_ref[...], w_ref[...], preferred_element_type=F32).astype(o_ref.dtype)


def norm_matmul_merged(xp, xs, g, w, tm, tn):
    tp, d = xp.shape
    bs = xs.shape[0]
    t = tp + bs
    n = w.shape[1]
    assert t % tm == 0 and bs <= tm
    return pl.pallas_call(
        _norm_matmul2_kernel,
        grid=(t // tm, n // tn),
        in_specs=_merged_specs(tm, d, bs, 2) + [pl.BlockSpec((1, d), lambda i, j: (0, 0)),
                                                 pl.BlockSpec((d, tn), lambda i, j: (0, j))],
        out_specs=pl.BlockSpec((tm, tn), lambda i, j: (i, j)),
        out_shape=jax.ShapeDtypeStruct((t, n), BF16),
        scratch_shapes=[pltpu.VMEM((tm, d), BF16)],
        compiler_params=_cparams("parallel", "arbitrary"),
        name="norm_matmul_merged",
    )(xp, xs, g.reshape(1, d), w)


def _matmul_res2_kernel(a_ref, w_ref, xp_ref, xs_ref, o_ref):
    def fill(x):
        o_ref[...] = x + jnp.dot(a_ref[...], w_ref[...], preferred_element_type=F32)

    _with_merged_rows(xp_ref, xs_ref, fill)


def matmul_residual_merged(a, w, xp, xs, tm):
    t, k = a.shape
    n = w.shape[1]
    bs = xs.shape[0]
    assert t % tm == 0 and bs <= tm and xp.shape[0] + bs == t
    return pl.pallas_call(
        _matmul_res2_kernel,
        grid=(t // tm,),
        in_specs=[pl.BlockSpec((tm, k), lambda i: (i, 0)),
                  pl.BlockSpec((k, n), lambda i: (0, 0))] + _merged_specs(tm, n, bs, 1),
        out_specs=pl.BlockSpec((tm, n), lambda i: (i, 0)),
        out_shape=jax.ShapeDtypeStruct((t, n), F32),
        compiler_params=_cparams("parallel"),
        name="matmul_residual_merged",
    )(a, w, xp, xs)


def _ffn_kernel(x_ref, g_ref, wg_ref, wu_ref, wd_ref, o_ref, h_ref):
    f = pl.program_id(1)

    @pl.when(f == 0)
    def _():
        x = x_ref[...]
        h_ref[...] = _rms(x, g_ref[...]).astype(BF16)
        o_ref[...] = x

    h = h_ref[...]
    a = jnp.dot(h, wg_ref[...], preferred_element_type=F32)
    u = jnp.dot(h, wu_ref[...], preferred_element_type=F32)
    act = (_silu(a) * u).astype(BF16)
    o_ref[...] += jnp.dot(act, wd_ref[...], preferred_element_type=F32)


def ffn_residual(x, g, wg, wu, wd, tm, tf):
    t, d = x.shape
    f = wg.shape[1]
    return pl.pallas_call(
        _ffn_kernel,
        grid=(t // tm, f // tf),
        in_specs=[pl.BlockSpec((tm, d), lambda i, j: (i, 0)),
                  pl.BlockSpec((1, d), lambda i, j: (0, 0)),
                  pl.BlockSpec((d, tf), lambda i, j: (0, j)),
                  pl.BlockSpec((d, tf), lambda i, j: (0, j)),
                  pl.BlockSpec((tf, d), lambda i, j: (j, 0))],
        out_specs=pl.BlockSpec((tm, d), lambda i, j: (i, 0)),
        out_shape=jax.ShapeDtypeStruct((t, d), F32),
        scratch_shapes=[pltpu.VMEM((tm, d), BF16)],
        compiler_params=_cparams("parallel", "arbitrary"),
        name="ffn_residual",
    )(x, g.reshape(1, d), wg, wu, wd)


def _rotary(x, cos, sin, half):
    x1 = x[:, :half]
    x2 = x[:, half:]
    return jnp.concatenate([x1 * cos - x2 * sin, x2 * cos + x1 * sin], axis=-1)


def _group_norm_gate(o, gate, gain):
    mu = jnp.mean(o, axis=-1, keepdims=True)
    d = o - mu
    var = jnp.mean(d * d, axis=-1, keepdims=True)
    return _silu(gate) * (d * lax.rsqrt(var + GN_EPS) * gain)


def _ret_scan_kernel(lg_ref, q_ref, k_ref, v_ref, g_ref, cos_ref, sin_ref, gn_ref,
                     o_ref, sfin_ref, s_ref, mask_ref, din_ref, dout_ref, *, chunk, dk, dv, hp):
    c = pl.program_id(2)
    head0 = pl.program_id(1) * hp

    @pl.when(c == 0)
    def _():
        s_ref[...] = jnp.zeros_like(s_ref)
        t = lax.broadcasted_iota(I32, (chunk, LANES), 0).astype(F32)
        ti = lax.broadcasted_iota(I32, (chunk, chunk), 0)
        si = lax.broadcasted_iota(I32, (chunk, chunk), 1)
        causal = ti >= si
        rel = jnp.where(causal, (ti - si).astype(F32), 0.0)
        for j in range(hp):
            lg = lg_ref[head0 + j]
            din_ref[j] = jnp.exp((t + 1.0) * lg)
            dout_ref[j] = jnp.exp((chunk - 1.0 - t) * lg)
            mask_ref[j] = jnp.where(causal, jnp.exp(rel * lg), 0.0)

    half = dk // 2
    cos = cos_ref[...]
    sin = sin_ref[...]
    for j in range(hp):
        qk_cols = slice(j * dk, (j + 1) * dk)
        v_cols = slice(j * dv, (j + 1) * dv)
        q = _rotary(q_ref[:, qk_cols].astype(F32), cos, sin, half)
        k = _rotary(k_ref[:, qk_cols].astype(F32), cos, sin, half) * (dk ** -0.5)
        v = v_ref[:, v_cols]
        decay_in = _lane_tile(din_ref[j], dk)
        decay_out = _lane_tile(dout_ref[j], dk)
        decay_chunk = jnp.exp(jnp.zeros((1, dv), F32) + chunk * lg_ref[head0 + j])

        s = s_ref[j]
        scores = (lax.dot_general(q.astype(BF16), k.astype(BF16), NT_DIMS, preferred_element_type=F32)
                  * mask_ref[j])
        o = (jnp.dot(scores.astype(BF16), v, preferred_element_type=F32)
             + jnp.dot((q * decay_in).astype(BF16), s.astype(BF16), preferred_element_type=F32))
        kd_t = (k * decay_out).T.astype(BF16)
        s_ref[j] = s * decay_chunk + jnp.dot(kd_t, v, preferred_element_type=F32)
        o_ref[:, v_cols] = _group_norm_gate(o, g_ref[:, v_cols].astype(F32), gn_ref[:, v_cols]).astype(BF16)

    @pl.when(c == pl.num_programs(2) - 1)
    def _():
        sfin_ref[0] = s_ref[...]


def retention_scan(proj, log_gamma, cos, sin, gn_g, *, batch, seq, heads, dk, dv, rows_total):
    chunk = min(RET_CHUNK, seq)
    nc = seq // chunk
    hp = RET_SCAN_HEADS
    assert heads % hp == 0
    ng = heads // hp
    kern = functools.partial(_ret_scan_kernel, chunk=chunk, dk=dk, dv=dv, hp=hp)
    v0 = 2 * heads * dk // (hp * dv)
    return pl.pallas_call(
        kern,
        grid=(batch, ng, nc),
        in_specs=[pl.BlockSpec(memory_space=pltpu.SMEM),
                  pl.BlockSpec((chunk, hp * dk), lambda b, h, c: (b * nc + c, h)),
                  pl.BlockSpec((chunk, hp * dk), lambda b, h, c: (b * nc + c, ng + h)),
                  pl.BlockSpec((chunk, hp * dv), lambda b, h, c: (b * nc + c, v0 + h)),
                  pl.BlockSpec((chunk, hp * dv), lambda b, h, c: (b * nc + c, v0 + ng + h)),
                  pl.BlockSpec((chunk, dk // 2), lambda b, h, c: (c, 0)),
                  pl.BlockSpec((chunk, dk // 2), lambda b, h, c: (c, 0)),
                  pl.BlockSpec((1, hp * dv), lambda b, h, c: (0, h))],
        out_specs=[pl.BlockSpec((chunk, hp * dv), lambda b, h, c: (b * nc + c, h)),
                   pl.BlockSpec((1, hp, dk, dv), lambda b, h, c: (b, h, 0, 0))],
        out_shape=[jax.ShapeDtypeStruct((rows_total, heads * dv), BF16),
                   jax.ShapeDtypeStruct((batch, heads, dk, dv), F32)],
        scratch_shapes=[pltpu.VMEM((hp, dk, dv), F32),
                        pltpu.VMEM((hp, chunk, chunk), F32),
                        pltpu.VMEM((hp, chunk, LANES), F32),
                        pltpu.VMEM((hp, chunk, LANES), F32)],
        compiler_params=_cparams("parallel", "parallel", "arbitrary"),
        name="retention_scan",
    )(log_gamma, proj, proj, proj, proj, cos, sin, gn_g.reshape(1, heads * dv))


def _one_hot_rows(nb, b):
    return jnp.where(lax.broadcasted_iota(I32, (nb, LANES), 0) == b, 1.0, 0.0).astype(BF16)


def _lane_tile(x, width):
    return jnp.concatenate([x] * (width // x.shape[1]), axis=1)


def _ret_step_kernel(lg_ref, q_ref, k_ref, v_ref, g_ref, cos_ref, sin_ref, gn_ref, s_ref, prev_ref,
                     o_ref, snew_ref, qt_ref, kt_ref, vf_ref, gf_ref, orow_ref, *, heads, dk, dv, nb, per_step):
    del prev_ref
    step = pl.program_id(0)
    half = dk // 2

    @pl.when(step == 0)
    def _():
        orow_ref[...] = jnp.zeros_like(orow_ref)
        vf_ref[...] = v_ref[...].astype(F32)
        gf_ref[...] = g_ref[...].astype(F32)
        cos = cos_ref[...]
        sin = sin_ref[...]
        for h in range(heads):
            qh = _rotary(q_ref[:, h * dk:(h + 1) * dk].astype(F32), cos, sin, half)
            kh = _rotary(k_ref[:, h * dk:(h + 1) * dk].astype(F32), cos, sin, half) * (dk ** -0.5)
            qt_ref[h] = qh.T.astype(BF16)
            kt_ref[h] = kh.T.astype(BF16)

    for i in range(per_step):
        b = step * per_step + i
        onehot = _one_hot_rows(nb, b)
        for h in range(heads):
            cols = slice(h * dv, (h + 1) * dv)
            kcol = _lane_tile(jnp.dot(kt_ref[h], onehot, preferred_element_type=F32), dv)
            qcol = _lane_tile(jnp.dot(qt_ref[h], onehot, preferred_element_type=F32), dv)
            vrow = _load_row(vf_ref, b, cols)
            gamma = jnp.exp(jnp.zeros((1, dv), F32) + lg_ref[h])
            s_new = s_ref[i, h] * gamma + kcol * vrow
            snew_ref[i, h] = s_new
            o = jnp.sum(s_new * qcol, axis=0, keepdims=True)
            _store_row(orow_ref, b, cols, _group_norm_gate(o, _load_row(gf_ref, b, cols), gn_ref[:, cols]))

    @pl.when(step == pl.num_programs(0) - 1)
    def _():
        o_ref[...] = orow_ref[...].astype(BF16)


def retention_step(proj, gated, state, log_gamma, cos, sin, gn_g, *, row0, nb, heads, dk, dv):
    per_step = RET_STEP_SEQS
    kern = functools.partial(_ret_step_kernel, heads=heads, dk=dk, dv=dv, nb=nb, per_step=per_step)
    rb = row0 // nb
    wq = heads * dk
    wv = heads * dv
    return pl.pallas_call(
        kern,
        grid=(nb // per_step,),
        in_specs=[pl.BlockSpec(memory_space=pltpu.SMEM),
                  pl.BlockSpec((nb, wq), lambda b: (rb, 0)),
                  pl.BlockSpec((nb, wq), lambda b: (rb, 1)),
                  pl.BlockSpec((nb, wv), lambda b: (rb, 2 * wq // wv)),
                  pl.BlockSpec((nb, wv), lambda b: (rb, 2 * wq // wv + 1)),
                  pl.BlockSpec((1, dk // 2), lambda b: (0, 0)),
                  pl.BlockSpec((1, dk // 2), lambda b: (0, 0)),
                  pl.BlockSpec((1, wv), lambda b: (0, 0)),
                  pl.BlockSpec((per_step, heads, dk, dv), lambda b: (b, 0, 0, 0)),
                  pl.BlockSpec(memory_space=pl.ANY)],
        out_specs=[pl.BlockSpec((nb, wv), lambda b: (rb, 0)),
                   pl.BlockSpec((per_step, heads, dk, dv), lambda b: (b, 0, 0, 0))],
        out_shape=[jax.ShapeDtypeStruct(gated.shape, gated.dtype),
                   jax.ShapeDtypeStruct(state.shape, F32)],
        scratch_shapes=[pltpu.VMEM((heads, dk, nb), BF16),
                        pltpu.VMEM((heads, dk, nb), BF16),
                        pltpu.VMEM((nb, wv), F32),
                        pltpu.VMEM((nb, wv), F32),
                        pltpu.VMEM((nb, wv), F32)],
        input_output_aliases={9: 0},
        compiler_params=_cparams("arbitrary"),
        name="retention_step",
    )(log_gamma, proj, proj, proj, proj, cos, sin, gn_g.reshape(1, wv), state, gated)


def _lower_bound(lbp, layer):
    m = jnp.max(lbp, axis=0, keepdims=True)
    e = jnp.exp(lbp - m)
    p = e / jnp.sum(e, axis=0, keepdims=True)
    return jnp.sum(p[:layer + 1], axis=0, keepdims=True) - p[0:1]


def _split_dot(mat_bf16, x):
    hi = x.astype(BF16)
    r1 = x - hi.astype(F32)
    mid = r1.astype(BF16)
    lo = (r1 - mid.astype(F32)).astype(BF16)
    return (jnp.dot(mat_bf16, hi, preferred_element_type=F32)
            + jnp.dot(mat_bf16, mid, preferred_element_type=F32)
            + jnp.dot(mat_bf16, lo, preferred_element_type=F32))


def _rms_gate(o, gate, gain):
    return _silu(gate) * (o * lax.rsqrt(jnp.mean(o * o, axis=-1, keepdims=True) + NORM_EPS) * gain)


def _group_row(x, s, group):
    n, w = x.shape
    x3 = x.reshape(n // group, group, w)
    return jnp.broadcast_to(x3[:, s:s + 1, :], x3.shape).reshape(n, w)


def _hgrn_scan_kernel(*refs, chunk, heads, dk, diag, layer, ncast):
    lbp_ref, q_ref, f_ref, i_ref, g_ref, ng_ref = refs[:6]
    cast_in = refs[6:6 + ncast]
    o_ref, sfin_ref = refs[6 + ncast:8 + ncast]
    cast_out = refs[8 + ncast:8 + 2 * ncast]
    st_ref = refs[8 + 2 * ncast]
    c = pl.program_id(1)

    @pl.when(c == 0)
    def _():
        st_ref[...] = jnp.zeros_like(st_ref)

    for src, dst in zip(cast_in, cast_out):
        dst[...] = src[...].astype(BF16)

    w = heads * dk
    lb = _lower_bound(lbp_ref[...], layer)
    qs = _silu(q_ref[...].astype(F32))
    forget = lb + (1.0 - lb) * _sigmoid(f_ref[...].astype(F32))
    kk = 1.0 - forget
    logf = jnp.log(forget)

    row = lax.broadcasted_iota(I32, (chunk, chunk), 0)
    col = lax.broadcasted_iota(I32, (chunk, chunk), 1)
    tril = jnp.where(row >= col, 1.0, 0.0).astype(BF16)
    bcum = _split_dot(tril, logf)
    blast = bcum[chunk - 1:chunk, :]
    qe = (qs * jnp.exp(bcum)).astype(BF16)
    kdec = (kk * jnp.exp(blast - bcum)).astype(BF16)
    lk = jnp.log(kk) - bcum
    iv = i_ref[...]
    gate = g_ref[...].astype(F32)
    gain = ng_ref[...]

    u = jnp.where(row > col, row ^ col, 0)
    rowv = lax.broadcasted_iota(I32, (chunk, 1), 0)
    levels = []
    m = chunk // 2
    while m >= diag:
        span = 2 * m
        bref = jnp.concatenate(
            [jnp.broadcast_to(bcum[g * span + m - 1:g * span + m], (span, w)) for g in range(chunk // span)],
            axis=0)
        upper = (rowv & m) != 0
        d = bcum - bref
        x = (jnp.where(upper, qs, kk) * jnp.exp(jnp.where(upper, d, -d))).astype(BF16)
        levels.append((x, lax.shift_right_logical(u, m.bit_length() - 1) == 1))
        m //= 2
    in_block = row & (diag - 1)
    block_col0 = row - in_block
    diag_masks = [(col == block_col0 + s) & (in_block >= s) for s in range(diag)]
    ones = jnp.ones((dk, LANES), BF16)

    for h in range(heads):
        cols = slice(h * dk, (h + 1) * dk)
        qs_h = qs[:, cols]
        lk_h = lk[:, cols]
        b_h = bcum[:, cols]
        a = jnp.zeros((chunk, chunk), F32)
        for x, mask in levels:
            x_h = x[:, cols]
            a = jnp.where(mask, lax.dot_general(x_h, x_h, NT_DIMS, preferred_element_type=F32), a)
        ws = [qs_h * jnp.exp(b_h + _group_row(lk_h, s, diag)) for s in range(diag)]
        rsum = jnp.dot(jnp.concatenate(ws, axis=0).astype(BF16), ones, preferred_element_type=F32)
        for s in range(diag):
            a = jnp.where(diag_masks[s], rsum[s * chunk:(s + 1) * chunk], a)
        st = st_ref[h]
        i_h = iv[:, cols]
        o_h = (jnp.dot(a.astype(BF16), i_h, preferred_element_type=F32)
               + lax.dot_general(qe[:, cols], st.astype(BF16), NT_DIMS, preferred_element_type=F32))
        st_new = (st * jnp.exp(blast[:, cols])
                  + lax.dot_general(i_h, kdec[:, cols], TN_DIMS, preferred_element_type=F32))
        st_ref[h] = st_new
        o_ref[:, cols] = _rms_gate(o_h, gate[:, cols], gain[:, cols]).astype(BF16)

    @pl.when(c == pl.num_programs(1) - 1)
    def _():
        for h in range(heads):
            sfin_ref[0, h] = st_ref[h].T


def hgrn_scan(proj, lb_param, norm_g, side_f32, *, batch, seq, heads, dk, rows_total, layer):
    chunk = min(HG_CHUNK, seq)
    assert chunk == LANES and dk == LANES
    nc = seq // chunk
    w = heads * dk
    steps = batch * nc
    slabs = []
    for a in side_f32:
        assert a.shape[0] % (steps * 16) == 0, (a.shape, steps)
        slabs.append(pl.BlockSpec((a.shape[0] // steps, a.shape[1]), lambda b, c: (b * nc + c, 0)))
    kern = functools.partial(_hgrn_scan_kernel, chunk=chunk, heads=heads, dk=dk, diag=HG_DIAG, layer=layer,
                             ncast=len(side_f32))
    return pl.pallas_call(
        kern,
        grid=(batch, nc),
        in_specs=[pl.BlockSpec(lb_param.shape, lambda b, c: (0, 0)),
                  pl.BlockSpec((chunk, w), lambda b, c: (b * nc + c, 0)),
                  pl.BlockSpec((chunk, w), lambda b, c: (b * nc + c, 1)),
                  pl.BlockSpec((chunk, w), lambda b, c: (b * nc + c, 2)),
                  pl.BlockSpec((chunk, w), lambda b, c: (b * nc + c, 3)),
                  pl.BlockSpec((1, w), lambda b, c: (0, 0))] + slabs,
        out_specs=[pl.BlockSpec((chunk, w), lambda b, c: (b * nc + c, 0)),
                   pl.BlockSpec((1, heads, dk, dk), lambda b, c: (b, 0, 0, 0))] + slabs,
        out_shape=[jax.ShapeDtypeStruct((rows_total, w), BF16),
                   jax.ShapeDtypeStruct((batch, heads, dk, dk), F32)]
                  + [jax.ShapeDtypeStruct(a.shape, BF16) for a in side_f32],
        scratch_shapes=[pltpu.VMEM((heads, dk, dk), F32)],
        compiler_params=_cparams("parallel", "arbitrary"),
        name="hgrn_scan",
    )(lb_param, proj, proj, proj, proj, norm_g.reshape(1, w), *side_f32)


def _hgrn_step_kernel(lbp_ref, q_ref, f_ref, i_ref, g_ref, ng_ref, s_ref, prev_ref,
                      o_ref, snew_ref, qt_ref, ft_ref, if_ref, gf_ref, orow_ref,
                      *, heads, dk, nb, layer, per_step):
    del prev_ref
    step = pl.program_id(0)

    @pl.when(step == 0)
    def _():
        orow_ref[...] = jnp.zeros_like(orow_ref)
        if_ref[...] = i_ref[...].astype(F32)
        gf_ref[...] = g_ref[...].astype(F32)
        lb = _lower_bound(lbp_ref[...], layer)
        qs = _silu(q_ref[...].astype(F32))
        forget = lb + (1.0 - lb) * _sigmoid(f_ref[...].astype(F32))
        for h in range(heads):
            cols = slice(h * dk, (h + 1) * dk)
            qt_ref[h] = qs[:, cols].T.astype(BF16)
            ft = forget[:, cols].T
            hi = ft.astype(BF16)
            r1 = ft - hi.astype(F32)
            mid = r1.astype(BF16)
            ft_ref[0, h] = hi
            ft_ref[1, h] = mid
            ft_ref[2, h] = (r1 - mid.astype(F32)).astype(BF16)

    for i in range(per_step):
        b = step * per_step + i
        onehot = _one_hot_rows(nb, b)
        for h in range(heads):
            cols = slice(h * dk, (h + 1) * dk)
            fcol = (jnp.dot(ft_ref[0, h], onehot, preferred_element_type=F32)
                    + jnp.dot(ft_ref[1, h], onehot, preferred_element_type=F32)
                    + jnp.dot(ft_ref[2, h], onehot, preferred_element_type=F32))
            qcol = jnp.dot(qt_ref[h], onehot, preferred_element_type=F32)
            irow = _load_row(if_ref, b, cols)
            s_new = s_ref[i, h] * fcol + (1.0 - fcol) * irow
            snew_ref[i, h] = s_new
            o = jnp.sum(s_new * qcol, axis=0, keepdims=True)
            _store_row(orow_ref, b, cols, _rms_gate(o, _load_row(gf_ref, b, cols), ng_ref[:, cols]))

    @pl.when(step == pl.num_programs(0) - 1)
    def _():
        o_ref[...] = orow_ref[...].astype(BF16)


def hgrn_step(proj, gated, state, lb_param, norm_g, *, row0, nb, heads, dk, layer):
    per_step = HG_STEP_SEQS
    kern = functools.partial(_hgrn_step_kernel, heads=heads, dk=dk, nb=nb, layer=layer, per_step=per_step)
    rb = row0 // nb
    w = heads * dk
    return pl.pallas_call(
        kern,
        grid=(nb // per_step,),
        in_specs=[pl.BlockSpec(lb_param.shape, lambda b: (0, 0)),
                  pl.BlockSpec((nb, w), lambda b: (rb, 0)),
                  pl.BlockSpec((nb, w), lambda b: (rb, 1)),
                  pl.BlockSpec((nb, w), lambda b: (rb, 2)),
                  pl.BlockSpec((nb, w), lambda b: (rb, 3)),
                  pl.BlockSpec((1, w), lambda b: (0, 0)),
                  pl.BlockSpec((per_step, heads, dk, dk), lambda b: (b, 0, 0, 0)),
                  pl.BlockSpec(memory_space=pl.ANY)],
        out_specs=[pl.BlockSpec((nb, w), lambda b: (rb, 0)),
                   pl.BlockSpec((per_step, heads, dk, dk), lambda b: (b, 0, 0, 0))],
        out_shape=[jax.ShapeDtypeStruct(gated.shape, gated.dtype),
                   jax.ShapeDtypeStruct(state.shape, F32)],
        scratch_shapes=[pltpu.VMEM((heads, dk, nb), BF16),
                        pltpu.VMEM((3, heads, dk, nb), BF16),
                        pltpu.VMEM((nb, w), F32),
                        pltpu.VMEM((nb, w), F32),
                        pltpu.VMEM((nb, w), F32)],
        input_output_aliases={7: 0},
        compiler_params=_cparams("arbitrary"),
        name="hgrn_step",
    )(lb_param, proj, proj, proj, proj, norm_g.reshape(1, w), state, gated)


def _store_row_tiles(ref, r0, x):
    n, d = x.shape
    s = d // LANES
    for c in range(s):
        ref[pl.ds(s * r0 + c, n, stride=s), :] = x[:, c * LANES:(c + 1) * LANES]


def _load_row_tiles(ref, r0, n, s):
    return jnp.concatenate([ref[pl.ds(s * r0 + c, n, stride=s), :] for c in range(s)], axis=1)


def _router_kernel(a_ref, w_ref, x_ref, g_ref, r_ref, xo_ref, h_ref, route_ref, cnt_ref,
                   carry_ref, rhi_ref, rlo_ref, *, tm, experts):
    i = pl.program_id(0)

    @pl.when(i == 0)
    def _():
        carry_ref[...] = jnp.zeros_like(carry_ref)
        r = r_ref[...]
        rhi = r.astype(BF16)
        rhi_ref[...] = rhi
        rlo_ref[...] = (r - rhi.astype(F32)).astype(BF16)

    step = (tm // ROUTER_GROUPS + 15) // 16 * 16
    groups = tuple((r, min(r + step, tm)) for r in range(0, tm, step))
    all_logits = []
    for r0, r1 in groups:
        rows = slice(r0, r1)
        x = x_ref[rows] + jnp.dot(a_ref[rows], w_ref[...], preferred_element_type=F32)
        xo_ref[rows] = x
        h = _rms(x, g_ref[...])
        _store_row_tiles(h_ref, r0, h)
        hhi = h.astype(BF16)
        hlo = (h - hhi.astype(F32)).astype(BF16)
        all_logits.append(jnp.dot(hhi, rhi_ref[...], preferred_element_type=F32)
                          + jnp.dot(hlo, rhi_ref[...], preferred_element_type=F32)
                          + jnp.dot(hhi, rlo_ref[...], preferred_element_type=F32))
    total = carry_ref[...]
    for (r0, r1), logits in zip(groups, all_logits):
        n = r1 - r0
        rows = slice(r0, r1)
        lane = lax.broadcasted_iota(I32, (n, LANES), 1)
        valid = lane < experts
        z = jnp.where(valid, logits, -jnp.inf)
        ez = jnp.exp(z - jnp.max(z, axis=-1, keepdims=True))
        p = ez / jnp.sum(ez, axis=-1, keepdims=True)
        p = jnp.where(valid, p, -1.0)
        lane_f = lane.astype(F32)
        v1 = jnp.max(p, axis=-1, keepdims=True)
        i1 = jnp.min(jnp.where(p == v1, lane_f, float(LANES)), axis=-1, keepdims=True)
        p2 = jnp.where(lane_f == i1, -1.0, p)
        v2 = jnp.max(p2, axis=-1, keepdims=True)
        i2 = jnp.min(jnp.where(p2 == v2, lane_f, float(LANES)), axis=-1, keepdims=True)
        den = v1 + v2
        m0 = jnp.where(lane_f == i1, 1.0, 0.0)
        m1 = jnp.where(lane_f == i2, 1.0, 0.0)
        msum = m0 + m1
        ri = lax.broadcasted_iota(I32, (n, n), 0)
        ci = lax.broadcasted_iota(I32, (n, n), 1)
        strict = jnp.where(ri > ci, 1.0, 0.0).astype(BF16)
        before = jnp.dot(strict, msum.astype(BF16), preferred_element_type=F32) + total
        rank0 = jnp.sum(m0 * before, axis=-1, keepdims=True)
        rank1 = jnp.sum(m1 * before, axis=-1, keepdims=True)
        total = total + jnp.sum(msum, axis=0, keepdims=True)
        route_ref[rows] = jnp.where(lane == 0, i1,
                          jnp.where(lane == 1, i2,
                          jnp.where(lane == 2, v1 / den,
                          jnp.where(lane == 3, v2 / den,
                          jnp.where(lane == 4, rank0,
                          jnp.where(lane == 5, rank1, 0.0))))))
    carry_ref[...] = total
    cnt_ref[...] = jnp.broadcast_to(total, cnt_ref.shape)


def outproj_router(a, w, x, g, router_w, tm):
    t, d = x.shape
    k = a.shape[1]
    experts = router_w.shape[1]
    rpad = jnp.zeros((d, LANES), F32).at[:, :experts].set(router_w)
    kern = functools.partial(_router_kernel, tm=tm, experts=experts)
    return pl.pallas_call(
        kern,
        grid=(t // tm,),
        in_specs=[pl.BlockSpec((tm, k), lambda i: (i, 0)),
                  pl.BlockSpec((k, d), lambda i: (0, 0)),
                  pl.BlockSpec((tm, d), lambda i: (i, 0)),
                  pl.BlockSpec((1, d), lambda i: (0, 0)),
                  pl.BlockSpec((d, LANES), lambda i: (0, 0))],
        out_specs=[pl.BlockSpec((tm, d), lambda i: (i, 0)),
                   pl.BlockSpec((tm * (d // LANES), LANES), lambda i: (i, 0)),
                   pl.BlockSpec((tm, LANES), lambda i: (i, 0)),
                   pl.BlockSpec((8, LANES), lambda i: (0, 0))],
        out_shape=[jax.ShapeDtypeStruct((t, d), F32),
                   jax.ShapeDtypeStruct((t * (d // LANES), LANES), F32),
                   jax.ShapeDtypeStruct((t, LANES), F32),
                   jax.ShapeDtypeStruct((8, LANES), F32)],
        scratch_shapes=[pltpu.VMEM((1, LANES), F32),
                        pltpu.VMEM((d, LANES), BF16),
                        pltpu.VMEM((d, LANES), BF16)],
        compiler_params=_cparams("arbitrary"),
        name="outproj_router",
    )(a, w, x, g.reshape(1, d), rpad)


def _plan_kernel(cnt_ref, route_ref, frow_ref, fexp_ref, nfull_ref, trow_ref, tsub_ref, zrow_ref, pos_ref,
                 *, experts, nfull_max, dummy, row_tile, sub, row_scale):
    tile0 = jnp.int32(0)
    nfull = jnp.int32(0)
    last = jnp.int32(0)
    route = route_ref[...]
    slot_expert = pltpu.roll(route, 4, axis=1)
    first_row = jnp.zeros_like(route)
    for e in range(experts):
        n = cnt_ref[e]
        first_row = jnp.where(slot_expert == float(e), (tile0 * row_tile).astype(F32), first_row)
        full = n // row_tile
        rem = n - full * row_tile

        def fill(j, carry, e=e, tile0=tile0, nfull=nfull):
            frow_ref[nfull + j] = tile0 + j
            fexp_ref[nfull + j] = e
            return carry

        lax.fori_loop(0, full, fill, 0)
        trow_ref[e] = jnp.where(rem > 0, tile0 + full, dummy)
        tsub_ref[e] = (rem + (sub - 1)) // sub
        zrow_ref[e] = tile0 * row_tile + (n // sub) * sub
        last = jnp.where(full > 0, e, last)
        nfull = nfull + full
        tile0 = tile0 + full + jnp.where(rem > 0, 1, 0)
    nfull_ref[0] = nfull
    pos_ref[...] = ((first_row + route) * float(row_scale)).astype(I32)

    def unused(j, carry):
        frow_ref[j] = dummy
        fexp_ref[j] = last
        return carry

    lax.fori_loop(nfull, nfull_max, unused, 0)


def plan(counts, route, ntiles, nfull_max, row_scale):
    experts = counts.shape[0]
    kern = functools.partial(_plan_kernel, experts=experts, nfull_max=nfull_max, dummy=ntiles - 1,
                             row_tile=MOE_ROW_TILE, sub=MOE_SUB, row_scale=row_scale)
    smem = pl.BlockSpec(memory_space=pltpu.SMEM)
    vmem = pl.BlockSpec(memory_space=pltpu.VMEM)
    return pl.pallas_call(
        kern,
        in_specs=[smem, vmem],
        out_specs=[smem] * 6 + [vmem],
        out_shape=[jax.ShapeDtypeStruct((nfull_max,), I32),
                   jax.ShapeDtypeStruct((nfull_max,), I32),
                   jax.ShapeDtypeStruct((1,), I32),
                   jax.ShapeDtypeStruct((experts,), I32),
                   jax.ShapeDtypeStruct((experts,), I32),
                   jax.ShapeDtypeStruct((experts,), I32),
                   jax.ShapeDtypeStruct(route.shape, I32)],
        compiler_params=pltpu.CompilerParams(vmem_limit_bytes=VMEM_LIMIT),
        name="moe_plan",
    )(counts, route)


def _dispatch_kernel(zrow_ref, p_ref, h_ref, out_hbm, zero_ref, sem, zsem, *, tm, experts, s):
    @pl.when(pl.program_id(0) == 0)
    def _():
        zero_ref[...] = jnp.zeros_like(zero_ref)

        def zero_copy(e):
            row = pl.multiple_of(zrow_ref[e] * s, MOE_SUB * s)
            return pltpu.make_async_copy(zero_ref, out_hbm.at[pl.ds(row, MOE_SUB * s)], zsem)

        for e in range(experts):
            zero_copy(e).start()
        for e in range(experts):
            zero_copy(e).wait()

    def row_copy(t, dst):
        return pltpu.make_async_copy(h_ref.at[pl.ds(pl.multiple_of(t * s, s), s)],
                                     out_hbm.at[pl.ds(pl.multiple_of(dst, s), s)], sem)

    def issue(t, carry):
        for k in range(TOP_K):
            row_copy(t, p_ref[0, 0, TOP_K * t + k]).start(priority=k)
        return carry

    lax.fori_loop(0, tm, issue, 0, unroll=DMA_UNROLL)

    def drain(t, carry):
        for k in range(TOP_K):
            row_copy(0, 0).wait()
        return carry

    lax.fori_loop(0, tm, drain, 0, unroll=DMA_UNROLL)


def dispatch(h, pos, zrow, rows, tm, s):
    nt = h.shape[0] // (tm * s)
    kern = functools.partial(_dispatch_kernel, tm=tm, experts=zrow.shape[0], s=s)
    return pl.pallas_call(
        kern,
        grid_spec=pltpu.PrefetchScalarGridSpec(
            num_scalar_prefetch=1,
            grid=(nt,),
            in_specs=[pl.BlockSpec((1, 1, TOP_K * tm), lambda i, zrow: (i, 0, 0), memory_space=pltpu.SMEM),
                      pl.BlockSpec((tm * s, LANES), lambda i, zrow: (i, 0))],
            out_specs=pl.BlockSpec(memory_space=pl.ANY),
            scratch_shapes=[pltpu.VMEM((MOE_SUB * s, LANES), h.dtype),
                            pltpu.SemaphoreType.DMA(()),
                            pltpu.SemaphoreType.DMA(())]),
        out_shape=jax.ShapeDtypeStruct((rows * s, LANES), h.dtype),
        compiler_params=_cparams("arbitrary"),
        name="moe_dispatch",
    )(zrow, pos.reshape(nt, 1, TOP_K * tm), h)


def _expert_full_kernel(nfull_ref, frow_ref, fexp_ref, x_ref, wg_ref, wu_ref, wd_ref, o_ref, h_ref, acc_ref,
                        *, s):
    del frow_ref, fexp_ref
    j = pl.program_id(0)
    f = pl.program_id(1)
    active = j < nfull_ref[0]
    rows = h_ref.shape[0]

    @pl.when(jnp.logical_not(active) & (f == 0))
    def _():
        o_ref[...] = jnp.zeros_like(o_ref)

    @pl.when(active & (f == 0))
    def _():
        h_ref[...] = _load_row_tiles(x_ref, 0, rows, s).astype(BF16)
        acc_ref[...] = jnp.zeros_like(acc_ref)

    @pl.when(active)
    def _():
        h = h_ref[...]
        a = jnp.dot(h, wg_ref[0], preferred_element_type=F32)
        u = jnp.dot(h, wu_ref[0], preferred_element_type=F32)
        act = (_silu(a) * u).astype(BF16)
        acc_ref[...] += jnp.dot(act, wd_ref[0], preferred_element_type=F32)

    @pl.when(active & (f == pl.num_programs(1) - 1))
    def _():
        _store_row_tiles(o_ref, 0, acc_ref[...])


def _expert_tail_kernel(trow_ref, tsub_ref, x_ref, wg_ref, wu_ref, wd_ref, prev_ref, o_ref, h_ref, acc_ref,
                        *, sub, nsub_max, s):
    del trow_ref, prev_ref
    f = pl.program_id(1)
    nsub = tsub_ref[pl.program_id(0)]

    @pl.when(f == 0)
    def _():
        o_ref[...] = jnp.zeros_like(o_ref)
        acc_ref[...] = jnp.zeros_like(acc_ref)

    for sb in range(nsub_max):
        @pl.when(sb < nsub)
        def _():
            rows = slice(sb * sub, (sb + 1) * sub)

            @pl.when(f == 0)
            def _():
                h_ref[rows] = _load_row_tiles(x_ref, sb * sub, sub, s).astype(BF16)

            h = h_ref[rows]
            a = jnp.dot(h, wg_ref[0], preferred_element_type=F32)
            u = jnp.dot(h, wu_ref[0], preferred_element_type=F32)
            act = (_silu(a) * u).astype(BF16)
            acc_ref[rows] += jnp.dot(act, wd_ref[0], preferred_element_type=F32)

            @pl.when(f == pl.num_programs(1) - 1)
            def _():
                _store_row_tiles(o_ref, sb * sub, acc_ref[rows])


def expert_ffn(xs, nfull, frow, fexp, trow, tsub, wg, wu, wd):
    experts, d, fdim = wg.shape
    s = d // LANES
    tf = MOE_F_TILE
    nf = fdim // tf
    nfull_max = frow.shape[0]
    tile = (MOE_ROW_TILE * s, LANES)
    scratch = [pltpu.VMEM((MOE_ROW_TILE, d), BF16), pltpu.VMEM((MOE_ROW_TILE, d), F32)]

    def fcol_full(j, f, nfull):
        return jnp.where(j < nfull[0], f, nf - 1)

    ys = pl.pallas_call(
        functools.partial(_expert_full_kernel, s=s),
        grid_spec=pltpu.PrefetchScalarGridSpec(
            num_scalar_prefetch=3,
            grid=(nfull_max, nf),
            in_specs=[pl.BlockSpec(tile, lambda j, f, nfull, frow, fexp: (frow[j], 0)),
                      pl.BlockSpec((1, d, tf), lambda j, f, nfull, frow, fexp: (fexp[j], 0, fcol_full(j, f, nfull))),
                      pl.BlockSpec((1, d, tf), lambda j, f, nfull, frow, fexp: (fexp[j], 0, fcol_full(j, f, nfull))),
                      pl.BlockSpec((1, tf, d), lambda j, f, nfull, frow, fexp: (fexp[j], fcol_full(j, f, nfull), 0))],
            out_specs=pl.BlockSpec(tile, lambda j, f, nfull, frow, fexp: (frow[j], 0)),
            scratch_shapes=scratch),
        out_shape=jax.ShapeDtypeStruct(xs.shape, F32),
        compiler_params=_cparams("arbitrary", "arbitrary"),
        name="expert_ffn_full",
    )(nfull, frow, fexp, xs, wg, wu, wd)

    def fcol_tail(e, f, tsub):
        return jnp.where(tsub[e] > 0, f, nf - 1)

    kern = functools.partial(_expert_tail_kernel, sub=MOE_SUB, nsub_max=MOE_ROW_TILE // MOE_SUB, s=s)
    return pl.pallas_call(
        kern,
        grid_spec=pltpu.PrefetchScalarGridSpec(
            num_scalar_prefetch=2,
            grid=(experts, nf),
            in_specs=[pl.BlockSpec(tile, lambda e, f, trow, tsub: (trow[e], 0)),
                      pl.BlockSpec((1, d, tf), lambda e, f, trow, tsub: (e, 0, fcol_tail(e, f, tsub))),
                      pl.BlockSpec((1, d, tf), lambda e, f, trow, tsub: (e, 0, fcol_tail(e, f, tsub))),
                      pl.BlockSpec((1, tf, d), lambda e, f, trow, tsub: (e, fcol_tail(e, f, tsub), 0)),
                      pl.BlockSpec(memory_space=pl.ANY)],
            out_specs=pl.BlockSpec(tile, lambda e, f, trow, tsub: (trow[e], 0)),
            scratch_shapes=scratch),
        out_shape=jax.ShapeDtypeStruct(xs.shape, F32),
        input_output_aliases={6: 0},
        compiler_params=_cparams("arbitrary", "arbitrary"),
        name="expert_ffn_tail",
    )(trow, tsub, xs, wg, wu, wd, ys)


def _combine_kernel(p_ref, x_ref, route_ref, fg_ref, ys_hbm, op_ref, os_ref, gath_ref, sem, *, tm, s):
    def row_copy(src, k, t):
        return pltpu.make_async_copy(ys_hbm.at[pl.ds(pl.multiple_of(src, s), s)],
                                     gath_ref.at[k, pl.ds(pl.multiple_of(t * s, s), s)], sem)

    def issue(t, carry):
        for k in range(TOP_K):
            row_copy(p_ref[0, 0, TOP_K * t + k], k, t).start(priority=k)
        return carry

    lax.fori_loop(0, tm, issue, 0, unroll=DMA_UNROLL)

    def drain(t, carry):
        for k in range(TOP_K):
            row_copy(0, 0, 0).wait()
        return carry

    lax.fori_loop(0, tm, drain, 0, unroll=DMA_UNROLL)
    route = route_ref[...]
    x = (x_ref[...] + route[:, 2:3] * _load_row_tiles(gath_ref.at[0], 0, tm, s)
         + route[:, 3:4] * _load_row_tiles(gath_ref.at[1], 0, tm, s))
    y = _rms(x, fg_ref[...])
    op_ref[...] = y

    @pl.when(pl.program_id(0) == pl.num_programs(0) - 1)
    def _():
        os_ref[...] = y[tm - os_ref.shape[0]:]


def combine(x, route, ys, pos, final_g, tm, n_sample):
    t, d = x.shape
    nt = t // tm
    s = d // LANES
    assert n_sample <= tm
    kern = functools.partial(_combine_kernel, tm=tm, s=s)
    return pl.pallas_call(
        kern,
        grid=(nt,),
        in_specs=[pl.BlockSpec((1, 1, TOP_K * tm), lambda i: (i, 0, 0), memory_space=pltpu.SMEM),
                  pl.BlockSpec((tm, d), lambda i: (i, 0)),
                  pl.BlockSpec((tm, LANES), lambda i: (i, 0)),
                  pl.BlockSpec((1, d), lambda i: (0, 0)),
                  pl.BlockSpec(memory_space=pl.ANY)],
        out_specs=[pl.BlockSpec((tm, d), lambda i: (i, 0)),
                   pl.BlockSpec((n_sample, d), lambda i: (0, 0))],
        out_shape=[jax.ShapeDtypeStruct((t - n_sample, d), F32),
                   jax.ShapeDtypeStruct((n_sample, d), F32)],
        scratch_shapes=[pltpu.VMEM((TOP_K, tm * s, LANES), F32),
                        pltpu.SemaphoreType.DMA(())],
        compiler_params=_cparams("arbitrary"),
        name="moe_combine",
    )(pos.reshape(nt, 1, TOP_K * tm), x, route, final_g.reshape(1, d), ys)


def moe_residual_final_norm(a, w_out, x_in, norm_g, router_w, wg, wu, wd, final_g, tm, n_sample):
    t, d = x_in.shape
    experts = router_w.shape[1]
    x, h, route, cnt = outproj_router(a, w_out, x_in, norm_g, router_w, tm)
    counts = cnt[0, :experts].astype(I32)
    ntiles = (TOP_K * t + experts * (MOE_ROW_TILE - 1)) // MOE_ROW_TILE + 1
    nfull_max = max(TOP_K * t // MOE_ROW_TILE, 1)
    s = d // LANES
    frow, fexp, nfull, trow, tsub, zrow, pos = plan(counts, route, ntiles, nfull_max, s)
    pos = pos[:, 4:4 + TOP_K]
    xs = dispatch(h, pos, zrow, ntiles * MOE_ROW_TILE, tm, s)
    ys = expert_ffn(xs, nfull, frow, fexp, trow, tsub, wg, wu, wd)
    return combine(x, route, ys, pos, final_g, tm, n_sample)


def _rope_tables(pos, half):
    inv = jnp.power(ROPE_THETA, -jnp.arange(half, dtype=F32) / half)
    ang = pos[:, None] * inv[None, :]
    return jnp.cos(ang), jnp.sin(ang)


def kernel(x_prompt, x_sample, state_retention, state_hgrn, norm_mix_g, norm_ffn_g, final_norm_g, ret_w_in, ret_gn_g, ret_w_out, hg_w_in, hg_lb_param, hg_norm_g, hg_w_out, ffn_w_gate, ffn_w_up, ffn_w_down, moe_router, moe_w_gate, moe_w_up, moe_w_down):
    bp, lp, d = x_prompt.shape
    bs, ls, _ = x_sample.shape
    assert ls == 1 and norm_mix_g.shape[0] == 2
    _, _, ret_heads, ret_dk, ret_dv = state_retention.shape
    _, _, hg_heads, hg_dk, _ = state_hgrn.shape
    tp = bp * lp
    t = tp + bs
    tm = _pick_tile(t, 768, 16)

    xp = x_prompt.reshape(tp, d)
    xs = x_sample.reshape(bs, d)

    log_gamma = jnp.log1p(-jnp.exp2(-5.0 - jnp.arange(ret_heads, dtype=F32)))
    cos_p, sin_p = _rope_tables(jnp.arange(lp, dtype=F32), ret_dk // 2)
    cos_s, sin_s = _rope_tables(PAST_LEN + jnp.arange(ls, dtype=F32), ret_dk // 2)
    tm_proj = _pick_tile(t, 1536, 16)
    proj = norm_matmul_merged(xp, xs, norm_mix_g[0], ret_w_in[0].astype(BF16), tm_proj,
                              _pick_tile(ret_w_in.shape[2], 1024, LANES))
    gated, ret_p = retention_scan(proj, log_gamma, cos_p, sin_p, ret_gn_g[0], batch=bp, seq=lp,
                                  heads=ret_heads, dk=ret_dk, dv=ret_dv, rows_total=t)
    gated, ret_s = retention_step(proj, gated, state_retention[0], log_gamma, cos_s, sin_s, ret_gn_g[0],
                                  row0=tp, nb=bs, heads=ret_heads, dk=ret_dk, dv=ret_dv)
    x = matmul_residual_merged(gated, ret_w_out[0].astype(BF16), xp, xs, tm)
    ff = ffn_w_gate.shape[2]
    x = ffn_residual(x, norm_ffn_g[0], ffn_w_gate[0].astype(BF16), ffn_w_up[0].astype(BF16),
                     ffn_w_down[0].astype(BF16), tm, _pick_tile(ff, 1536, LANES))

    proj = norm_matmul(x, norm_mix_g[1], hg_w_in[0].astype(BF16), tm_proj,
                       _pick_tile(hg_w_in.shape[2], 1024, LANES))
    experts, _, fe = moe_w_gate.shape[1:]
    gated, hg_p, wg, wu, wd = hgrn_scan(
        proj, hg_lb_param, hg_norm_g[0],
        (moe_w_gate[0].reshape(experts * d, fe), moe_w_up[0].reshape(experts * d, fe),
         moe_w_down[0].reshape(experts * fe, d)),
        batch=bp, seq=lp, heads=hg_heads, dk=hg_dk, rows_total=t, layer=1)
    gated, hg_s = hgrn_step(proj, gated, state_hgrn[0], hg_lb_param, hg_norm_g[0], row0=tp, nb=bs,
                            heads=hg_heads, dk=hg_dk, layer=1)
    y_p, y_s = moe_residual_final_norm(gated, hg_w_out[0].astype(BF16), x, norm_ffn_g[1], moe_router[0],
                                       wg.reshape(experts, d, fe), wu.reshape(experts, d, fe),
                                       wd.reshape(experts, fe, d), final_norm_g, tm, bs)

    return (y_p.reshape(bp, lp, d), y_s.reshape(bs, ls, d),
            ret_p[None], ret_s[None], hg_p[None], hg_s[None])
```

```python
import functools

import jax
import jax.numpy as jnp
from jax import lax
from jax.experimental import pallas as pl
from jax.experimental.pallas import tpu as pltpu

F32 = jnp.float32
BF16 = jnp.bfloat16
I32 = jnp.int32

NORM_EPS = 1e-6
GN_EPS = 1e-5
ROPE_THETA = 10000.0
PAST_LEN = 16384
TOP_K = 2

LANES = 128
VMEM_LIMIT = 56 * 1024 * 1024

RET_CHUNK = 256
HG_CHUNK = 128
HG_DIAG = 4
RET_SCAN_HEADS = 4
RET_STEP_SEQS = 4
HG_STEP_SEQS = 8
MOE_ROW_TILE = 1024
MOE_SUB = 256
MOE_F_TILE = 512
ROUTER_GROUPS = 2
DMA_UNROLL = 8

NT_DIMS = (((1,), (1,)), ((), ()))
TN_DIMS = (((0,), (0,)), ((), ()))


def _cparams(*sem):
    return pltpu.CompilerParams(dimension_semantics=sem, vmem_limit_bytes=VMEM_LIMIT)


def _pick_tile(n, target, mult):
    best = None
    for t in range(mult, min(n, target) + 1, mult):
        if n % t == 0:
            best = t
    assert best is not None, (n, target, mult)
    return best


def _sigmoid(x):
    return 0.5 * jnp.tanh(0.5 * x) + 0.5


def _silu(x):
    h = 0.5 * x
    return h * jnp.tanh(h) + h


def _rms(x, g):
    return x * lax.rsqrt(jnp.mean(x * x, axis=-1, keepdims=True) + NORM_EPS) * g


def _row_group(b):
    return pl.multiple_of(lax.shift_left(lax.shift_right_logical(b, 3), 3), 8), b & 7


def _load_row(ref, b, cols):
    base, r = _row_group(b)
    blk = ref[pl.ds(base, 8), cols]
    rows = lax.broadcasted_iota(I32, blk.shape, 0)
    return jnp.sum(jnp.where(rows == r, blk, 0.0), axis=0, keepdims=True)


def _store_row(ref, b, cols, row):
    base, r = _row_group(b)
    blk = ref[pl.ds(base, 8), cols]
    rows = lax.broadcasted_iota(I32, blk.shape, 0)
    ref[pl.ds(base, 8), cols] = jnp.where(rows == r, row, blk)


def _norm_matmul_kernel(x_ref, g_ref, w_ref, o_ref, h_ref):
    @pl.when(pl.program_id(1) == 0)
    def _():
        h_ref[...] = _rms(x_ref[...], g_ref[...]).astype(BF16)

    o_ref[...] = jnp.dot(h_ref[...], w_ref[...], preferred_element_type=F32).astype(o_ref.dtype)


def norm_matmul(x, g, w, tm, tn):
    t, d = x.shape
    n = w.shape[1]
    return pl.pallas_call(
        _norm_matmul_kernel,
        grid=(t // tm, n // tn),
        in_specs=[pl.BlockSpec((tm, d), lambda i, j: (i, 0)),
                  pl.BlockSpec((1, d), lambda i, j: (0, 0)),
                  pl.BlockSpec((d, tn), lambda i, j: (0, j))],
        out_specs=pl.BlockSpec((tm, tn), lambda i, j: (i, j)),
        out_shape=jax.ShapeDtypeStruct((t, n), BF16),
        scratch_shapes=[pltpu.VMEM((tm, d), BF16)],
        compiler_params=_cparams("parallel", "arbitrary"),
        name="norm_matmul",
    )(x, g.reshape(1, d), w)


def _with_merged_rows(xp_ref, xs_ref, fn):
    i = pl.program_id(0)
    last = pl.num_programs(0) - 1
    n_prompt = xp_ref.shape[0] - xs_ref.shape[0]

    @pl.when(i < last)
    def _():
        fn(xp_ref[...])

    @pl.when(i == last)
    def _():
        fn(jnp.concatenate([xp_ref[:n_prompt], xs_ref[...]], axis=0))


def _merged_specs(tm, d, bs, nidx):
    if nidx == 1:
        return [pl.BlockSpec((tm, d), lambda i: (i, 0)), pl.BlockSpec((bs, d), lambda i: (0, 0))]
    return [pl.BlockSpec((tm, d), lambda i, j: (i, 0)), pl.BlockSpec((bs, d), lambda i, j: (0, 0))]


def _norm_matmul2_kernel(xp_ref, xs_ref, g_ref, w_ref, o_ref, h_ref):
    @pl.when(pl.program_id(1) == 0)
    def _():
        def fill(x):
            h_ref[...] = _rms(x, g_ref[...]).astype(BF16)

        _with_merged_rows(xp_ref, xs_ref, fill)

    o_ref[...] = jnp.dot(h_ref[...], w_ref[...], preferred_element_type=F32).astype(o_ref.dtype)


def norm_matmul_merged(xp, xs, g, w, tm, tn):
    tp, d = xp.shape
    bs = xs.shape[0]
    t = tp + bs
    n = w.shape[1]
    assert t % tm == 0 and bs <= tm
    return pl.pallas_call(
        _norm_matmul2_kernel,
        grid=(t // tm, n // tn),
        in_specs=_merged_specs(tm, d, bs, 2) + [pl.BlockSpec((1, d), lambda i, j: (0, 0)),
                                                 pl.BlockSpec((d, tn), lambda i, j: (0, j))],
        out_specs=pl.BlockSpec((tm, tn), lambda i, j: (i, j)),
        out_shape=jax.ShapeDtypeStruct((t, n), BF16),
        scratch_shapes=[pltpu.VMEM((tm, d), BF16)],
        compiler_params=_cparams("parallel", "arbitrary"),
        name="norm_matmul_merged",
    )(xp, xs, g.reshape(1, d), w)


def _matmul_res2_kernel(a_ref, w_ref, xp_ref, xs_ref, o_ref):
    def fill(x):
        o_ref[...] = x + jnp.dot(a_ref[...], w_ref[...], preferred_element_type=F32)

    _with_merged_rows(xp_ref, xs_ref, fill)


def matmul_residual_merged(a, w, xp, xs, tm):
    t, k = a.shape
    n = w.shape[1]
    bs = xs.shape[0]
    assert t % tm == 0 and bs <= tm and xp.shape[0] + bs == t
    return pl.pallas_call(
        _matmul_res2_kernel,
        grid=(t // tm,),
        in_specs=[pl.BlockSpec((tm, k), lambda i: (i, 0)),
                  pl.BlockSpec((k, n), lambda i: (0, 0))] + _merged_specs(tm, n, bs, 1),
        out_specs=pl.BlockSpec((tm, n), lambda i: (i, 0)),
        out_shape=jax.ShapeDtypeStruct((t, n), F32),
        compiler_params=_cparams("parallel"),
        name="matmul_residual_merged",
    )(a, w, xp, xs)


def _ffn_kernel(x_ref, g_ref, wg_ref, wu_ref, wd_ref, o_ref, h_ref):
    f = pl.program_id(1)

    @pl.when(f == 0)
    def _():
        x = x_ref[...]
        h_ref[...] = _rms(x, g_ref[...]).astype(BF16)
        o_ref[...] = x

    h = h_ref[...]
    a = jnp.dot(h, wg_ref[...], preferred_element_type=F32)
    u = jnp.dot(h, wu_ref[...], preferred_element_type=F32)
    act = (_silu(a) * u).astype(BF16)
    o_ref[...] += jnp.dot(act, wd_ref[...], preferred_element_type=F32)


def ffn_residual(x, g, wg, wu, wd, tm, tf):
    t, d = x.shape
    f = wg.shape[1]
    return pl.pallas_call(
        _ffn_kernel,
        grid=(t // tm, f // tf),
        in_specs=[pl.BlockSpec((tm, d), lambda i, j: (i, 0)),
                  pl.BlockSpec((1, d), lambda i, j: (0, 0)),
                  pl.BlockSpec((d, tf), lambda i, j: (0, j)),
                  pl.BlockSpec((d, tf), lambda i, j: (0, j)),
                  pl.BlockSpec((tf, d), lambda i, j: (j, 0))],
        out_specs=pl.BlockSpec((tm, d), lambda i, j: (i, 0)),
        out_shape=jax.ShapeDtypeStruct((t, d), F32),
        scratch_shapes=[pltpu.VMEM((tm, d), BF16)],
        compiler_params=_cparams("parallel", "arbitrary"),
        name="ffn_residual",
    )(x, g.reshape(1, d), wg, wu, wd)


def _rotary(x, cos, sin, half):
    x1 = x[:, :half]
    x2 = x[:, half:]
    return jnp.concatenate([x1 * cos - x2 * sin, x2 * cos + x1 * sin], axis=-1)


def _group_norm_gate(o, gate, gain):
    mu = jnp.mean(o, axis=-1, keepdims=True)
    d = o - mu
    var = jnp.mean(d * d, axis=-1, keepdims=True)
    return _silu(gate) * (d * lax.rsqrt(var + GN_EPS) * gain)


def _ret_scan_kernel(lg_ref, q_ref, k_ref, v_ref, g_ref, cos_ref, sin_ref, gn_ref,
                     o_ref, sfin_ref, s_ref, mask_ref, din_ref, dout_ref, *, chunk, dk, dv, hp):
    c = pl.program_id(2)
    head0 = pl.program_id(1) * hp

    @pl.when(c == 0)
    def _():
        s_ref[...] = jnp.zeros_like(s_ref)
        t = lax.broadcasted_iota(I32, (chunk, LANES), 0).astype(F32)
        ti = lax.broadcasted_iota(I32, (chunk, chunk), 0)
        si = lax.broadcasted_iota(I32, (chunk, chunk), 1)
        causal = ti >= si
        rel = jnp.where(causal, (ti - si).astype(F32), 0.0)
        for j in range(hp):
            lg = lg_ref[head0 + j]
            din_ref[j] = jnp.exp((t + 1.0) * lg)
            dout_ref[j] = jnp.exp((chunk - 1.0 - t) * lg)
            mask_ref[j] = jnp.where(causal, jnp.exp(rel * lg), 0.0)

    half = dk // 2
    cos = cos_ref[...]
    sin = sin_ref[...]
    for j in range(hp):
        qk_cols = slice(j * dk, (j + 1) * dk)
        v_cols = slice(j * dv, (j + 1) * dv)
        q = _rotary(q_ref[:, qk_cols].astype(F32), cos, sin, half)
        k = _rotary(k_ref[:, qk_cols].astype(F32), cos, sin, half) * (dk ** -0.5)
        v = v_ref[:, v_cols]
        decay_in = _lane_tile(din_ref[j], dk)
        decay_out = _lane_tile(dout_ref[j], dk)
        decay_chunk = jnp.exp(jnp.zeros((1, dv), F32) + chunk * lg_ref[head0 + j])

        s = s_ref[j]
        scores = (lax.dot_general(q.astype(BF16), k.astype(BF16), NT_DIMS, preferred_element_type=F32)
                  * mask_ref[j])
        o = (jnp.dot(scores.astype(BF16), v, preferred_element_type=F32)
             + jnp.dot((q * decay_in).astype(BF16), s.astype(BF16), preferred_element_type=F32))
        kd_t = (k * decay_out).T.astype(BF16)
        s_ref[j] = s * decay_chunk + jnp.dot(kd_t, v, preferred_element_type=F32)
        o_ref[:, v_cols] = _group_norm_gate(o, g_ref[:, v_cols].astype(F32), gn_ref[:, v_cols]).astype(BF16)

    @pl.when(c == pl.num_programs(2) - 1)
    def _():
        sfin_ref[0] = s_ref[...]


def retention_scan(proj, log_gamma, cos, sin, gn_g, *, batch, seq, heads, dk, dv, rows_total):
    chunk = min(RET_CHUNK, seq)
    nc = seq // chunk
    hp = min(RET_SCAN_HEADS, heads)
    assert heads % hp == 0
    ng = heads // hp
    kern = functools.partial(_ret_scan_kernel, chunk=chunk, dk=dk, dv=dv, hp=hp)
    v0 = 2 * heads * dk // (hp * dv)
    return pl.pallas_call(
        kern,
        grid=(batch, ng, nc),
        in_specs=[pl.BlockSpec(memory_space=pltpu.SMEM),
                  pl.BlockSpec((chunk, hp * dk), lambda b, h, c: (b * nc + c, h)),
                  pl.BlockSpec((chunk, hp * dk), lambda b, h, c: (b * nc + c, ng + h)),
                  pl.BlockSpec((chunk, hp * dv), lambda b, h, c: (b * nc + c, v0 + h)),
                  pl.BlockSpec((chunk, hp * dv), lambda b, h, c: (b * nc + c, v0 + ng + h)),
                  pl.BlockSpec((chunk, dk // 2), lambda b, h, c: (c, 0)),
                  pl.BlockSpec((chunk, dk // 2), lambda b, h, c: (c, 0)),
                  pl.BlockSpec((1, hp * dv), lambda b, h, c: (0, h))],
        out_specs=[pl.BlockSpec((chunk, hp * dv), lambda b, h, c: (b * nc + c, h)),
                   pl.BlockSpec((1, hp, dk, dv), lambda b, h, c: (b, h, 0, 0))],
        out_shape=[jax.ShapeDtypeStruct((rows_total, heads * dv), BF16),
                   jax.ShapeDtypeStruct((batch, heads, dk, dv), F32)],
        scratch_shapes=[pltpu.VMEM((hp, dk, dv), F32),
                        pltpu.VMEM((hp, chunk, chunk), F32),
                        pltpu.VMEM((hp, chunk, LANES), F32),
                        pltpu.VMEM((hp, chunk, LANES), F32)],
        compiler_params=_cparams("parallel", "parallel", "arbitrary"),
        name="retention_scan",
    )(log_gamma, proj, proj, proj, proj, cos, sin, gn_g.reshape(1, heads * dv))


def _one_hot_rows(nb, b):
    return jnp.where(lax.broadcasted_iota(I32, (nb, LANES), 0) == b, 1.0, 0.0).astype(BF16)


def _lane_tile(x, width):
    return jnp.concatenate([x] * (width // x.shape[1]), axis=1)


def _ret_step_kernel(lg_ref, q_ref, k_ref, v_ref, g_ref, cos_ref, sin_ref, gn_ref, s_ref, prev_ref,
                     o_ref, snew_ref, qt_ref, kt_ref, vf_ref, gf_ref, orow_ref, *, heads, dk, dv, nb, per_step):
    del prev_ref
    step = pl.program_id(0)
    half = dk // 2

    @pl.when(step == 0)
    def _():
        orow_ref[...] = jnp.zeros_like(orow_ref)
        vf_ref[...] = v_ref[...].astype(F32)
        gf_ref[...] = g_ref[...].astype(F32)
        cos = cos_ref[...]
        sin = sin_ref[...]
        for h in range(heads):
            qh = _rotary(q_ref[:, h * dk:(h + 1) * dk].astype(F32), cos, sin, half)
            kh = _rotary(k_ref[:, h * dk:(h + 1) * dk].astype(F32), cos, sin, half) * (dk ** -0.5)
            qt_ref[h] = qh.T.astype(BF16)
            kt_ref[h] = kh.T.astype(BF16)

    for i in range(per_step):
        b = step * per_step + i
        onehot = _one_hot_rows(nb, b)
        for h in range(heads):
            cols = slice(h * dv, (h + 1) * dv)
            kcol = _lane_tile(jnp.dot(kt_ref[h], onehot, preferred_element_type=F32), dv)
            qcol = _lane_tile(jnp.dot(qt_ref[h], onehot, preferred_element_type=F32), dv)
            vrow = _load_row(vf_ref, b, cols)
            gamma = jnp.exp(jnp.zeros((1, dv), F32) + lg_ref[h])
            s_new = s_ref[i, h] * gamma + kcol * vrow
            snew_ref[i, h] = s_new
            o = jnp.sum(s_new * qcol, axis=0, keepdims=True)
            _store_row(orow_ref, b, cols, _group_norm_gate(o, _load_row(gf_ref, b, cols), gn_ref[:, cols]))

    @pl.when(step == pl.num_programs(0) - 1)
    def _():
        o_ref[...] = orow_ref[...].astype(BF16)


def retention_step(proj, gated, state, log_gamma, cos, sin, gn_g, *, row0, nb, heads, dk, dv):
    per_step = RET_STEP_SEQS
    kern = functools.partial(_ret_step_kernel, heads=heads, dk=dk, dv=dv, nb=nb, per_step=per_step)
    rb = row0 // nb
    wq = heads * dk
    wv = heads * dv
    return pl.pallas_call(
        kern,
        grid=(nb // per_step,),
        in_specs=[pl.BlockSpec(memory_space=pltpu.SMEM),
                  pl.BlockSpec((nb, wq), lambda b: (rb, 0)),
                  pl.BlockSpec((nb, wq), lambda b: (rb, 1)),
                  pl.BlockSpec((nb, wv), lambda b: (rb, 2 * wq // wv)),
                  pl.BlockSpec((nb, wv), lambda b: (rb, 2 * wq // wv + 1)),
                  pl.BlockSpec((1, dk // 2), lambda b: (0, 0)),
                  pl.BlockSpec((1, dk // 2), lambda b: (0, 0)),
                  pl.BlockSpec((1, wv), lambda b: (0, 0)),
                  pl.BlockSpec((per_step, heads, dk, dv), lambda b: (b, 0, 0, 0)),
                  pl.BlockSpec(memory_space=pl.ANY)],
        out_specs=[pl.BlockSpec((nb, wv), lambda b: (rb, 0)),
                   pl.BlockSpec((per_step, heads, dk, dv), lambda b: (b, 0, 0, 0))],
        out_shape=[jax.ShapeDtypeStruct(gated.shape, gated.dtype),
                   jax.ShapeDtypeStruct(state.shape, F32)],
        scratch_shapes=[pltpu.VMEM((heads, dk, nb), BF16),
                        pltpu.VMEM((heads, dk, nb), BF16),
                        pltpu.VMEM((nb, wv), F32),
                        pltpu.VMEM((nb, wv), F32),
                        pltpu.VMEM((nb, wv), F32)],
        input_output_aliases={9: 0},
        compiler_params=_cparams("arbitrary"),
        name="retention_step",
    )(log_gamma, proj, proj, proj, proj, cos, sin, gn_g.reshape(1, wv), state, gated)


def _lower_bound(lbp, layer):
    m = jnp.max(lbp, axis=0, keepdims=True)
    e = jnp.exp(lbp - m)
    p = e / jnp.sum(e, axis=0, keepdims=True)
    return jnp.sum(p[:layer + 1], axis=0, keepdims=True) - p[0:1]


def _split_dot(mat_bf16, x):
    hi = x.astype(BF16)
    r1 = x - hi.astype(F32)
    mid = r1.astype(BF16)
    lo = (r1 - mid.astype(F32)).astype(BF16)
    return (jnp.dot(mat_bf16, hi, preferred_element_type=F32)
            + jnp.dot(mat_bf16, mid, preferred_element_type=F32)
            + jnp.dot(mat_bf16, lo, preferred_element_type=F32))


def _rms_gate(o, gate, gain):
    return _silu(gate) * (o * lax.rsqrt(jnp.mean(o * o, axis=-1, keepdims=True) + NORM_EPS) * gain)


def _group_row(x, s, group):
    n, w = x.shape
    x3 = x.reshape(n // group, group, w)
    return jnp.broadcast_to(x3[:, s:s + 1, :], x3.shape).reshape(n, w)


def _hgrn_scan_kernel(*refs, chunk, heads, dk, diag, layer, ncast):
    lbp_ref, q_ref, f_ref, i_ref, g_ref, ng_ref = refs[:6]
    cast_in = refs[6:6 + ncast]
    o_ref, sfin_ref = refs[6 + ncast:8 + ncast]
    cast_out = refs[8 + ncast:8 + 2 * ncast]
    st_ref = refs[8 + 2 * ncast]
    c = pl.program_id(1)

    @pl.when(c == 0)
    def _():
        st_ref[...] = jnp.zeros_like(st_ref)

    for src, dst in zip(cast_in, cast_out):
        dst[...] = src[...].astype(BF16)

    w = heads * dk
    lb = _lower_bound(lbp_ref[...], layer)
    qs = _silu(q_ref[...].astype(F32))
    forget = lb + (1.0 - lb) * _sigmoid(f_ref[...].astype(F32))
    kk = 1.0 - forget
    logf = jnp.log(forget)

    row = lax.broadcasted_iota(I32, (chunk, chunk), 0)
    col = lax.broadcasted_iota(I32, (chunk, chunk), 1)
    tril = jnp.where(row >= col, 1.0, 0.0).astype(BF16)
    bcum = _split_dot(tril, logf)
    blast = bcum[chunk - 1:chunk, :]
    qe = (qs * jnp.exp(bcum)).astype(BF16)
    kdec = (kk * jnp.exp(blast - bcum)).astype(BF16)
    lk = jnp.log(kk) - bcum
    iv = i_ref[...]
    gate = g_ref[...].astype(F32)
    gain = ng_ref[...]

    u = jnp.where(row > col, row ^ col, 0)
    rowv = lax.broadcasted_iota(I32, (chunk, 1), 0)
    levels = []
    m = chunk // 2
    while m >= diag:
        span = 2 * m
        bref = jnp.concatenate(
            [jnp.broadcast_to(bcum[g * span + m - 1:g * span + m], (span, w)) for g in range(chunk // span)],
            axis=0)
        upper = (rowv & m) != 0
        d = bcum - bref
        x = (jnp.where(upper, qs, kk) * jnp.exp(jnp.where(upper, d, -d))).astype(BF16)
        levels.append((x, lax.shift_right_logical(u, m.bit_length() - 1) == 1))
        m //= 2
    in_block = row & (diag - 1)
    block_col0 = row - in_block
    diag_masks = [(col == block_col0 + s) & (in_block >= s) for s in range(diag)]
    ones = jnp.ones((dk, LANES), BF16)

    for h in range(heads):
        cols = slice(h * dk, (h + 1) * dk)
        qs_h = qs[:, cols]
        lk_h = lk[:, cols]
        b_h = bcum[:, cols]
        a = jnp.zeros((chunk, chunk), F32)
        for x, mask in levels:
            x_h = x[:, cols]
            a = jnp.where(mask, lax.dot_general(x_h, x_h, NT_DIMS, preferred_element_type=F32), a)
        ws = [qs_h * jnp.exp(b_h + _group_row(lk_h, s, diag)) for s in range(diag)]
        rsum = jnp.dot(jnp.concatenate(ws, axis=0).astype(BF16), ones, preferred_element_type=F32)
        for s in range(diag):
            a = jnp.where(diag_masks[s], rsum[s * chunk:(s + 1) * chunk], a)
        st = st_ref[h]
        i_h = iv[:, cols]
        o_h = (jnp.dot(a.astype(BF16), i_h, preferred_element_type=F32)
               + lax.dot_general(qe[:, cols], st.astype(BF16), NT_DIMS, preferred_element_type=F32))
        st_new = (st * jnp.exp(blast[:, cols])
                  + lax.dot_general(i_h, kdec[:, cols], TN_DIMS, preferred_element_type=F32))
        st_ref[h] = st_new
        o_ref[:, cols] = _rms_gate(o_h, gate[:, cols], gain[:, cols]).astype(BF16)

    @pl.when(c == pl.num_programs(1) - 1)
    def _():
        for h in range(heads):
            sfin_ref[0, h] = st_ref[h].T


def hgrn_scan(proj, lb_param, norm_g, side_f32, *, batch, seq, heads, dk, rows_total, layer):
    chunk = min(HG_CHUNK, seq)
    assert chunk == LANES and dk == LANES
    nc = seq // chunk
    w = heads * dk
    steps = batch * nc
    slabs = []
    for a in side_f32:
        assert a.shape[0] % (steps * 16) == 0, (a.shape, steps)
        slabs.append(pl.BlockSpec((a.shape[0] // steps, a.shape[1]), lambda b, c: (b * nc + c, 0)))
    kern = functools.partial(_hgrn_scan_kernel, chunk=chunk, heads=heads, dk=dk, diag=HG_DIAG, layer=layer,
                             ncast=len(side_f32))
    return pl.pallas_call(
        kern,
        grid=(batch, nc),
        in_specs=[pl.BlockSpec(lb_param.shape, lambda b, c: (0, 0)),
                  pl.BlockSpec((chunk, w), lambda b, c: (b * nc + c, 0)),
                  pl.BlockSpec((chunk, w), lambda b, c: (b * nc + c, 1)),
                  pl.BlockSpec((chunk, w), lambda b, c: (b * nc + c, 2)),
                  pl.BlockSpec((chunk, w), lambda b, c: (b * nc + c, 3)),
                  pl.BlockSpec((1, w), lambda b, c: (0, 0))] + slabs,
        out_specs=[pl.BlockSpec((chunk, w), lambda b, c: (b * nc + c, 0)),
                   pl.BlockSpec((1, heads, dk, dk), lambda b, c: (b, 0, 0, 0))] + slabs,
        out_shape=[jax.ShapeDtypeStruct((rows_total, w), BF16),
                   jax.ShapeDtypeStruct((batch, heads, dk, dk), F32)]
                  + [jax.ShapeDtypeStruct(a.shape, BF16) for a in side_f32],
        scratch_shapes=[pltpu.VMEM((heads, dk, dk), F32)],
        compiler_params=_cparams("parallel", "arbitrary"),
        name="hgrn_scan",
    )(lb_param, proj, proj, proj, proj, norm_g.reshape(1, w), *side_f32)


def _hgrn_step_kernel(lbp_ref, q_ref, f_ref, i_ref, g_ref, ng_ref, s_ref, prev_ref,
                      o_ref, snew_ref, qt_ref, ft_ref, if_ref, gf_ref, orow_ref,
                      *, heads, dk, nb, layer, per_step):
    del prev_ref
    step = pl.program_id(0)

    @pl.when(step == 0)
    def _():
        orow_ref[...] = jnp.zeros_like(orow_ref)
        if_ref[...] = i_ref[...].astype(F32)
        gf_ref[...] = g_ref[...].astype(F32)
        lb = _lower_bound(lbp_ref[...], layer)
        qs = _silu(q_ref[...].astype(F32))
        forget = lb + (1.0 - lb) * _sigmoid(f_ref[...].astype(F32))
        for h in range(heads):
            cols = slice(h * dk, (h + 1) * dk)
            qt_ref[h] = qs[:, cols].T.astype(BF16)
            ft = forget[:, cols].T
            hi = ft.astype(BF16)
            r1 = ft - hi.astype(F32)
            mid = r1.astype(BF16)
            ft_ref[0, h] = hi
            ft_ref[1, h] = mid
            ft_ref[2, h] = (r1 - mid.astype(F32)).astype(BF16)

    for i in range(per_step):
        b = step * per_step + i
        onehot = _one_hot_rows(nb, b)
        for h in range(heads):
            cols = slice(h * dk, (h + 1) * dk)
            fcol = (jnp.dot(ft_ref[0, h], onehot, preferred_element_type=F32)
                    + jnp.dot(ft_ref[1, h], onehot, preferred_element_type=F32)
                    + jnp.dot(ft_ref[2, h], onehot, preferred_element_type=F32))
            qcol = jnp.dot(qt_ref[h], onehot, preferred_element_type=F32)
            irow = _load_row(if_ref, b, cols)
            s_new = s_ref[i, h] * fcol + (1.0 - fcol) * irow
            snew_ref[i, h] = s_new
            o = jnp.sum(s_new * qcol, axis=0, keepdims=True)
            _store_row(orow_ref, b, cols, _rms_gate(o, _load_row(gf_ref, b, cols), ng_ref[:, cols]))

    @pl.when(step == pl.num_programs(0) - 1)
    def _():
        o_ref[...] = orow_ref[...].astype(BF16)


def hgrn_step(proj, gated, state, lb_param, norm_g, *, row0, nb, heads, dk, layer):
    per_step = HG_STEP_SEQS
    kern = functools.partial(_hgrn_step_kernel, heads=heads, dk=dk, nb=nb, layer=layer, per_step=per_step)
    rb = row0 // nb
    w = heads * dk
    return pl.pallas_call(
        kern,
        grid=(nb // per_step,),
        in_specs=[pl.BlockSpec(lb_param.shape, lambda b: (0, 0)),
                  pl.BlockSpec((nb, w), lambda b: (rb, 0)),
                  pl.BlockSpec((nb, w), lambda b: (rb, 1)),
                  pl.BlockSpec((nb, w), lambda b: (rb, 2)),
                  pl.BlockSpec((nb, w), lambda b: (rb, 3)),
                  pl.BlockSpec((1, w), lambda b: (0, 0)),
                  pl.BlockSpec((per_step, heads, dk, dk), lambda b: (b, 0, 0, 0)),
                  pl.BlockSpec(memory_space=pl.ANY)],
        out_specs=[pl.BlockSpec((nb, w), lambda b: (rb, 0)),
                   pl.BlockSpec((per_step, heads, dk, dk), lambda b: (b, 0, 0, 0))],
        out_shape=[jax.ShapeDtypeStruct(gated.shape, gated.dtype),
                   jax.ShapeDtypeStruct(state.shape, F32)],
        scratch_shapes=[pltpu.VMEM((heads, dk, nb), BF16),
                        pltpu.VMEM((3, heads, dk, nb), BF16),
                        pltpu.VMEM((nb, w), F32),
                        pltpu.VMEM((nb, w), F32),
                        pltpu.VMEM((nb, w), F32)],
        input_output_aliases={7: 0},
        compiler_params=_cparams("arbitrary"),
        name="hgrn_step",
    )(lb_param, proj, proj, proj, proj, norm_g.reshape(1, w), state, gated)


def _store_row_tiles(ref, r0, x):
    n, d = x.shape
    s = d // LANES
    for c in range(s):
        ref[pl.ds(s * r0 + c, n, stride=s), :] = x[:, c * LANES:(c + 1) * LANES]


def _load_row_tiles(ref, r0, n, s):
    return jnp.concatenate([ref[pl.ds(s * r0 + c, n, stride=s), :] for c in range(s)], axis=1)


def _router_kernel(a_ref, w_ref, x_ref, g_ref, r_ref, xo_ref, h_ref, route_ref, cnt_ref,
                   carry_ref, rhi_ref, rlo_ref, *, tm, experts):
    i = pl.program_id(0)

    @pl.when(i == 0)
    def _():
        carry_ref[...] = jnp.zeros_like(carry_ref)
        r = r_ref[...]
        rhi = r.astype(BF16)
        rhi_ref[...] = rhi
        rlo_ref[...] = (r - rhi.astype(F32)).astype(BF16)

    step = (tm // ROUTER_GROUPS + 15) // 16 * 16
    groups = tuple((r, min(r + step, tm)) for r in range(0, tm, step))
    all_logits = []
    for r0, r1 in groups:
        rows = slice(r0, r1)
        x = x_ref[rows] + jnp.dot(a_ref[rows], w_ref[...], preferred_element_type=F32)
        xo_ref[rows] = x
        h = _rms(x, g_ref[...])
        _store_row_tiles(h_ref, r0, h)
        hhi = h.astype(BF16)
        hlo = (h - hhi.astype(F32)).astype(BF16)
        all_logits.append(jnp.dot(hhi, rhi_ref[...], preferred_element_type=F32)
                          + jnp.dot(hlo, rhi_ref[...], preferred_element_type=F32)
                          + jnp.dot(hhi, rlo_ref[...], preferred_element_type=F32))
    total = carry_ref[...]
    for (r0, r1), logits in zip(groups, all_logits):
        n = r1 - r0
        rows = slice(r0, r1)
        lane = lax.broadcasted_iota(I32, (n, LANES), 1)
        valid = lane < experts
        z = jnp.where(valid, logits, -jnp.inf)
        ez = jnp.exp(z - jnp.max(z, axis=-1, keepdims=True))
        p = ez / jnp.sum(ez, axis=-1, keepdims=True)
        p = jnp.where(valid, p, -1.0)
        lane_f = lane.astype(F32)
        v1 = jnp.max(p, axis=-1, keepdims=True)
        i1 = jnp.min(jnp.where(p == v1, lane_f, float(LANES)), axis=-1, keepdims=True)
        p2 = jnp.where(lane_f == i1, -1.0, p)
        v2 = jnp.max(p2, axis=-1, keepdims=True)
        i2 = jnp.min(jnp.where(p2 == v2, lane_f, float(LANES)), axis=-1, keepdims=True)
        den = v1 + v2
        m0 = jnp.where(lane_f == i1, 1.0, 0.0)
        m1 = jnp.where(lane_f == i2, 1.0, 0.0)
        msum = m0 + m1
        ri = lax.broadcasted_iota(I32, (n, n), 0)
        ci = lax.broadcasted_iota(I32, (n, n), 1)
        strict = jnp.where(ri > ci, 1.0, 0.0).astype(BF16)
        before = jnp.dot(strict, msum.astype(BF16), preferred_element_type=F32) + total
        rank0 = jnp.sum(m0 * before, axis=-1, keepdims=True)
        rank1 = jnp.sum(m1 * before, axis=-1, keepdims=True)
        total = total + jnp.sum(msum, axis=0, keepdims=True)
        route_ref[rows] = jnp.where(lane == 0, i1,
                          jnp.where(lane == 1, i2,
                          jnp.where(lane == 2, v1 / den,
                          jnp.where(lane == 3, v2 / den,
                          jnp.where(lane == 4, rank0,
                          jnp.where(lane == 5, rank1, 0.0))))))
    carry_ref[...] = total
    cnt_ref[...] = jnp.broadcast_to(total, cnt_ref.shape)


def outproj_router(a, w, x, g, router_w, tm):
    t, d = x.shape
    k = a.shape[1]
    experts = router_w.shape[1]
    rpad = jnp.zeros((d, LANES), F32).at[:, :experts].set(router_w)
    kern = functools.partial(_router_kernel, tm=tm, experts=experts)
    return pl.pallas_call(
        kern,
        grid=(t // tm,),
        in_specs=[pl.BlockSpec((tm, k), lambda i: (i, 0)),
                  pl.BlockSpec((k, d), lambda i: (0, 0)),
                  pl.BlockSpec((tm, d), lambda i: (i, 0)),
                  pl.BlockSpec((1, d), lambda i: (0, 0)),
                  pl.BlockSpec((d, LANES), lambda i: (0, 0))],
        out_specs=[pl.BlockSpec((tm, d), lambda i: (i, 0)),
                   pl.BlockSpec((tm * (d // LANES), LANES), lambda i: (i, 0)),
                   pl.BlockSpec((tm, LANES), lambda i: (i, 0)),
                   pl.BlockSpec((8, LANES), lambda i: (0, 0))],
        out_shape=[jax.ShapeDtypeStruct((t, d), F32),
                   jax.ShapeDtypeStruct((t * (d // LANES), LANES), F32),
                   jax.ShapeDtypeStruct((t, LANES), F32),
                   jax.ShapeDtypeStruct((8, LANES), F32)],
        scratch_shapes=[pltpu.VMEM((1, LANES), F32),
                        pltpu.VMEM((d, LANES), BF16),
                        pltpu.VMEM((d, LANES), BF16)],
        compiler_params=_cparams("arbitrary"),
        name="outproj_router",
    )(a, w, x, g.reshape(1, d), rpad)


def _plan_kernel(cnt_ref, route_ref, frow_ref, fexp_ref, nfull_ref, trow_ref, tsub_ref, zrow_ref, pos_ref,
                 *, experts, nfull_max, dummy, row_tile, sub, row_scale):
    tile0 = jnp.int32(0)
    nfull = jnp.int32(0)
    last = jnp.int32(0)
    route = route_ref[...]
    slot_expert = pltpu.roll(route, 4, axis=1)
    first_row = jnp.zeros_like(route)
    for e in range(experts):
        n = cnt_ref[e]
        first_row = jnp.where(slot_expert == float(e), (tile0 * row_tile).astype(F32), first_row)
        full = n // row_tile
        rem = n - full * row_tile

        def fill(j, carry, e=e, tile0=tile0, nfull=nfull):
            frow_ref[nfull + j] = tile0 + j
            fexp_ref[nfull + j] = e
            return carry

        lax.fori_loop(0, full, fill, 0)
        trow_ref[e] = jnp.where(rem > 0, tile0 + full, dummy)
        tsub_ref[e] = (rem + (sub - 1)) // sub
        zrow_ref[e] = tile0 * row_tile + (n // sub) * sub
        last = jnp.where(full > 0, e, last)
        nfull = nfull + full
        tile0 = tile0 + full + jnp.where(rem > 0, 1, 0)
    nfull_ref[0] = nfull
    pos_ref[...] = ((first_row + route) * float(row_scale)).astype(I32)

    def unused(j, carry):
        frow_ref[j] = dummy
        fexp_ref[j] = last
        return carry

    lax.fori_loop(nfull, nfull_max, unused, 0)


def plan(counts, route, ntiles, nfull_max, row_scale):
    experts = counts.shape[0]
    kern = functools.partial(_plan_kernel, experts=experts, nfull_max=nfull_max, dummy=ntiles - 1,
                             row_tile=MOE_ROW_TILE, sub=MOE_SUB, row_scale=row_scale)
    smem = pl.BlockSpec(memory_space=pltpu.SMEM)
    vmem = pl.BlockSpec(memory_space=pltpu.VMEM)
    return pl.pallas_call(
        kern,
        in_specs=[smem, vmem],
        out_specs=[smem] * 6 + [vmem],
        out_shape=[jax.ShapeDtypeStruct((nfull_max,), I32),
                   jax.ShapeDtypeStruct((nfull_max,), I32),
                   jax.ShapeDtypeStruct((1,), I32),
                   jax.ShapeDtypeStruct((experts,), I32),
                   jax.ShapeDtypeStruct((experts,), I32),
                   jax.ShapeDtypeStruct((experts,), I32),
                   jax.ShapeDtypeStruct(route.shape, I32)],
        compiler_params=pltpu.CompilerParams(vmem_limit_bytes=VMEM_LIMIT),
        name="moe_plan",
    )(counts, route)


def _dispatch_kernel(zrow_ref, p_ref, h_ref, out_hbm, zero_ref, sem, zsem, *, tm, experts, s):
    @pl.when(pl.program_id(0) == 0)
    def _():
        zero_ref[...] = jnp.zeros_like(zero_ref)

        def zero_copy(e):
            row = pl.multiple_of(zrow_ref[e] * s, MOE_SUB * s)
            return pltpu.make_async_copy(zero_ref, out_hbm.at[pl.ds(row, MOE_SUB * s)], zsem)

        for e in range(experts):
            zero_copy(e).start()
        for e in range(experts):
            zero_copy(e).wait()

    def row_copy(t, dst):
        return pltpu.make_async_copy(h_ref.at[pl.ds(pl.multiple_of(t * s, s), s)],
                                     out_hbm.at[pl.ds(pl.multiple_of(dst, s), s)], sem)

    def issue(t, carry):
        for k in range(TOP_K):
            row_copy(t, p_ref[0, 0, TOP_K * t + k]).start(priority=k)
        return carry

    lax.fori_loop(0, tm, issue, 0, unroll=DMA_UNROLL)

    def drain(t, carry):
        for k in range(TOP_K):
            row_copy(0, 0).wait()
        return carry

    lax.fori_loop(0, tm, drain, 0, unroll=DMA_UNROLL)


def dispatch(h, pos, zrow, rows, tm, s):
    nt = h.shape[0] // (tm * s)
    kern = functools.partial(_dispatch_kernel, tm=tm, experts=zrow.shape[0], s=s)
    return pl.pallas_call(
        kern,
        grid_spec=pltpu.PrefetchScalarGridSpec(
            num_scalar_prefetch=1,
            grid=(nt,),
            in_specs=[pl.BlockSpec((1, 1, TOP_K * tm), lambda i, zrow: (i, 0, 0), memory_space=pltpu.SMEM),
                      pl.BlockSpec((tm * s, LANES), lambda i, zrow: (i, 0))],
            out_specs=pl.BlockSpec(memory_space=pl.ANY),
            scratch_shapes=[pltpu.VMEM((MOE_SUB * s, LANES), h.dtype),
                            pltpu.SemaphoreType.DMA(()),
                            pltpu.SemaphoreType.DMA(())]),
        out_shape=jax.ShapeDtypeStruct((rows * s, LANES), h.dtype),
        compiler_params=_cparams("arbitrary"),
        name="moe_dispatch",
    )(zrow, pos.reshape(nt, 1, TOP_K * tm), h)


def _expert_full_kernel(nfull_ref, frow_ref, fexp_ref, x_ref, wg_ref, wu_ref, wd_ref, o_ref, h_ref, acc_ref,
                        *, s):
    del frow_ref, fexp_ref
    j = pl.program_id(0)
    f = pl.program_id(1)
    active = j < nfull_ref[0]
    rows = h_ref.shape[0]

    @pl.when(jnp.logical_not(active) & (f == 0))
    def _():
        o_ref[...] = jnp.zeros_like(o_ref)

    @pl.when(active & (f == 0))
    def _():
        h_ref[...] = _load_row_tiles(x_ref, 0, rows, s).astype(BF16)
        acc_ref[...] = jnp.zeros_like(acc_ref)

    @pl.when(active)
    def _():
        h = h_ref[...]
        a = jnp.dot(h, wg_ref[0], preferred_element_type=F32)
        u = jnp.dot(h, wu_ref[0], preferred_element_type=F32)
        act = (_silu(a) * u).astype(BF16)
        acc_ref[...] += jnp.dot(act, wd_ref[0], preferred_element_type=F32)

    @pl.when(active & (f == pl.num_programs(1) - 1))
    def _():
        _store_row_tiles(o_ref, 0, acc_ref[...])


def _expert_tail_kernel(trow_ref, tsub_ref, x_ref, wg_ref, wu_ref, wd_ref, prev_ref, o_ref, h_ref, acc_ref,
                        *, sub, nsub_max, s):
    del trow_ref, prev_ref
    f = pl.program_id(1)
    nsub = tsub_ref[pl.program_id(0)]

    @pl.when(f == 0)
    def _():
        o_ref[...] = jnp.zeros_like(o_ref)
        acc_ref[...] = jnp.zeros_like(acc_ref)

    for sb in range(nsub_max):
        @pl.when(sb < nsub)
        def _():
            rows = slice(sb * sub, (sb + 1) * sub)

            @pl.when(f == 0)
            def _():
                h_ref[rows] = _load_row_tiles(x_ref, sb * sub, sub, s).astype(BF16)

            h = h_ref[rows]
            a = jnp.dot(h, wg_ref[0], preferred_element_type=F32)
            u = jnp.dot(h, wu_ref[0], preferred_element_type=F32)
            act = (_silu(a) * u).astype(BF16)
            acc_ref[rows] += jnp.dot(act, wd_ref[0], preferred_element_type=F32)

            @pl.when(f == pl.num_programs(1) - 1)
            def _():
                _store_row_tiles(o_ref, sb * sub, acc_ref[rows])


def expert_ffn(xs, nfull, frow, fexp, trow, tsub, wg, wu, wd):
    experts, d, fdim = wg.shape
    s = d // LANES
    tf = MOE_F_TILE
    nf = fdim // tf
    nfull_max = frow.shape[0]
    tile = (MOE_ROW_TILE * s, LANES)
    scratch = [pltpu.VMEM((MOE_ROW_TILE, d), BF16), pltpu.VMEM((MOE_ROW_TILE, d), F32)]

    def fcol_full(j, f, nfull):
        return jnp.where(j < nfull[0], f, nf - 1)

    ys = pl.pallas_call(
        functools.partial(_expert_full_kernel, s=s),
        grid_spec=pltpu.PrefetchScalarGridSpec(
            num_scalar_prefetch=3,
            grid=(nfull_max, nf),
            in_specs=[pl.BlockSpec(tile, lambda j, f, nfull, frow, fexp: (frow[j], 0)),
                      pl.BlockSpec((1, d, tf), lambda j, f, nfull, frow, fexp: (fexp[j], 0, fcol_full(j, f, nfull))),
                      pl.BlockSpec((1, d, tf), lambda j, f, nfull, frow, fexp: (fexp[j], 0, fcol_full(j, f, nfull))),
                      pl.BlockSpec((1, tf, d), lambda j, f, nfull, frow, fexp: (fexp[j], fcol_full(j, f, nfull), 0))],
            out_specs=pl.BlockSpec(tile, lambda j, f, nfull, frow, fexp: (frow[j], 0)),
            scratch_shapes=scratch),
        out_shape=jax.ShapeDtypeStruct(xs.shape, F32),
        compiler_params=_cparams("arbitrary", "arbitrary"),
        name="expert_ffn_full",
    )(nfull, frow, fexp, xs, wg, wu, wd)

    def fcol_tail(e, f, tsub):
        return jnp.where(tsub[e] > 0, f, nf - 1)

    kern = functools.partial(_expert_tail_kernel, sub=MOE_SUB, nsub_max=MOE_ROW_TILE // MOE_SUB, s=s)
    return pl.pallas_call(
        kern,
        grid_spec=pltpu.PrefetchScalarGridSpec(
            num_scalar_prefetch=2,
            grid=(experts, nf),
            in_specs=[pl.BlockSpec(tile, lambda e, f, trow, tsub: (trow[e], 0)),
                      pl.BlockSpec((1, d, tf), lambda e, f, trow, tsub: (e, 0, fcol_tail(e, f, tsub))),
                      pl.BlockSpec((1, d, tf), lambda e, f, trow, tsub: (e, 0, fcol_tail(e, f, tsub))),
                      pl.BlockSpec((1, tf, d), lambda e, f, trow, tsub: (e, fcol_tail(e, f, tsub), 0)),
                      pl.BlockSpec(memory_space=pl.ANY)],
            out_specs=pl.BlockSpec(tile, lambda e, f, trow, tsub: (trow[e], 0)),
            scratch_shapes=scratch),
        out_shape=jax.ShapeDtypeStruct(xs.shape, F32),
        input_output_aliases={6: 0},
        compiler_params=_cparams("arbitrary", "arbitrary"),
        name="expert_ffn_tail",
    )(trow, tsub, xs, wg, wu, wd, ys)


def _combine_kernel(p_ref, x_ref, route_ref, fg_ref, ys_hbm, op_ref, os_ref, gath_ref, sem, *, tm, s):
    def row_copy(src, k, t):
        return pltpu.make_async_copy(ys_hbm.at[pl.ds(pl.multiple_of(src, s), s)],
                                     gath_ref.at[k, pl.ds(pl.multiple_of(t * s, s), s)], sem)

    def issue(t, carry):
        for k in range(TOP_K):
            row_copy(p_ref[0, 0, TOP_K * t + k], k, t).start(priority=k)
        return carry

    lax.fori_loop(0, tm, issue, 0, unroll=DMA_UNROLL)

    def drain(t, carry):
        for k in range(TOP_K):
            row_copy(0, 0, 0).wait()
        return carry

    lax.fori_loop(0, tm, drain, 0, unroll=DMA_UNROLL)
    route = route_ref[...]
    x = (x_ref[...] + route[:, 2:3] * _load_row_tiles(gath_ref.at[0], 0, tm, s)
         + route[:, 3:4] * _load_row_tiles(gath_ref.at[1], 0, tm, s))
    y = _rms(x, fg_ref[...])
    op_ref[...] = y

    @pl.when(pl.program_id(0) == pl.num_programs(0) - 1)
    def _():
        os_ref[...] = y[tm - os_ref.shape[0]:]


def combine(x, route, ys, pos, final_g, tm, n_sample):
    t, d = x.shape
    nt = t // tm
    s = d // LANES
    assert n_sample <= tm
    kern = functools.partial(_combine_kernel, tm=tm, s=s)
    return pl.pallas_call(
        kern,
        grid=(nt,),
        in_specs=[pl.BlockSpec((1, 1, TOP_K * tm), lambda i: (i, 0, 0), memory_space=pltpu.SMEM),
                  pl.BlockSpec((tm, d), lambda i: (i, 0)),
                  pl.BlockSpec((tm, LANES), lambda i: (i, 0)),
                  pl.BlockSpec((1, d), lambda i: (0, 0)),
                  pl.BlockSpec(memory_space=pl.ANY)],
        out_specs=[pl.BlockSpec((tm, d), lambda i: (i, 0)),
                   pl.BlockSpec((n_sample, d), lambda i: (0, 0))],
        out_shape=[jax.ShapeDtypeStruct((t - n_sample, d), F32),
                   jax.ShapeDtypeStruct((n_sample, d), F32)],
        scratch_shapes=[pltpu.VMEM((TOP_K, tm * s, LANES), F32),
                        pltpu.SemaphoreType.DMA(())],
        compiler_params=_cparams("arbitrary"),
        name="moe_combine",
    )(pos.reshape(nt, 1, TOP_K * tm), x, route, final_g.reshape(1, d), ys)


def moe_residual_final_norm(a, w_out, x_in, norm_g, router_w, wg, wu, wd, final_g, tm, n_sample):
    t, d = x_in.shape
    experts = router_w.shape[1]
    x, h, route, cnt = outproj_router(a, w_out, x_in, norm_g, router_w, tm)
    counts = cnt[0, :experts].astype(I32)
    ntiles = (TOP_K * t + experts * (MOE_ROW_TILE - 1)) // MOE_ROW_TILE + 1
    nfull_max = max(TOP_K * t // MOE_ROW_TILE, 1)
    s = d // LANES
    frow, fexp, nfull, trow, tsub, zrow, pos = plan(counts, route, ntiles, nfull_max, s)
    pos = pos[:, 4:4 + TOP_K]
    xs = dispatch(h, pos, zrow, ntiles * MOE_ROW_TILE, tm, s)
    ys = expert_ffn(xs, nfull, frow, fexp, trow, tsub, wg, wu, wd)
    return combine(x, route, ys, pos, final_g, tm, n_sample)


def _rope_tables(pos, half):
    inv = jnp.power(ROPE_THETA, -jnp.arange(half, dtype=F32) / half)
    ang = pos[:, None] * inv[None, :]
    return jnp.cos(ang), jnp.sin(ang)


def kernel(x_prompt, x_sample, state_retention, state_hgrn, norm_mix_g, norm_ffn_g, final_norm_g, ret_w_in, ret_gn_g, ret_w_out, hg_w_in, hg_lb_param, hg_norm_g, hg_w_out, ffn_w_gate, ffn_w_up, ffn_w_down, moe_router, moe_w_gate, moe_w_up, moe_w_down):
    bp, lp, d = x_prompt.shape
    bs, ls, _ = x_sample.shape
    assert ls == 1 and norm_mix_g.shape[0] == 2
    _, _, ret_heads, ret_dk, ret_dv = state_retention.shape
    _, _, hg_heads, hg_dk, _ = state_hgrn.shape
    tp = bp * lp
    t = tp + bs
    tm = _pick_tile(t, 768, 16)

    xp = x_prompt.reshape(tp, d)
    xs = x_sample.reshape(bs, d)

    log_gamma = jnp.log1p(-jnp.exp2(-5.0 - jnp.arange(ret_heads, dtype=F32)))
    cos_p, sin_p = _rope_tables(jnp.arange(lp, dtype=F32), ret_dk // 2)
    cos_s, sin_s = _rope_tables(PAST_LEN + jnp.arange(ls, dtype=F32), ret_dk // 2)
    tm_proj = _pick_tile(t, 1536, 16)
    proj = norm_matmul_merged(xp, xs, norm_mix_g[0], ret_w_in[0].astype(BF16), tm_proj,
                              _pick_tile(ret_w_in.shape[2], 1024, LANES))
    gated, ret_p = retention_scan(proj, log_gamma, cos_p, sin_p, ret_gn_g[0], batch=bp, seq=lp,
                                  heads=ret_heads, dk=ret_dk, dv=ret_dv, rows_total=t)
    gated, ret_s = retention_step(proj, gated, state_retention[0], log_gamma, cos_s, sin_s, ret_gn_g[0],
                                  row0=tp, nb=bs, heads=ret_heads, dk=ret_dk, dv=ret_dv)
    x = matmul_residual_merged(gated, ret_w_out[0].astype(BF16), xp, xs, tm)
    ff = ffn_w_gate.shape[2]
    x = ffn_residual(x, norm_ffn_g[0], ffn_w_gate[0].astype(BF16), ffn_w_up[0].astype(BF16),
                     ffn_w_down[0].astype(BF16), tm, _pick_tile(ff, 1536, LANES))

    proj = norm_matmul(x, norm_mix_g[1], hg_w_in[0].astype(BF16), tm_proj,
                       _pick_tile(hg_w_in.shape[2], 1024, LANES))
    experts, _, fe = moe_w_gate.shape[1:]
    gated, hg_p, wg, wu, wd = hgrn_scan(
        proj, hg_lb_param, hg_norm_g[0],
        (moe_w_gate[0].reshape(experts * d, fe), moe_w_up[0].reshape(experts * d, fe),
         moe_w_down[0].reshape(experts * fe, d)),
        batch=bp, seq=lp, heads=hg_heads, dk=hg_dk, rows_total=t, layer=1)
    gated, hg_s = hgrn_step(proj, gated, state_hgrn[0], hg_lb_param, hg_norm_g[0], row0=tp, nb=bs,
                            heads=hg_heads, dk=hg_dk, layer=1)
    y_p, y_s = moe_residual_final_norm(gated, hg_w_out[0].astype(BF16), x, norm_ffn_g[1], moe_router[0],
                                       wg.reshape(experts, d, fe), wu.reshape(experts, d, fe),
                                       wd.reshape(experts, fe, d), final_norm_g, tm, bs)

    return (y_p.reshape(bp, lp, d), y_s.reshape(bs, ls, d),
            ret_p[None], ret_s[None], hg_p[None], hg_s[None])
```

```python
import functools

import jax
import jax.numpy as jnp
from jax import lax
from jax.experimental import pallas as pl
from jax.experimental.pallas import tpu as pltpu

F32 = jnp.float32
BF16 = jnp.bfloat16
I32 = jnp.int32

NORM_EPS = 1e-6
GN_EPS = 1e-5
ROPE_THETA = 10000.0
PAST_LEN = 16384
TOP_K = 2

LANES = 128
VMEM_LIMIT = 56 * 1024 * 1024

RET_CHUNK = 256
HG_CHUNK = 128
HG_DIAG = 4
RET_SCAN_HEADS = 4
RET_STEP_SEQS = 4
HG_STEP_SEQS = 8
MOE_ROW_TILE = 1024
MOE_SUB = 256
MOE_F_TILE = 512
ROUTER_GROUPS = 2
DMA_UNROLL = 8

NT_DIMS = (((1,), (1,)), ((), ()))
TN_DIMS = (((0,), (0,)), ((), ()))


def _cparams(*sem):
    return pltpu.CompilerParams(dimension_semantics=sem, vmem_limit_bytes=VMEM_LIMIT)


def _pick_tile(n, target, mult):
    best = None
    for t in range(mult, min(n, target) + 1, mult):
        if n % t == 0:
            best = t
    assert best is not None, (n, target, mult)
    return best


def _sigmoid(x):
    return 0.5 * jnp.tanh(0.5 * x) + 0.5


def _silu(x):
    h = 0.5 * x
    return h * jnp.tanh(h) + h


def _rms(x, g):
    return x * lax.rsqrt(jnp.mean(x * x, axis=-1, keepdims=True) + NORM_EPS) * g


def _row_group(b):
    return pl.multiple_of(lax.shift_left(lax.shift_right_logical(b, 3), 3), 8), b & 7


def _load_row(ref, b, cols):
    base, r = _row_group(b)
    blk = ref[pl.ds(base, 8), cols]
    rows = lax.broadcasted_iota(I32, blk.shape, 0)
    return jnp.sum(jnp.where(rows == r, blk, 0.0), axis=0, keepdims=True)


def _store_row(ref, b, cols, row):
    base, r = _row_group(b)
    blk = ref[pl.ds(base, 8), cols]
    rows = lax.broadcasted_iota(I32, blk.shape, 0)
    ref[pl.ds(base, 8), cols] = jnp.where(rows == r, row, blk)


def _norm_matmul_kernel(x_ref, g_ref, w_ref, o_ref, h_ref):
    @pl.when(pl.program_id(1) == 0)
    def _():
        h_ref[...] = _rms(x_ref[...], g_ref[...]).astype(BF16)

    o_ref[...] = jnp.dot(h_ref[...], w_ref[...], preferred_element_type=F32).astype(o_ref.dtype)


def norm_matmul(x, g, w, tm, tn):
    t, d = x.shape
    n = w.shape[1]
    return pl.pallas_call(
        _norm_matmul_kernel,
        grid=(t // tm, n // tn),
        in_specs=[pl.BlockSpec((tm, d), lambda i, j: (i, 0)),
                  pl.BlockSpec((1, d), lambda i, j: (0, 0)),
                  pl.BlockSpec((d, tn), lambda i, j: (0, j))],
        out_specs=pl.BlockSpec((tm, tn), lambda i, j: (i, j)),
        out_shape=jax.ShapeDtypeStruct((t, n), BF16),
        scratch_shapes=[pltpu.VMEM((tm, d), BF16)],
        compiler_params=_cparams("parallel", "arbitrary"),
        name="norm_matmul",
    )(x, g.reshape(1, d), w)


def _with_merged_rows(xp_ref, xs_ref, fn):
    i = pl.program_id(0)
    last = pl.num_programs(0) - 1
    n_prompt = xp_ref.shape[0] - xs_ref.shape[0]

    @pl.when(i < last)
    def _():
        fn(xp_ref[...])

    @pl.when(i == last)
    def _():
        fn(jnp.concatenate([xp_ref[:n_prompt], xs_ref[...]], axis=0))


def _merged_specs(tm, d, bs, nidx):
    if nidx == 1:
        return [pl.BlockSpec((tm, d), lambda i: (i, 0)), pl.BlockSpec((bs, d), lambda i: (0, 0))]
    return [pl.BlockSpec((tm, d), lambda i, j: (i, 0)), pl.BlockSpec((bs, d), lambda i, j: (0, 0))]


def _norm_matmul2_kernel(xp_ref, xs_ref, g_ref, w_ref, o_ref, h_ref):
    @pl.when(pl.program_id(1) == 0)
    def _():
        def fill(x):
            h_ref[...] = _rms(x, g_ref[...]).astype(BF16)

        _with_merged_rows(xp_ref, xs_ref, fill)

    o_ref[...] = jnp.dot(h_ref[...], w_ref[...], preferred_element_type=F32).astype(o_ref.dtype)


def norm_matmul_merged(xp, xs, g, w, tm, tn):
    tp, d = xp.shape
    bs = xs.shape[0]
    t = tp + bs
    n = w.shape[1]
    assert t % tm == 0 and bs <= tm
    return pl.pallas_call(
        _norm_matmul2_kernel,
        grid=(t // tm, n // tn),
        in_specs=_merged_specs(tm, d, bs, 2) + [pl.BlockSpec((1, d), lambda i, j: (0, 0)),
                                                 pl.BlockSpec((d, tn), lambda i, j: (0, j))],
        out_specs=pl.BlockSpec((tm, tn), lambda i, j: (i, j)),
        out_shape=jax.ShapeDtypeStruct((t, n), BF16),
        scratch_shapes=[pltpu.VMEM((tm, d), BF16)],
        compiler_params=_cparams("parallel", "arbitrary"),
        name="norm_matmul_merged",
    )(xp, xs, g.reshape(1, d), w)


def _matmul_res2_kernel(a_ref, w_ref, xp_ref, xs_ref, o_ref):
    def fill(x):
        o_ref[...] = x + jnp.dot(a_ref[...], w_ref[...], preferred_element_type=F32)

    _with_merged_rows(xp_ref, xs_ref, fill)


def matmul_residual_merged(a, w, xp, xs, tm):
    t, k = a.shape
    n = w.shape[1]
    bs = xs.shape[0]
    assert t % tm == 0 and bs <= tm and xp.shape[0] + bs == t
    return pl.pallas_call(
        _matmul_res2_kernel,
        grid=(t // tm,),
        in_specs=[pl.BlockSpec((tm, k), lambda i: (i, 0)),
                  pl.BlockSpec((k, n), lambda i: (0, 0))] + _merged_specs(tm, n, bs, 1),
        out_specs=pl.BlockSpec((tm, n), lambda i: (i, 0)),
        out_shape=jax.ShapeDtypeStruct((t, n), F32),
        compiler_params=_cparams("parallel"),
        name="matmul_residual_merged",
    )(a, w, xp, xs)


def _ffn_kernel(x_ref, g_ref, wg_ref, wu_ref, wd_ref, o_ref, h_ref):
    f = pl.program_id(1)

    @pl.when(f == 0)
    def _():
        x = x_ref[...]
        h_ref[...] = _rms(x, g_ref[...]).astype(BF16)
        o_ref[...] = x

    h = h_ref[...]
    a = jnp.dot(h, wg_ref[...], preferred_element_type=F32)
    u = jnp.dot(h, wu_ref[...], preferred_element_type=F32)
    act = (_silu(a) * u).astype(BF16)
    o_ref[...] += jnp.dot(act, wd_ref[...], preferred_element_type=F32)


def ffn_residual(x, g, wg, wu, wd, tm, tf):
    t, d = x.shape
    f = wg.shape[1]
    return pl.pallas_call(
        _ffn_kernel,
        grid=(t // tm, f // tf),
        in_specs=[pl.BlockSpec((tm, d), lambda i, j: (i, 0)),
                  pl.BlockSpec((1, d), lambda i, j: (0, 0)),
                  pl.BlockSpec((d, tf), lambda i, j: (0, j)),
                  pl.BlockSpec((d, tf), lambda i, j: (0, j)),
                  pl.BlockSpec((tf, d), lambda i, j: (j, 0))],
        out_specs=pl.BlockSpec((tm, d), lambda i, j: (i, 0)),
        out_shape=jax.ShapeDtypeStruct((t, d), F32),
        scratch_shapes=[pltpu.VMEM((tm, d), BF16)],
        compiler_params=_cparams("parallel", "arbitrary"),
        name="ffn_residual",
    )(x, g.reshape(1, d), wg, wu, wd)


def _rotary(x, cos, sin, half):
    x1 = x[:, :half]
    x2 = x[:, half:]
    return jnp.concatenate([x1 * cos - x2 * sin, x2 * cos + x1 * sin], axis=-1)


def _group_norm_gate(o, gate, gain):
    mu = jnp.mean(o, axis=-1, keepdims=True)
    d = o - mu
    var = jnp.mean(d * d, axis=-1, keepdims=True)
    return _silu(gate) * (d * lax.rsqrt(var + GN_EPS) * gain)


def _ret_scan_kernel(lg_ref, q_ref, k_ref, v_ref, g_ref, cos_ref, sin_ref, gn_ref,
                     o_ref, sfin_ref, s_ref, mask_ref, din_ref, dout_ref, *, chunk, dk, dv, hp):
    c = pl.program_id(2)
    head0 = pl.program_id(1) * hp

    @pl.when(c == 0)
    def _():
        s_ref[...] = jnp.zeros_like(s_ref)
        t = lax.broadcasted_iota(I32, (chunk, LANES), 0).astype(F32)
        ti = lax.broadcasted_iota(I32, (chunk, chunk), 0)
        si = lax.broadcasted_iota(I32, (chunk, chunk), 1)
        causal = ti >= si
        rel = jnp.where(causal, (ti - si).astype(F32), 0.0)
        for j in range(hp):
            lg = lg_ref[head0 + j]
            din_ref[j] = jnp.exp((t + 1.0) * lg)
            dout_ref[j] = jnp.exp((chunk - 1.0 - t) * lg)
            mask_ref[j] = jnp.where(causal, jnp.exp(rel * lg), 0.0)

    half = dk // 2
    cos = cos_ref[...]
    sin = sin_ref[...]
    for j in range(hp):
        qk_cols = slice(j * dk, (j + 1) * dk)
        v_cols = slice(j * dv, (j + 1) * dv)
        q = _rotary(q_ref[:, qk_cols].astype(F32), cos, sin, half)
        k = _rotary(k_ref[:, qk_cols].astype(F32), cos, sin, half) * (dk ** -0.5)
        v = v_ref[:, v_cols]
        decay_in = _lane_tile(din_ref[j], dk)
        decay_out = _lane_tile(dout_ref[j], dk)
        decay_chunk = jnp.exp(jnp.zeros((1, dv), F32) + chunk * lg_ref[head0 + j])

        s = s_ref[j]
        scores = (lax.dot_general(q.astype(BF16), k.astype(BF16), NT_DIMS, preferred_element_type=F32)
                  * mask_ref[j])
        o = (jnp.dot(scores.astype(BF16), v, preferred_element_type=F32)
             + jnp.dot((q * decay_in).astype(BF16), s.astype(BF16), preferred_element_type=F32))
        kd_t = (k * decay_out).T.astype(BF16)
        s_ref[j] = s * decay_chunk + jnp.dot(kd_t, v, preferred_element_type=F32)
        o_ref[:, v_cols] = _group_norm_gate(o, g_ref[:, v_cols].astype(F32), gn_ref[:, v_cols]).astype(BF16)

    @pl.when(c == pl.num_programs(2) - 1)
    def _():
        sfin_ref[0] = s_ref[...]


def retention_scan(proj, log_gamma, cos, sin, gn_g, *, batch, seq, heads, dk, dv, rows_total):
    chunk = min(RET_CHUNK, seq)
    nc = seq // chunk
    hp = min(RET_SCAN_HEADS, heads)
    assert heads % hp == 0
    ng = heads // hp
    kern = functools.partial(_ret_scan_kernel, chunk=chunk, dk=dk, dv=dv, hp=hp)
    v0 = 2 * heads * dk // (hp * dv)
    return pl.pallas_call(
        kern,
        grid=(batch, ng, nc),
        in_specs=[pl.BlockSpec(memory_space=pltpu.SMEM),
                  pl.BlockSpec((chunk, hp * dk), lambda b, h, c: (b * nc + c, h)),
                  pl.BlockSpec((chunk, hp * dk), lambda b, h, c: (b * nc + c, ng + h)),
                  pl.BlockSpec((chunk, hp * dv), lambda b, h, c: (b * nc + c, v0 + h)),
                  pl.BlockSpec((chunk, hp * dv), lambda b, h, c: (b * nc + c, v0 + ng + h)),
                  pl.BlockSpec((chunk, dk // 2), lambda b, h, c: (c, 0)),
                  pl.BlockSpec((chunk, dk // 2), lambda b, h, c: (c, 0)),
                  pl.BlockSpec((1, hp * dv), lambda b, h, c: (0, h))],
        out_specs=[pl.BlockSpec((chunk, hp * dv), lambda b, h, c: (b * nc + c, h)),
                   pl.BlockSpec((1, hp, dk, dv), lambda b, h, c: (b, h, 0, 0))],
        out_shape=[jax.ShapeDtypeStruct((rows_total, heads * dv), BF16),
                   jax.ShapeDtypeStruct((batch, heads, dk, dv), F32)],
        scratch_shapes=[pltpu.VMEM((hp, dk, dv), F32),
                        pltpu.VMEM((hp, chunk, chunk), F32),
                        pltpu.VMEM((hp, chunk, LANES), F32),
                        pltpu.VMEM((hp, chunk, LANES), F32)],
        compiler_params=_cparams("parallel", "parallel", "arbitrary"),
        name="retention_scan",
    )(log_gamma, proj, proj, proj, proj, cos, sin, gn_g.reshape(1, heads * dv))


def _one_hot_rows(nb, b):
    return jnp.where(lax.broadcasted_iota(I32, (nb, LANES), 0) == b, 1.0, 0.0).astype(BF16)


def _lane_tile(x, width):
    return jnp.concatenate([x] * (width // x.shape[1]), axis=1)


def _ret_step_kernel(lg_ref, q_ref, k_ref, v_ref, g_ref, cos_ref, sin_ref, gn_ref, s_ref, prev_ref,
                     o_ref, snew_ref, qt_ref, kt_ref, vf_ref, gf_ref, orow_ref, *, heads, dk, dv, nb, per_step):
    del prev_ref
    step = pl.program_id(0)
    half = dk // 2

    @pl.when(step == 0)
    def _():
        orow_ref[...] = jnp.zeros_like(orow_ref)
        vf_ref[...] = v_ref[...].astype(F32)
        gf_ref[...] = g_ref[...].astype(F32)
        cos = cos_ref[...]
        sin = sin_ref[...]
        for h in range(heads):
            qh = _rotary(q_ref[:, h * dk:(h + 1) * dk].astype(F32), cos, sin, half)
            kh = _rotary(k_ref[:, h * dk:(h + 1) * dk].astype(F32), cos, sin, half) * (dk ** -0.5)
            qt_ref[h] = qh.T.astype(BF16)
            kt_ref[h] = kh.T.astype(BF16)

    for i in range(per_step):
        b = step * per_step + i
        onehot = _one_hot_rows(nb, b)
        for h in range(heads):
            cols = slice(h * dv, (h + 1) * dv)
            kcol = _lane_tile(jnp.dot(kt_ref[h], onehot, preferred_element_type=F32), dv)
            qcol = _lane_tile(jnp.dot(qt_ref[h], onehot, preferred_element_type=F32), dv)
            vrow = _load_row(vf_ref, b, cols)
            gamma = jnp.exp(jnp.zeros((1, dv), F32) + lg_ref[h])
            s_new = s_ref[i, h] * gamma + kcol * vrow
            snew_ref[i, h] = s_new
            o = jnp.sum(s_new * qcol, axis=0, keepdims=True)
            _store_row(orow_ref, b, cols, _group_norm_gate(o, _load_row(gf_ref, b, cols), gn_ref[:, cols]))

    @pl.when(step == pl.num_programs(0) - 1)
    def _():
        o_ref[...] = orow_ref[...].astype(BF16)


def retention_step(proj, gated, state, log_gamma, cos, sin, gn_g, *, row0, nb, heads, dk, dv):
    per_step = RET_STEP_SEQS
    kern = functools.partial(_ret_step_kernel, heads=heads, dk=dk, dv=dv, nb=nb, per_step=per_step)
    rb = row0 // nb
    wq = heads * dk
    wv = heads * dv
    return pl.pallas_call(
        kern,
        grid=(nb // per_step,),
        in_specs=[pl.BlockSpec(memory_space=pltpu.SMEM),
                  pl.BlockSpec((nb, wq), lambda b: (rb, 0)),
                  pl.BlockSpec((nb, wq), lambda b: (rb, 1)),
                  pl.BlockSpec((nb, wv), lambda b: (rb, 2 * wq // wv)),
                  pl.BlockSpec((nb, wv), lambda b: (rb, 2 * wq // wv + 1)),
                  pl.BlockSpec((1, dk // 2), lambda b: (0, 0)),
                  pl.BlockSpec((1, dk // 2), lambda b: (0, 0)),
                  pl.BlockSpec((1, wv), lambda b: (0, 0)),
                  pl.BlockSpec((per_step, heads, dk, dv), lambda b: (b, 0, 0, 0)),
                  pl.BlockSpec(memory_space=pl.ANY)],
        out_specs=[pl.BlockSpec((nb, wv), lambda b: (rb, 0)),
                   pl.BlockSpec((per_step, heads, dk, dv), lambda b: (b, 0, 0, 0))],
        out_shape=[jax.ShapeDtypeStruct(gated.shape, gated.dtype),
                   jax.ShapeDtypeStruct(state.shape, F32)],
        scratch_shapes=[pltpu.VMEM((heads, dk, nb), BF16),
                        pltpu.VMEM((heads, dk, nb), BF16),
                        pltpu.VMEM((nb, wv), F32),
                        pltpu.VMEM((nb, wv), F32),
                        pltpu.VMEM((nb, wv), F32)],
        input_output_aliases={9: 0},
        compiler_params=_cparams("arbitrary"),
        name="retention_step",
    )(log_gamma, proj, proj, proj, proj, cos, sin, gn_g.reshape(1, wv), state, gated)


def _lower_bound(lbp, layer):
    m = jnp.max(lbp, axis=0, keepdims=True)
    e = jnp.exp(lbp - m)
    p = e / jnp.sum(e, axis=0, keepdims=True)
    return jnp.sum(p[:layer + 1], axis=0, keepdims=True) - p[0:1]


def _split_dot(mat_bf16, x):
    hi = x.astype(BF16)
    r1 = x - hi.astype(F32)
    mid = r1.astype(BF16)
    lo = (r1 - mid.astype(F32)).astype(BF16)
    return (jnp.dot(mat_bf16, hi, preferred_element_type=F32)
            + jnp.dot(mat_bf16, mid, preferred_element_type=F32)
            + jnp.dot(mat_bf16, lo, preferred_element_type=F32))


def _rms_gate(o, gate, gain):
    return _silu(gate) * (o * lax.rsqrt(jnp.mean(o * o, axis=-1, keepdims=True) + NORM_EPS) * gain)


def _group_row(x, s, group):
    n, w = x.shape
    x3 = x.reshape(n // group, group, w)
    return jnp.broadcast_to(x3[:, s:s + 1, :], x3.shape).reshape(n, w)


def _hgrn_scan_kernel(*refs, chunk, heads, dk, diag, layer, ncast):
    lbp_ref, q_ref, f_ref, i_ref, g_ref, ng_ref = refs[:6]
    cast_in = refs[6:6 + ncast]
    o_ref, sfin_ref = refs[6 + ncast:8 + ncast]
    cast_out = refs[8 + ncast:8 + 2 * ncast]
    st_ref = refs[8 + 2 * ncast]
    c = pl.program_id(1)

    @pl.when(c == 0)
    def _():
        st_ref[...] = jnp.zeros_like(st_ref)

    for src, dst in zip(cast_in, cast_out):
        dst[...] = src[...].astype(BF16)

    w = heads * dk
    lb = _lower_bound(lbp_ref[...], layer)
    qs = _silu(q_ref[...].astype(F32))
    forget = lb + (1.0 - lb) * _sigmoid(f_ref[...].astype(F32))
    kk = 1.0 - forget
    logf = jnp.log(forget)

    row = lax.broadcasted_iota(I32, (chunk, chunk), 0)
    col = lax.broadcasted_iota(I32, (chunk, chunk), 1)
    tril = jnp.where(row >= col, 1.0, 0.0).astype(BF16)
    bcum = _split_dot(tril, logf)
    blast = bcum[chunk - 1:chunk, :]
    qe = (qs * jnp.exp(bcum)).astype(BF16)
    kdec = (kk * jnp.exp(blast - bcum)).astype(BF16)
    lk = jnp.log(kk) - bcum
    iv = i_ref[...]
    gate = g_ref[...].astype(F32)
    gain = ng_ref[...]

    u = jnp.where(row > col, row ^ col, 0)
    rowv = lax.broadcasted_iota(I32, (chunk, 1), 0)
    levels = []
    m = chunk // 2
    while m >= diag:
        span = 2 * m
        bref = jnp.concatenate(
            [jnp.broadcast_to(bcum[g * span + m - 1:g * span + m], (span, w)) for g in range(chunk // span)],
            axis=0)
        upper = (rowv & m) != 0
        d = bcum - bref
        x = (jnp.where(upper, qs, kk) * jnp.exp(jnp.where(upper, d, -d))).astype(BF16)
        levels.append((x, lax.shift_right_logical(u, m.bit_length() - 1) == 1))
        m //= 2
    in_block = row & (diag - 1)
    block_col0 = row - in_block
    diag_masks = [(col == block_col0 + s) & (in_block >= s) for s in range(diag)]
    ones = jnp.ones((dk, LANES), BF16)

    for h in range(heads):
        cols = slice(h * dk, (h + 1) * dk)
        qs_h = qs[:, cols]
        lk_h = lk[:, cols]
        b_h = bcum[:, cols]
        a = jnp.zeros((chunk, chunk), F32)
        for x, mask in levels:
            x_h = x[:, cols]
            a = jnp.where(mask, lax.dot_general(x_h, x_h, NT_DIMS, preferred_element_type=F32), a)
        ws = [qs_h * jnp.exp(b_h + _group_row(lk_h, s, diag)) for s in range(diag)]
        rsum = jnp.dot(jnp.concatenate(ws, axis=0).astype(BF16), ones, preferred_element_type=F32)
        for s in range(diag):
            a = jnp.where(diag_masks[s], rsum[s * chunk:(s + 1) * chunk], a)
        st = st_ref[h]
        i_h = iv[:, cols]
        o_h = (jnp.dot(a.astype(BF16), i_h, preferred_element_type=F32)
               + lax.dot_general(qe[:, cols], st.astype(BF16), NT_DIMS, preferred_element_type=F32))
        st_new = (st * jnp.exp(blast[:, cols])
                  + lax.dot_general(i_h, kdec[:, cols], TN_DIMS, preferred_element_type=F32))
        st_ref[h] = st_new
        o_ref[:, cols] = _rms_gate(o_h, gate[:, cols], gain[:, cols]).astype(BF16)

    @pl.when(c == pl.num_programs(1) - 1)
    def _():
        for h in range(heads):
            sfin_ref[0, h] = st_ref[h].T


def hgrn_scan(proj, lb_param, norm_g, side_f32, *, batch, seq, heads, dk, rows_total, layer):
    chunk = min(HG_CHUNK, seq)
    assert chunk == LANES and dk == LANES
    nc = seq // chunk
    w = heads * dk
    steps = batch * nc
    slabs = []
    for a in side_f32:
        assert a.shape[0] % (steps * 16) == 0, (a.shape, steps)
        slabs.append(pl.BlockSpec((a.shape[0] // steps, a.shape[1]), lambda b, c: (b * nc + c, 0)))
    kern = functools.partial(_hgrn_scan_kernel, chunk=chunk, heads=heads, dk=dk, diag=HG_DIAG, layer=layer,
                             ncast=len(side_f32))
    return pl.pallas_call(
        kern,
        grid=(batch, nc),
        in_specs=[pl.BlockSpec(lb_param.shape, lambda b, c: (0, 0)),
                  pl.BlockSpec((chunk, w), lambda b, c: (b * nc + c, 0)),
                  pl.BlockSpec((chunk, w), lambda b, c: (b * nc + c, 1)),
                  pl.BlockSpec((chunk, w), lambda b, c: (b * nc + c, 2)),
                  pl.BlockSpec((chunk, w), lambda b, c: (b * nc + c, 3)),
                  pl.BlockSpec((1, w), lambda b, c: (0, 0))] + slabs,
        out_specs=[pl.BlockSpec((chunk, w), lambda b, c: (b * nc + c, 0)),
                   pl.BlockSpec((1, heads, dk, dk), lambda b, c: (b, 0, 0, 0))] + slabs,
        out_shape=[jax.ShapeDtypeStruct((rows_total, w), BF16),
                   jax.ShapeDtypeStruct((batch, heads, dk, dk), F32)]
                  + [jax.ShapeDtypeStruct(a.shape, BF16) for a in side_f32],
        scratch_shapes=[pltpu.VMEM((heads, dk, dk), F32)],
        compiler_params=_cparams("parallel", "arbitrary"),
        name="hgrn_scan",
    )(lb_param, proj, proj, proj, proj, norm_g.reshape(1, w), *side_f32)


def _hgrn_step_kernel(lbp_ref, q_ref, f_ref, i_ref, g_ref, ng_ref, s_ref, prev_ref,
                      o_ref, snew_ref, qt_ref, ft_ref, if_ref, gf_ref, orow_ref,
                      *, heads, dk, nb, layer, per_step):
    del prev_ref
    step = pl.program_id(0)

    @pl.when(step == 0)
    def _():
        orow_ref[...] = jnp.zeros_like(orow_ref)
        if_ref[...] = i_ref[...].astype(F32)
        gf_ref[...] = g_ref[...].astype(F32)
        lb = _lower_bound(lbp_ref[...], layer)
        qs = _silu(q_ref[...].astype(F32))
        forget = lb + (1.0 - lb) * _sigmoid(f_ref[...].astype(F32))
        for h in range(heads):
            cols = slice(h * dk, (h + 1) * dk)
            qt_ref[h] = qs[:, cols].T.astype(BF16)
            ft = forget[:, cols].T
            hi = ft.astype(BF16)
            r1 = ft - hi.astype(F32)
            mid = r1.astype(BF16)
            ft_ref[0, h] = hi
            ft_ref[1, h] = mid
            ft_ref[2, h] = (r1 - mid.astype(F32)).astype(BF16)

    for i in range(per_step):
        b = step * per_step + i
        onehot = _one_hot_rows(nb, b)
        for h in range(heads):
            cols = slice(h * dk, (h + 1) * dk)
            fcol = (jnp.dot(ft_ref[0, h], onehot, preferred_element_type=F32)
                    + jnp.dot(ft_ref[1, h], onehot, preferred_element_type=F32)
                    + jnp.dot(ft_ref[2, h], onehot, preferred_element_type=F32))
            qcol = jnp.dot(qt_ref[h], onehot, preferred_element_type=F32)
            irow = _load_row(if_ref, b, cols)
            s_new = s_ref[i, h] * fcol + (1.0 - fcol) * irow
            snew_ref[i, h] = s_new
            o = jnp.sum(s_new * qcol, axis=0, keepdims=True)
            _store_row(orow_ref, b, cols, _rms_gate(o, _load_row(gf_ref, b, cols), ng_ref[:, cols]))

    @pl.when(step == pl.num_programs(0) - 1)
    def _():
        o_ref[...] = orow_ref[...].astype(BF16)


def hgrn_step(proj, gated, state, lb_param, norm_g, *, row0, nb, heads, dk, layer):
    per_step = HG_STEP_SEQS
    kern = functools.partial(_hgrn_step_kernel, heads=heads, dk=dk, nb=nb, layer=layer, per_step=per_step)
    rb = row0 // nb
    w = heads * dk
    return pl.pallas_call(
        kern,
        grid=(nb // per_step,),
        in_specs=[pl.BlockSpec(lb_param.shape, lambda b: (0, 0)),
                  pl.BlockSpec((nb, w), lambda b: (rb, 0)),
                  pl.BlockSpec((nb, w), lambda b: (rb, 1)),
                  pl.BlockSpec((nb, w), lambda b: (rb, 2)),
                  pl.BlockSpec((nb, w), lambda b: (rb, 3)),
                  pl.BlockSpec((1, w), lambda b: (0, 0)),
                  pl.BlockSpec((per_step, heads, dk, dk), lambda b: (b, 0, 0, 0)),
                  pl.BlockSpec(memory_space=pl.ANY)],
        out_specs=[pl.BlockSpec((nb, w), lambda b: (rb, 0)),
                   pl.BlockSpec((per_step, heads, dk, dk), lambda b: (b, 0, 0, 0))],
        out_shape=[jax.ShapeDtypeStruct(gated.shape, gated.dtype),
                   jax.ShapeDtypeStruct(state.shape, F32)],
        scratch_shapes=[pltpu.VMEM((heads, dk, nb), BF16),
                        pltpu.VMEM((3, heads, dk, nb), BF16),
                        pltpu.VMEM((nb, w), F32),
                        pltpu.VMEM((nb, w), F32),
                        pltpu.VMEM((nb, w), F32)],
        input_output_aliases={7: 0},
        compiler_params=_cparams("arbitrary"),
        name="hgrn_step",
    )(lb_param, proj, proj, proj, proj, norm_g.reshape(1, w), state, gated)


def _store_row_tiles(ref, r0, x):
    n, d = x.shape
    s = d // LANES
    for c in range(s):
        ref[pl.ds(s * r0 + c, n, stride=s), :] = x[:, c * LANES:(c + 1) * LANES]


def _load_row_tiles(ref, r0, n, s):
    return jnp.concatenate([ref[pl.ds(s * r0 + c, n, stride=s), :] for c in range(s)], axis=1)


def _router_kernel(a_ref, w_ref, x_ref, g_ref, r_ref, xo_ref, h_ref, route_ref, cnt_ref,
                   carry_ref, rhi_ref, rlo_ref, *, tm, experts):
    i = pl.program_id(0)

    @pl.when(i == 0)
    def _():
        carry_ref[...] = jnp.zeros_like(carry_ref)
        r = r_ref[...]
        rhi = r.astype(BF16)
        rhi_ref[...] = rhi
        rlo_ref[...] = (r - rhi.astype(F32)).astype(BF16)

    step = (tm // ROUTER_GROUPS + 15) // 16 * 16
    groups = tuple((r, min(r + step, tm)) for r in range(0, tm, step))
    all_logits = []
    for r0, r1 in groups:
        rows = slice(r0, r1)
        x = x_ref[rows] + jnp.dot(a_ref[rows], w_ref[...], preferred_element_type=F32)
        xo_ref[rows] = x
        h = _rms(x, g_ref[...])
        _store_row_tiles(h_ref, r0, h)
        hhi = h.astype(BF16)
        hlo = (h - hhi.astype(F32)).astype(BF16)
        all_logits.append(jnp.dot(hhi, rhi_ref[...], preferred_element_type=F32)
                          + jnp.dot(hlo, rhi_ref[...], preferred_element_type=F32)
                          + jnp.dot(hhi, rlo_ref[...], preferred_element_type=F32))
    total = carry_ref[...]
    for (r0, r1), logits in zip(groups, all_logits):
        n = r1 - r0
        rows = slice(r0, r1)
        lane = lax.broadcasted_iota(I32, (n, LANES), 1)
        valid = lane < experts
        z = jnp.where(valid, logits, -jnp.inf)
        ez = jnp.exp(z - jnp.max(z, axis=-1, keepdims=True))
        p = ez / jnp.sum(ez, axis=-1, keepdims=True)
        p = jnp.where(valid, p, -1.0)
        lane_f = lane.astype(F32)
        v1 = jnp.max(p, axis=-1, keepdims=True)
        i1 = jnp.min(jnp.where(p == v1, lane_f, float(LANES)), axis=-1, keepdims=True)
        p2 = jnp.where(lane_f == i1, -1.0, p)
        v2 = jnp.max(p2, axis=-1, keepdims=True)
        i2 = jnp.min(jnp.where(p2 == v2, lane_f, float(LANES)), axis=-1, keepdims=True)
        den = v1 + v2
        m0 = jnp.where(lane_f == i1, 1.0, 0.0)
        m1 = jnp.where(lane_f == i2, 1.0, 0.0)
        msum = m0 + m1
        ri = lax.broadcasted_iota(I32, (n, n), 0)
        ci = lax.broadcasted_iota(I32, (n, n), 1)
        strict = jnp.where(ri > ci, 1.0, 0.0).astype(BF16)
        before = jnp.dot(strict, msum.astype(BF16), preferred_element_type=F32) + total
        rank0 = jnp.sum(m0 * before, axis=-1, keepdims=True)
        rank1 = jnp.sum(m1 * before, axis=-1, keepdims=True)
        total = total + jnp.sum(msum, axis=0, keepdims=True)
        route_ref[rows] = jnp.where(lane == 0, i1,
                          jnp.where(lane == 1, i2,
                          jnp.where(lane == 2, v1 / den,
                          jnp.where(lane == 3, v2 / den,
                          jnp.where(lane == 4, rank0,
                          jnp.where(lane == 5, rank1, 0.0))))))
    carry_ref[...] = total
    cnt_ref[...] = jnp.broadcast_to(total, cnt_ref.shape)


def outproj_router(a, w, x, g, router_w, tm):
    t, d = x.shape
    k = a.shape[1]
    experts = router_w.shape[1]
    rpad = jnp.zeros((d, LANES), F32).at[:, :experts].set(router_w)
    kern = functools.partial(_router_kernel, tm=tm, experts=experts)
    return pl.pallas_call(
        kern,
        grid=(t // tm,),
        in_specs=[pl.BlockSpec((tm, k), lambda i: (i, 0)),
                  pl.BlockSpec((k, d), lambda i: (0, 0)),
                  pl.BlockSpec((tm, d), lambda i: (i, 0)),
                  pl.BlockSpec((1, d), lambda i: (0, 0)),
                  pl.BlockSpec((d, LANES), lambda i: (0, 0))],
        out_specs=[pl.BlockSpec((tm, d), lambda i: (i, 0)),
                   pl.BlockSpec((tm * (d // LANES), LANES), lambda i: (i, 0)),
                   pl.BlockSpec((tm, LANES), lambda i: (i, 0)),
                   pl.BlockSpec((8, LANES), lambda i: (0, 0))],
        out_shape=[jax.ShapeDtypeStruct((t, d), F32),
                   jax.ShapeDtypeStruct((t * (d // LANES), LANES), F32),
                   jax.ShapeDtypeStruct((t, LANES), F32),
                   jax.ShapeDtypeStruct((8, LANES), F32)],
        scratch_shapes=[pltpu.VMEM((1, LANES), F32),
                        pltpu.VMEM((d, LANES), BF16),
                        pltpu.VMEM((d, LANES), BF16)],
        compiler_params=_cparams("arbitrary"),
        name="outproj_router",
    )(a, w, x, g.reshape(1, d), rpad)


def _plan_kernel(cnt_ref, route_ref, frow_ref, fexp_ref, nfull_ref, trow_ref, tsub_ref, zrow_ref, pos_ref,
                 *, experts, nfull_max, dummy, row_tile, sub, row_scale):
    tile0 = jnp.int32(0)
    nfull = jnp.int32(0)
    last = jnp.int32(0)
    route = route_ref[...]
    slot_expert = pltpu.roll(route, 4, axis=1)
    first_row = jnp.zeros_like(route)
    for e in range(experts):
        n = cnt_ref[e]
        first_row = jnp.where(slot_expert == float(e), (tile0 * row_tile).astype(F32), first_row)
        full = n // row_tile
        rem = n - full * row_tile

        def fill(j, carry, e=e, tile0=tile0, nfull=nfull):
            frow_ref[nfull + j] = tile0 + j
            fexp_ref[nfull + j] = e
            return carry

        lax.fori_loop(0, full, fill, 0)
        trow_ref[e] = jnp.where(rem > 0, tile0 + full, dummy)
        tsub_ref[e] = (rem + (sub - 1)) // sub
        zrow_ref[e] = tile0 * row_tile + (n // sub) * sub
        last = jnp.where(full > 0, e, last)
        nfull = nfull + full
        tile0 = tile0 + full + jnp.where(rem > 0, 1, 0)
    nfull_ref[0] = nfull
    pos_ref[...] = ((first_row + route) * float(row_scale)).astype(I32)

    def unused(j, carry):
        frow_ref[j] = dummy
        fexp_ref[j] = last
        return carry

    lax.fori_loop(nfull, nfull_max, unused, 0)


def plan(counts, route, ntiles, nfull_max, row_scale):
    experts = counts.shape[0]
    kern = functools.partial(_plan_kernel, experts=experts, nfull_max=nfull_max, dummy=ntiles - 1,
                             row_tile=MOE_ROW_TILE, sub=MOE_SUB, row_scale=row_scale)
    smem = pl.BlockSpec(memory_space=pltpu.SMEM)
    vmem = pl.BlockSpec(memory_space=pltpu.VMEM)
    return pl.pallas_call(
        kern,
        in_specs=[smem, vmem],
        out_specs=[smem] * 6 + [vmem],
        out_shape=[jax.ShapeDtypeStruct((nfull_max,), I32),
                   jax.ShapeDtypeStruct((nfull_max,), I32),
                   jax.ShapeDtypeStruct((1,), I32),
                   jax.ShapeDtypeStruct((experts,), I32),
                   jax.ShapeDtypeStruct((experts,), I32),
                   jax.ShapeDtypeStruct((experts,), I32),
                   jax.ShapeDtypeStruct(route.shape, I32)],
        compiler_params=pltpu.CompilerParams(vmem_limit_bytes=VMEM_LIMIT),
        name="moe_plan",
    )(counts, route)


def _dispatch_kernel(zrow_ref, p_ref, h_ref, out_hbm, zero_ref, sem, zsem, *, tm, experts, s):
    @pl.when(pl.program_id(0) == 0)
    def _():
        zero_ref[...] = jnp.zeros_like(zero_ref)

        def zero_copy(e):
            row = pl.multiple_of(zrow_ref[e] * s, MOE_SUB * s)
            return pltpu.make_async_copy(zero_ref, out_hbm.at[pl.ds(row, MOE_SUB * s)], zsem)

        for e in range(experts):
            zero_copy(e).start()
        for e in range(experts):
            zero_copy(e).wait()

    def row_copy(t, dst):
        return pltpu.make_async_copy(h_ref.at[pl.ds(pl.multiple_of(t * s, s), s)],
                                     out_hbm.at[pl.ds(pl.multiple_of(dst, s), s)], sem)

    def issue(t, carry):
        for k in range(TOP_K):
            row_copy(t, p_ref[0, 0, TOP_K * t + k]).start(priority=k)
        return carry

    lax.fori_loop(0, tm, issue, 0, unroll=DMA_UNROLL)

    def drain(t, carry):
        for k in range(TOP_K):
            row_copy(0, 0).wait()
        return carry

    lax.fori_loop(0, tm, drain, 0, unroll=DMA_UNROLL)


def dispatch(h, pos, zrow, rows, tm, s):
    nt = h.shape[0] // (tm * s)
    kern = functools.partial(_dispatch_kernel, tm=tm, experts=zrow.shape[0], s=s)
    return pl.pallas_call(
        kern,
        grid_spec=pltpu.PrefetchScalarGridSpec(
            num_scalar_prefetch=1,
            grid=(nt,),
            in_specs=[pl.BlockSpec((1, 1, TOP_K * tm), lambda i, zrow: (i, 0, 0), memory_space=pltpu.SMEM),
                      pl.BlockSpec((tm * s, LANES), lambda i, zrow: (i, 0))],
            out_specs=pl.BlockSpec(memory_space=pl.ANY),
            scratch_shapes=[pltpu.VMEM((MOE_SUB * s, LANES), h.dtype),
                            pltpu.SemaphoreType.DMA(()),
                            pltpu.SemaphoreType.DMA(())]),
        out_shape=jax.ShapeDtypeStruct((rows * s, LANES), h.dtype),
        compiler_params=_cparams("arbitrary"),
        name="moe_dispatch",
    )(zrow, pos.reshape(nt, 1, TOP_K * tm), h)


def _expert_full_kernel(nfull_ref, frow_ref, fexp_ref, x_ref, wg_ref, wu_ref, wd_ref, o_ref, h_ref, acc_ref,
                        *, s):
    del frow_ref, fexp_ref
    j = pl.program_id(0)
    f = pl.program_id(1)
    active = j < nfull_ref[0]
    rows = h_ref.shape[0]

    @pl.when(jnp.logical_not(active) & (f == 0))
    def _():
        o_ref[...] = jnp.zeros_like(o_ref)

    last = pl.num_programs(1) - 1
    group = rows // 4

    def chain(h):
        a = jnp.dot(h, wg_ref[0], preferred_element_type=F32)
        u = jnp.dot(h, wu_ref[0], preferred_element_type=F32)
        act = (_silu(a) * u).astype(BF16)
        return jnp.dot(act, wd_ref[0], preferred_element_type=F32)

    @pl.when(active & (f == 0))
    def _():
        for r0 in range(0, rows, group):
            hg = _load_row_tiles(x_ref, r0, group, s).astype(BF16)
            h_ref[r0:r0 + group] = hg
            acc_ref[r0:r0 + group] = chain(hg)

    @pl.when(active & (f > 0) & (f < last))
    def _():
        acc_ref[...] += chain(h_ref[...])

    @pl.when(active & (f == last))
    def _():
        for r0 in range(0, rows, group):
            _store_row_tiles(o_ref, r0, acc_ref[r0:r0 + group] + chain(h_ref[r0:r0 + group]))


def _expert_tail_kernel(trow_ref, tsub_ref, x_ref, wg_ref, wu_ref, wd_ref, prev_ref, o_ref, h_ref, acc_ref,
                        *, sub, nsub_max, s):
    del trow_ref, prev_ref
    f = pl.program_id(1)
    nsub = tsub_ref[pl.program_id(0)]

    @pl.when(f == 0)
    def _():
        o_ref[...] = jnp.zeros_like(o_ref)
        acc_ref[...] = jnp.zeros_like(acc_ref)

    for sb in range(nsub_max):
        @pl.when(sb < nsub)
        def _():
            rows = slice(sb * sub, (sb + 1) * sub)

            @pl.when(f == 0)
            def _():
                h_ref[rows] = _load_row_tiles(x_ref, sb * sub, sub, s).astype(BF16)

            h = h_ref[rows]
            a = jnp.dot(h, wg_ref[0], preferred_element_type=F32)
            u = jnp.dot(h, wu_ref[0], preferred_element_type=F32)
            act = (_silu(a) * u).astype(BF16)
            acc_ref[rows] += jnp.dot(act, wd_ref[0], preferred_element_type=F32)

            @pl.when(f == pl.num_programs(1) - 1)
            def _():
                _store_row_tiles(o_ref, sb * sub, acc_ref[rows])


def expert_ffn(xs, nfull, frow, fexp, trow, tsub, wg, wu, wd):
    experts, d, fdim = wg.shape
    s = d // LANES
    tf = MOE_F_TILE
    nf = fdim // tf
    nfull_max = frow.shape[0]
    tile = (MOE_ROW_TILE * s, LANES)
    scratch = [pltpu.VMEM((MOE_ROW_TILE, d), BF16), pltpu.VMEM((MOE_ROW_TILE, d), F32)]

    def fcol_full(j, f, nfull):
        return jnp.where(j < nfull[0], f, nf - 1)

    ys = pl.pallas_call(
        functools.partial(_expert_full_kernel, s=s),
        grid_spec=pltpu.PrefetchScalarGridSpec(
            num_scalar_prefetch=3,
            grid=(nfull_max, nf),
            in_specs=[pl.BlockSpec(tile, lambda j, f, nfull, frow, fexp: (frow[j], 0)),
                      pl.BlockSpec((1, d, tf), lambda j, f, nfull, frow, fexp: (fexp[j], 0, fcol_full(j, f, nfull))),
                      pl.BlockSpec((1, d, tf), lambda j, f, nfull, frow, fexp: (fexp[j], 0, fcol_full(j, f, nfull))),
                      pl.BlockSpec((1, tf, d), lambda j, f, nfull, frow, fexp: (fexp[j], fcol_full(j, f, nfull), 0))],
            out_specs=pl.BlockSpec(tile, lambda j, f, nfull, frow, fexp: (frow[j], 0)),
            scratch_shapes=scratch),
        out_shape=jax.ShapeDtypeStruct(xs.shape, F32),
        compiler_params=_cparams("arbitrary", "arbitrary"),
        name="expert_ffn_full",
    )(nfull, frow, fexp, xs, wg, wu, wd)

    def fcol_tail(e, f, tsub):
        return jnp.where(tsub[e] > 0, f, nf - 1)

    kern = functools.partial(_expert_tail_kernel, sub=MOE_SUB, nsub_max=MOE_ROW_TILE // MOE_SUB, s=s)
    return pl.pallas_call(
        kern,
        grid_spec=pltpu.PrefetchScalarGridSpec(
            num_scalar_prefetch=2,
            grid=(experts, nf),
            in_specs=[pl.BlockSpec(tile, lambda e, f, trow, tsub: (trow[e], 0)),
                      pl.BlockSpec((1, d, tf), lambda e, f, trow, tsub: (e, 0, fcol_tail(e, f, tsub))),
                      pl.BlockSpec((1, d, tf), lambda e, f, trow, tsub: (e, 0, fcol_tail(e, f, tsub))),
                      pl.BlockSpec((1, tf, d), lambda e, f, trow, tsub: (e, fcol_tail(e, f, tsub), 0)),
                      pl.BlockSpec(memory_space=pl.ANY)],
            out_specs=pl.BlockSpec(tile, lambda e, f, trow, tsub: (trow[e], 0)),
            scratch_shapes=scratch),
        out_shape=jax.ShapeDtypeStruct(xs.shape, F32),
        input_output_aliases={6: 0},
        compiler_params=_cparams("arbitrary", "arbitrary"),
        name="expert_ffn_tail",
    )(trow, tsub, xs, wg, wu, wd, ys)


def _combine_kernel(p_ref, x_ref, route_ref, fg_ref, ys_hbm, op_ref, os_ref, gath_ref, sem, *, tm, s):
    def row_copy(src, k, t):
        return pltpu.make_async_copy(ys_hbm.at[pl.ds(pl.multiple_of(src, s), s)],
                                     gath_ref.at[k, pl.ds(pl.multiple_of(t * s, s), s)], sem)

    def issue(t, carry):
        for k in range(TOP_K):
            row_copy(p_ref[0, 0, TOP_K * t + k], k, t).start(priority=k)
        return carry

    lax.fori_loop(0, tm, issue, 0, unroll=DMA_UNROLL)

    def drain(t, carry):
        for k in range(TOP_K):
            row_copy(0, 0, 0).wait()
        return carry

    lax.fori_loop(0, tm, drain, 0, unroll=DMA_UNROLL)
    route = route_ref[...]
    x = (x_ref[...] + route[:, 2:3] * _load_row_tiles(gath_ref.at[0], 0, tm, s)
         + route[:, 3:4] * _load_row_tiles(gath_ref.at[1], 0, tm, s))
    y = _rms(x, fg_ref[...])
    op_ref[...] = y

    @pl.when(pl.program_id(0) == pl.num_programs(0) - 1)
    def _():
        os_ref[...] = y[tm - os_ref.shape[0]:]


def combine(x, route, ys, pos, final_g, tm, n_sample):
    t, d = x.shape
    nt = t // tm
    s = d // LANES
    assert n_sample <= tm
    kern = functools.partial(_combine_kernel, tm=tm, s=s)
    return pl.pallas_call(
        kern,
        grid=(nt,),
        in_specs=[pl.BlockSpec((1, 1, TOP_K * tm), lambda i: (i, 0, 0), memory_space=pltpu.SMEM),
                  pl.BlockSpec((tm, d), lambda i: (i, 0)),
                  pl.BlockSpec((tm, LANES), lambda i: (i, 0)),
                  pl.BlockSpec((1, d), lambda i: (0, 0)),
                  pl.BlockSpec(memory_space=pl.ANY)],
        out_specs=[pl.BlockSpec((tm, d), lambda i: (i, 0)),
                   pl.BlockSpec((n_sample, d), lambda i: (0, 0))],
        out_shape=[jax.ShapeDtypeStruct((t - n_sample, d), F32),
                   jax.ShapeDtypeStruct((n_sample, d), F32)],
        scratch_shapes=[pltpu.VMEM((TOP_K, tm * s, LANES), F32),
                        pltpu.SemaphoreType.DMA(())],
        compiler_params=_cparams("arbitrary"),
        name="moe_combine",
    )(pos.reshape(nt, 1, TOP_K * tm), x, route, final_g.reshape(1, d), ys)


def moe_residual_final_norm(a, w_out, x_in, norm_g, router_w, wg, wu, wd, final_g, tm, n_sample):
    t, d = x_in.shape
    experts = router_w.shape[1]
    x, h, route, cnt = outproj_router(a, w_out, x_in, norm_g, router_w, tm)
    counts = cnt[0, :experts].astype(I32)
    ntiles = (TOP_K * t + experts * (MOE_ROW_TILE - 1)) // MOE_ROW_TILE + 1
    nfull_max = max(TOP_K * t // MOE_ROW_TILE, 1)
    s = d // LANES
    frow, fexp, nfull, trow, tsub, zrow, pos = plan(counts, route, ntiles, nfull_max, s)
    pos = pos[:, 4:4 + TOP_K]
    xs = dispatch(h, pos, zrow, ntiles * MOE_ROW_TILE, tm, s)
    ys = expert_ffn(xs, nfull, frow, fexp, trow, tsub, wg, wu, wd)
    return combine(x, route, ys, pos, final_g, tm, n_sample)


def _rope_tables(pos, half):
    inv = jnp.power(ROPE_THETA, -jnp.arange(half, dtype=F32) / half)
    ang = pos[:, None] * inv[None, :]
    return jnp.cos(ang), jnp.sin(ang)


def kernel(x_prompt, x_sample, state_retention, state_hgrn, norm_mix_g, norm_ffn_g, final_norm_g, ret_w_in, ret_gn_g, ret_w_out, hg_w_in, hg_lb_param, hg_norm_g, hg_w_out, ffn_w_gate, ffn_w_up, ffn_w_down, moe_router, moe_w_gate, moe_w_up, moe_w_down):
    bp, lp, d = x_prompt.shape
    bs, ls, _ = x_sample.shape
    assert ls == 1 and norm_mix_g.shape[0] == 2
    _, _, ret_heads, ret_dk, ret_dv = state_retention.shape
    _, _, hg_heads, hg_dk, _ = state_hgrn.shape
    tp = bp * lp
    t = tp + bs
    tm = _pick_tile(t, 768, 16)

    xp = x_prompt.reshape(tp, d)
    xs = x_sample.reshape(bs, d)

    log_gamma = jnp.log1p(-jnp.exp2(-5.0 - jnp.arange(ret_heads, dtype=F32)))
    cos_p, sin_p = _rope_tables(jnp.arange(lp, dtype=F32), ret_dk // 2)
    cos_s, sin_s = _rope_tables(PAST_LEN + jnp.arange(ls, dtype=F32), ret_dk // 2)
    tm_proj = _pick_tile(t, 1536, 16)
    proj = norm_matmul_merged(xp, xs, norm_mix_g[0], ret_w_in[0].astype(BF16), tm_proj,
                              _pick_tile(ret_w_in.shape[2], 1024, LANES))
    gated, ret_p = retention_scan(proj, log_gamma, cos_p, sin_p, ret_gn_g[0], batch=bp, seq=lp,
                                  heads=ret_heads, dk=ret_dk, dv=ret_dv, rows_total=t)
    gated, ret_s = retention_step(proj, gated, state_retention[0], log_gamma, cos_s, sin_s, ret_gn_g[0],
                                  row0=tp, nb=bs, heads=ret_heads, dk=ret_dk, dv=ret_dv)
    x = matmul_residual_merged(gated, ret_w_out[0].astype(BF16), xp, xs, tm)
    ff = ffn_w_gate.shape[2]
    x = ffn_residual(x, norm_ffn_g[0], ffn_w_gate[0].astype(BF16), ffn_w_up[0].astype(BF16),
                     ffn_w_down[0].astype(BF16), tm, _pick_tile(ff, 1536, LANES))

    proj = norm_matmul(x, norm_mix_g[1], hg_w_in[0].astype(BF16), tm_proj,
                       _pick_tile(hg_w_in.shape[2], 1024, LANES))
    experts, _, fe = moe_w_gate.shape[1:]
    gated, hg_p, wg, wu, wd = hgrn_scan(
        proj, hg_lb_param, hg_norm_g[0],
        (moe_w_gate[0].reshape(experts * d, fe), moe_w_up[0].reshape(experts * d, fe),
         moe_w_down[0].reshape(experts * fe, d)),
        batch=bp, seq=lp, heads=hg_heads, dk=hg_dk, rows_total=t, layer=1)
    gated, hg_s = hgrn_step(proj, gated, state_hgrn[0], hg_lb_param, hg_norm_g[0], row0=tp, nb=bs,
                            heads=hg_heads, dk=hg_dk, layer=1)
    y_p, y_s = moe_residual_final_norm(gated, hg_w_out[0].astype(BF16), x, norm_ffn_g[1], moe_router[0],
                                       wg.reshape(experts, d, fe), wu.reshape(experts, d, fe),
                                       wd.reshape(experts, fe, d), final_norm_g, tm, bs)

    return (y_p.reshape(bp, lp, d), y_s.reshape(bs, ls, d),
            ret_p[None], ret_s[None], hg_p[None], hg_s[None])
```
